```python
import math
import jax, jax.numpy as jnp
from jax import lax
import numpy as np

D_MODEL = 1024
BATCH = 8
SEQ = 4096
DEPTH = 2

N_A_LAYERS = DEPTH // 2
N_B_LAYERS = DEPTH - N_A_LAYERS
SSM_WIDTH = D_MODEL
SSM_GROUP = 16
SSM_GROUPS = SSM_WIDTH // SSM_GROUP
SSM_STATE = 64
DT_MIN = 1e-3
DT_MAX = 1e-1
N_HEADS = 8
HEAD_DIM = D_MODEL // (2 * N_HEADS)
V_DIM = 2 * HEAD_DIM
QK_WIDTH = N_HEADS * 2 * HEAD_DIM
ATTN_WIDTH = N_HEADS * V_DIM
Q_BLOCK = 128
EPS = 1e-6

kernel_name = "yoco_s5_diffattn_sandwich_adaln"


def rmsnorm(x, g):
    xf = x.astype(jnp.float32)
    y = xf * lax.rsqrt(jnp.mean(xf * xf, axis=-1, keepdims=True) + EPS)
    return (y * g.astype(jnp.float32)).astype(x.dtype)


def adaln(c, w, b):
    mod = jax.nn.silu(c) @ w + b
    shift, scale, gate = jnp.split(mod, 3, axis=-1)
    return shift[:, None, :], scale[:, None, :], gate[:, None, :]


def _linear_recurrence_op(left, right):
    a_l, b_l = left
    a_r, b_r = right
    return a_l * a_r, a_r * b_l + b_r


def s5_mixer(h, w_in, lam_re, lam_im, log_dt, b_re, b_im, c_re, c_im, d_skip, w_glu, b_glu, w_out):
    bsz, seq, _ = h.shape
    u, z = jnp.split(h @ w_in, 2, axis=-1)
    uf = u.astype(jnp.float32).reshape(bsz, seq, SSM_GROUPS, SSM_GROUP)
    lam = lax.complex(lam_re.astype(jnp.float32), lam_im.astype(jnp.float32))
    dt = jnp.exp(log_dt.astype(jnp.float32))[:, None]
    lam_bar = jnp.exp(lam * dt)
    b_mat = lax.complex(b_re.astype(jnp.float32), b_im.astype(jnp.float32))
    b_bar = ((lam_bar - 1.0) / lam)[..., None] * b_mat
    bu = jnp.einsum('blgc,gpc->blgp', uf.astype(jnp.complex64), b_bar)
    a_seq = jnp.broadcast_to(lam_bar, bu.shape)
    _, states = lax.associative_scan(_linear_recurrence_op, (a_seq, bu), axis=1)
    c_mat = lax.complex(c_re.astype(jnp.float32), c_im.astype(jnp.float32))
    y = jnp.einsum('blgp,gcp->blgc', states, c_mat).real
    y = y + d_skip.astype(jnp.float32).reshape(SSM_GROUPS, SSM_GROUP) * uf
    y = jax.nn.gelu(y.reshape(bsz, seq, SSM_WIDTH))
    y = y * jax.nn.sigmoid(y @ w_glu.astype(jnp.float32) + b_glu.astype(jnp.float32))
    y = y * jax.nn.silu(z.astype(jnp.float32))
    return y.astype(h.dtype) @ w_out


def diff_attention(h, k, v, w_in, lq1, lk1, lq2, lk2, g_sub, w_out, lambda_init):
    bsz, seq, _ = h.shape
    q, z = jnp.split(h @ w_in, [QK_WIDTH], axis=-1)
    q = q.astype(jnp.float32).reshape(bsz, seq, N_HEADS, 2, HEAD_DIM)
    lam = (jnp.exp(jnp.sum(lq1.astype(jnp.float32) * lk1.astype(jnp.float32)))
           - jnp.exp(jnp.sum(lq2.astype(jnp.float32) * lk2.astype(jnp.float32))) + lambda_init)
    n_blocks = seq // Q_BLOCK
    qb = q.reshape(bsz, n_blocks, Q_BLOCK, N_HEADS, 2, HEAD_DIM).transpose(1, 0, 2, 3, 4, 5)
    kf = k.astype(jnp.float32)
    vf = v.astype(jnp.float32)
    key_pos = jnp.arange(seq)
    scale = HEAD_DIM ** -0.5

    def block(args):
        q_blk, blk = args
        s = jnp.einsum('bqhcd,bkhcd->bhcqk', q_blk, kf) * scale
        q_pos = blk * Q_BLOCK + jnp.arange(Q_BLOCK)
        mask = key_pos[None, :] <= q_pos[:, None]
        s = jnp.where(mask, s, -jnp.inf)
        p = jax.nn.softmax(s, axis=-1)
        att = p[:, :, 0] - lam * p[:, :, 1]
        return jnp.einsum('bhqk,bkhe->bqhe', att, vf)

    o = lax.map(block, (qb, jnp.arange(n_blocks)))
    o = o.transpose(1, 0, 2, 3, 4).reshape(bsz, seq, N_HEADS, V_DIM)
    o = rmsnorm(o, g_sub) * (1.0 - lambda_init)
    o = o.reshape(bsz, seq, ATTN_WIDTH) * jax.nn.silu(z.astype(jnp.float32))
    return o.astype(h.dtype) @ w_out


def setup_inputs(seed: int = 0) -> dict:
    key = jax.random.key(seed)
    ks = jax.random.split(key, 32)
    D = D_MODEL
    nrm = lambda k, shape, s: jax.random.normal(k, shape, jnp.float32) * s
    lam_im_base = jnp.pi * jnp.arange(SSM_STATE, dtype=jnp.float32)
    return {
        "x": nrm(ks[0], (BATCH, SEQ, D), 1.0),
        "c": nrm(ks[1], (BATCH, D), 1.0),
        "ada_w": nrm(ks[2], (DEPTH, D, 3 * D), D ** -0.5),
        "ada_b": nrm(ks[3], (DEPTH, 3 * D), 0.02),
        "g_pre": 1.0 + nrm(ks[4], (DEPTH, D), 0.02),
        "g_post": 1.0 + nrm(ks[5], (DEPTH, D), 0.02),
        "a_w_in": nrm(ks[6], (N_A_LAYERS, D, 2 * SSM_WIDTH), D ** -0.5),
        "a_lam_re": -0.5 + nrm(ks[7], (N_A_LAYERS, SSM_GROUPS, SSM_STATE), 0.01),
        "a_lam_im": lam_im_base + nrm(ks[8], (N_A_LAYERS, SSM_GROUPS, SSM_STATE), 0.01),
        "a_log_dt": jax.random.uniform(ks[9], (N_A_LAYERS, SSM_GROUPS), jnp.float32,
                                       math.log(DT_MIN), math.log(DT_MAX)),
        "a_b_re": nrm(ks[10], (N_A_LAYERS, SSM_GROUPS, SSM_STATE, SSM_GROUP), (2 * SSM_GROUP) ** -0.5),
        "a_b_im": nrm(ks[11], (N_A_LAYERS, SSM_GROUPS, SSM_STATE, SSM_GROUP), (2 * SSM_GROUP) ** -0.5),
        "a_c_re": nrm(ks[12], (N_A_LAYERS, SSM_GROUPS, SSM_GROUP, SSM_STATE), (2 * SSM_STATE) ** -0.5),
        "a_c_im": nrm(ks[13], (N_A_LAYERS, SSM_GROUPS, SSM_GROUP, SSM_STATE), (2 * SSM_STATE) ** -0.5),
        "a_d": nrm(ks[14], (N_A_LAYERS, SSM_WIDTH), 1.0),
        "a_w_glu": nrm(ks[15], (N_A_LAYERS, SSM_WIDTH, SSM_WIDTH), SSM_WIDTH ** -0.5),
        "a_b_glu": nrm(ks[16], (N_A_LAYERS, SSM_WIDTH), 0.02),
        "a_w_out": nrm(ks[17], (N_A_LAYERS, SSM_WIDTH, D), SSM_WIDTH ** -0.5),
        "g_kv": 1.0 + nrm(ks[18], (D,), 0.02),
        "w_k": nrm(ks[19], (D, QK_WIDTH), D ** -0.5),
        "w_v": nrm(ks[20], (D, ATTN_WIDTH), D ** -0.5),
        "b_w_in": nrm(ks[21], (N_B_LAYERS, D, QK_WIDTH + ATTN_WIDTH), D ** -0.5),
        "b_lq1": nrm(ks[22], (N_B_LAYERS, HEAD_DIM), 0.1),
        "b_lk1": nrm(ks[23], (N_B_LAYERS, HEAD_DIM), 0.1),
        "b_lq2": nrm(ks[24], (N_B_LAYERS, HEAD_DIM), 0.1),
        "b_lk2": nrm(ks[25], (N_B_LAYERS, HEAD_DIM), 0.1),
        "b_g_sub": 1.0 + nrm(ks[26], (N_B_LAYERS, V_DIM), 0.02),
        "b_w_out": nrm(ks[27], (N_B_LAYERS, ATTN_WIDTH, D), ATTN_WIDTH ** -0.5),
    }


def reference(x, c, ada_w, ada_b, g_pre, g_post, a_w_in, a_lam_re, a_lam_im, a_log_dt,
              a_b_re, a_b_im, a_c_re, a_c_im, a_d, a_w_glu, a_b_glu, a_w_out,
              g_kv, w_k, w_v, b_w_in, b_lq1, b_lk1, b_lq2, b_lk2, b_g_sub, b_w_out):
    bsz, seq, _ = x.shape
    h = x
    k = None
    v = None
    for layer in range(DEPTH):
        shift, scale, gate = adaln(c, ada_w[layer], ada_b[layer])
        h_in = rmsnorm(h, g_pre[layer]) * (1.0 + scale) + shift
        if layer < N_A_LAYERS:
            i = layer
            y = s5_mixer(h_in, a_w_in[i], a_lam_re[i], a_lam_im[i], a_log_dt[i], a_b_re[i], a_b_im[i],
                         a_c_re[i], a_c_im[i], a_d[i], a_w_glu[i], a_b_glu[i], a_w_out[i])
        else:
            if layer == N_A_LAYERS:
                kv_in = rmsnorm(h, g_kv)
                k = (kv_in @ w_k).reshape(bsz, seq, N_HEADS, 2, HEAD_DIM)
                v = (kv_in @ w_v).reshape(bsz, seq, N_HEADS, V_DIM)
            j = layer - N_A_LAYERS
            lambda_init = 0.8 - 0.6 * math.exp(-0.3 * layer)
            y = diff_attention(h_in, k, v, b_w_in[j], b_lq1[j], b_lk1[j], b_lq2[j], b_lk2[j],
                               b_g_sub[j], b_w_out[j], lambda_init)
        h = h + gate * rmsnorm(y, g_post[layer])
    return h
```

```python
import functools
import math

import jax
import jax.numpy as jnp
from jax import lax
from jax.experimental import pallas as pl
from jax.experimental.pallas import tpu as pltpu

F32 = jnp.float32
BF16 = jnp.bfloat16

EPS = 1e-6
DEPTH = 2
SSM_GROUP = 16
SSM_STATE = 64
CHUNK = 16
GROUPS_PER_STEP = 16
N_HEADS = 8
HEAD_DIM = 64
V_DIM = 2 * HEAD_DIM
TOKEN_BLOCK = 256
GLU_BLOCK = 512
VMEM_LIMIT = 48 * 1024 * 1024


def _params(semantics):
    return pltpu.CompilerParams(dimension_semantics=semantics, vmem_limit_bytes=VMEM_LIMIT)


def _sigmoid(v):
    return 1.0 / (1.0 + jnp.exp(-v))


def _silu(v):
    return v * _sigmoid(v)


def _gelu_tanh(v):
    return 0.5 * v * (1.0 + jnp.tanh(math.sqrt(2.0 / math.pi) * (v + 0.044715 * (v * v * v))))


def _rms_rows(v, g):
    return v * lax.rsqrt(jnp.mean(v * v, axis=-1, keepdims=True) + EPS) * g


def _dot(a, b):
    return jnp.dot(a, b, preferred_element_type=F32)


def _dot_nt(a, b):
    return lax.dot_general(a, b, (((1,), (1,)), ((), ())), preferred_element_type=F32)


def _dot_tn(a, b):
    return lax.dot_general(a, b, (((0,), (0,)), ((), ())), preferred_element_type=F32)


def _modulation_kernel(c_ref, w_ref, b_ref, o_ref):
    c = c_ref[...]
    o_ref[...] = jnp.dot(_silu(c), w_ref[...], preferred_element_type=F32,
                         precision=lax.Precision.HIGHEST) + b_ref[...]


def _modulation(c, ada_w, ada_b):
    bsz, d = c.shape
    depth, _, n = ada_w.shape
    tn = 512
    return pl.pallas_call(
        _modulation_kernel,
        grid=(depth, n // tn),
        in_specs=[
            pl.BlockSpec((bsz, d), lambda l, j: (0, 0)),
            pl.BlockSpec((None, d, tn), lambda l, j: (l, 0, j)),
            pl.BlockSpec((None, 1, tn), lambda l, j: (l, 0, j)),
        ],
        out_specs=pl.BlockSpec((None, bsz, tn), lambda l, j: (l, 0, j)),
        out_shape=jax.ShapeDtypeStruct((depth, bsz, n), F32),
        compiler_params=_params(("parallel", "parallel")),
        name="modulation",
    )(c, ada_w, ada_b.reshape(depth, 1, n))


def _chunk_permutation(n):
    r = jnp.arange(n)
    src = (r % (n // CHUNK)) * CHUNK + r // (n // CHUNK)
    return (src[:, None] == r[None, :]).astype(BF16)


def _prenorm_kernel(x_ref, mod_ref, g_ref, p_ref, o_ref):
    d = x_ref.shape[-1]
    x = x_ref[...]
    shift = mod_ref[:, 0:d]
    scale = mod_ref[:, d:2 * d]
    h = _rms_rows(x, g_ref[...]) * (1.0 + scale) + shift
    hp = _dot(p_ref[...], h.astype(BF16)).astype(BF16)
    o_ref[...] = hp.reshape(o_ref.shape)


def _prenorm(x, mod0, g_pre0, perm):
    bsz, seq, d = x.shape
    tb = TOKEN_BLOCK
    return pl.pallas_call(
        _prenorm_kernel,
        grid=(bsz, seq // tb),
        in_specs=[
            pl.BlockSpec((None, tb, d), lambda b, j: (b, j, 0)),
            pl.BlockSpec((None, 1, 3 * d), lambda b, j: (b, 0, 0)),
            pl.BlockSpec((1, d), lambda b, j: (0, 0)),
            pl.BlockSpec((tb, tb), lambda b, j: (0, 0)),
        ],
        out_specs=pl.BlockSpec((None, CHUNK, tb // CHUNK, d), lambda b, j: (b, 0, j, 0)),
        out_shape=jax.ShapeDtypeStruct((bsz, CHUNK, seq // CHUNK, d), BF16),
        compiler_params=_params(("parallel", "parallel")),
        name="prenorm",
    )(x, mod0.reshape(bsz, 1, 3 * d), g_pre0.reshape(1, d), perm)


def _ssm_operators(lam_re, lam_im, log_dt, b_re, b_im, c_re, c_im, d_skip):
    g, p = lam_re.shape
    cpg = SSM_GROUP
    dt = jnp.exp(log_dt)[:, None]
    ar, ai = lam_re * dt, lam_im * dt

    def apow(k):
        k = jnp.asarray(k, F32).reshape(-1, 1, 1)
        mag = jnp.exp(k * ar)
        return mag * jnp.cos(k * ai), mag * jnp.sin(k * ai)

    a1r, a1i = jnp.exp(ar) * jnp.cos(ai), jnp.exp(ar) * jnp.sin(ai)
    den = lam_re * lam_re + lam_im * lam_im
    fr = ((a1r - 1.0) * lam_re + a1i * lam_im) / den
    fi = (a1i * lam_re - (a1r - 1.0) * lam_im) / den
    bbr = fr[..., None] * b_re - fi[..., None] * b_im
    bbi = fr[..., None] * b_im + fi[..., None] * b_re

    pr, pi = apow(jnp.arange(CHUNK))
    car = c_re[None] * pr[:, :, None, :] - c_im[None] * pi[:, :, None, :]
    cai = c_re[None] * pi[:, :, None, :] + c_im[None] * pr[:, :, None, :]
    kern = (jnp.einsum("tgcp,gpd->tgcd", car, bbr, precision=lax.Precision.HIGHEST)
            - jnp.einsum("tgcp,gpd->tgcd", cai, bbi, precision=lax.Precision.HIGHEST))
    kern = jnp.concatenate([kern, jnp.zeros_like(kern[:1])], axis=0)
    lag = jnp.arange(CHUNK)[:, None] - jnp.arange(CHUNK)[None, :]
    toep = kern[jnp.where(lag >= 0, lag, CHUNK)]
    toep = toep.transpose(2, 0, 3, 1, 4)
    eye_t = jnp.eye(CHUNK, dtype=F32)[None, :, None, :, None]
    eye_c = jnp.eye(cpg, dtype=F32)[None, None, :, None, :]
    toep = toep + eye_t * eye_c * d_skip.reshape(g, 1, cpg, 1, 1)
    toep = toep.reshape(g, CHUNK * cpg, CHUNK * cpg)

    qr, qi = apow(CHUNK - 1 - jnp.arange(CHUNK))
    sin_r = qr[..., None] * bbr[None] - qi[..., None] * bbi[None]
    sin_i = qr[..., None] * bbi[None] + qi[..., None] * bbr[None]
    state_in = jnp.stack([sin_r, sin_i], axis=0).transpose(2, 0, 3, 1, 4)
    state_in = state_in.reshape(g, 2 * p, CHUNK * cpg)

    wr, wi = apow(jnp.arange(CHUNK) + 1)
    so_r = c_re[None] * wr[:, :, None, :] - c_im[None] * wi[:, :, None, :]
    so_i = c_re[None] * wi[:, :, None, :] + c_im[None] * wr[:, :, None, :]
    state_out = jnp.stack([so_r, -so_i], axis=3).transpose(1, 0, 2, 3, 4)
    state_out = state_out.reshape(g, CHUNK * cpg, 2 * p)

    dr, di = apow(jnp.array([CHUNK]))
    decay = jnp.concatenate([dr[0], di[0]], axis=-1)
    decay = jnp.broadcast_to(decay[..., None], (g, 2 * p, 128))
    return toep.astype(BF16), state_in.astype(BF16), state_out.astype(BF16), decay


def _ssm_kernel(h_ref, wu_ref, toep_ref, sin_ref, sout_ref, decay_ref, y_ref, xs_ref):
    n_chunks = y_ref.shape[-1] // CHUNK
    rows = CHUNK * SSM_GROUP
    n_state = SSM_STATE

    for p in range(CHUNK):
        u = _dot_nt(wu_ref[...], h_ref[p * n_chunks:(p + 1) * n_chunks, :]).astype(BF16)
        for g in range(GROUPS_PER_STEP):
            xs_ref[g, p * SSM_GROUP:(p + 1) * SSM_GROUP, :] = u[g * SSM_GROUP:(g + 1) * SSM_GROUP, :]

    lane = lax.broadcasted_iota(jnp.int32, (n_state, n_chunks), 1)

    def shifted(v, s):
        return jnp.where(lane >= s, pltpu.roll(v, s, 1), 0.0)

    def group(g, carry):
        xg = xs_ref[g]
        y = _dot(toep_ref[g], xg)
        inc = _dot(sin_ref[g], xg)
        er, ei = shifted(inc[:n_state], 1), shifted(inc[n_state:], 1)
        dec = decay_ref[g]
        reps = n_chunks // 128
        ar = jnp.concatenate([dec[:n_state]] * reps, axis=1)
        ai = jnp.concatenate([dec[n_state:]] * reps, axis=1)
        s = 1
        while s < n_chunks:
            sr, si = shifted(er, s), shifted(ei, s)
            er, ei = er + (ar * sr - ai * si), ei + (ar * si + ai * sr)
            ar, ai = ar * ar - ai * ai, 2.0 * (ar * ai)
            s *= 2
        state = jnp.concatenate([er, ei], axis=0).astype(BF16)
        y = y + _dot(sout_ref[g], state)
        act = _gelu_tanh(y).astype(BF16)
        row0 = pl.multiple_of(g * SSM_GROUP, SSM_GROUP)
        for p in range(CHUNK):
            y_ref[pl.ds(row0, SSM_GROUP), p * n_chunks:(p + 1) * n_chunks] = (
                act[p * SSM_GROUP:(p + 1) * SSM_GROUP, :])
        return carry

    lax.fori_loop(0, GROUPS_PER_STEP, group, 0)


def _ssm(hperm, wu_t, toep, state_in, state_out, decay):
    bsz, seq, d = hperm.shape
    e = wu_t.shape[0]
    cb = GROUPS_PER_STEP * SSM_GROUP
    rows = CHUNK * SSM_GROUP
    n_chunks = seq // CHUNK
    gps = GROUPS_PER_STEP
    return pl.pallas_call(
        _ssm_kernel,
        grid=(bsz, e // cb),
        in_specs=[
            pl.BlockSpec((None, seq, d), lambda b, j: (b, 0, 0)),
            pl.BlockSpec((cb, d), lambda b, j: (j, 0)),
            pl.BlockSpec((gps, rows, rows), lambda b, j: (j, 0, 0)),
            pl.BlockSpec((gps, 2 * SSM_STATE, rows), lambda b, j: (j, 0, 0)),
            pl.BlockSpec((gps, rows, 2 * SSM_STATE), lambda b, j: (j, 0, 0)),
            pl.BlockSpec((gps, 2 * SSM_STATE, 128), lambda b, j: (j, 0, 0)),
        ],
        out_specs=pl.BlockSpec((None, cb, seq), lambda b, j: (b, j, 0)),
        out_shape=jax.ShapeDtypeStruct((bsz, e, seq), BF16),
        scratch_shapes=[pltpu.VMEM((gps, rows, n_chunks), BF16)],
        compiler_params=_params(("parallel", "arbitrary")),
        name="ssm",
    )(hperm, wu_t, toep, state_in, state_out, decay)


def _glu_kernel(y_ref, h_ref, wg_ref, bg_ref, wz_ref, o_ref, gt_ref):
    e = y_ref.shape[0]
    rb = 256
    ya = y_ref[...]
    hb = h_ref[...]
    for r in range(e // rb):
        rows = slice(r * rb, (r + 1) * rb)
        gl = _dot(wg_ref[rows, :], ya) + bg_ref[rows, :]
        z = _dot_nt(wz_ref[rows, :], hb)
        yr = y_ref[rows, :].astype(F32)
        gt_ref[rows, :] = yr * _sigmoid(gl) * _silu(z)
    o_ref[...] = gt_ref[...].T.astype(BF16)


def _glu(y_t, hperm, wglu_t, b_glu, wz_t):
    bsz, e, seq = y_t.shape
    d = hperm.shape[-1]
    tn = GLU_BLOCK
    return pl.pallas_call(
        _glu_kernel,
        grid=(bsz, seq // tn),
        in_specs=[
            pl.BlockSpec((None, e, tn), lambda b, j: (b, 0, j)),
            pl.BlockSpec((None, tn, d), lambda b, j: (b, j, 0)),
            pl.BlockSpec((e, e), lambda b, j: (0, 0)),
            pl.BlockSpec((e, 1), lambda b, j: (0, 0)),
            pl.BlockSpec((e, d), lambda b, j: (0, 0)),
        ],
        out_specs=pl.BlockSpec((None, tn, e), lambda b, j: (b, j, 0)),
        out_shape=jax.ShapeDtypeStruct((bsz, seq, e), BF16),
        scratch_shapes=[pltpu.VMEM((e, tn), F32)],
        compiler_params=_params(("parallel", "parallel")),
        name="glu",
    )(y_t, hperm, wglu_t, b_glu.reshape(e, 1), wz_t)


def _mid_kernel(gp_ref, x_ref, mod0_ref, mod1_ref, p_ref, wo_ref, gpost_ref, gkv_ref, gpre_ref,
                wk_ref, wv_ref, wq_ref, wz_ref, h_ref, k_ref, vt_ref, qt_ref, zt_ref):
    d = x_ref.shape[-1]
    tb = x_ref.shape[0]
    gated = _dot(p_ref[...], gp_ref[...].reshape(tb, -1)).astype(BF16)
    y = _dot(gated, wo_ref[...])
    gate0 = mod0_ref[:, 2 * d:3 * d]
    h = x_ref[...] + gate0 * _rms_rows(y, gpost_ref[...])
    h_ref[...] = h
    kv_in = _rms_rows(h, gkv_ref[...]).astype(BF16)
    k_ref[...] = _dot(kv_in, wk_ref[...]).astype(BF16)
    vt_ref[...] = _dot_nt(wv_ref[...], kv_in).astype(BF16)
    shift1 = mod1_ref[:, 0:d]
    scale1 = mod1_ref[:, d:2 * d]
    h_in = (_rms_rows(h, gpre_ref[...]) * (1.0 + scale1) + shift1).astype(BF16)
    qt_ref[...] = (_dot_nt(wq_ref[...], h_in) * (HEAD_DIM ** -0.5)).astype(BF16)
    zt_ref[...] = _dot_nt(wz_ref[...], h_in).astype(BF16)


def _mid(gated_perm, x, mod0, mod1, perm, wo, g_post0, g_kv, g_pre1, wk, wv_t, wq_t, wz_t):
    bsz, seq, d = x.shape
    tb = TOKEN_BLOCK
    nb = seq // tb
    e = gated_perm.shape[-1]
    qk = wk.shape[1]
    av = wv_t.shape[0]
    row = lambda b, j: (b, j, 0)
    const2 = lambda b, j: (0, 0)
    t_spec = lambda n: pl.BlockSpec((None, None, n, tb), lambda b, j: (b, j, 0, 0))
    return pl.pallas_call(
        _mid_kernel,
        grid=(bsz, nb),
        in_specs=[
            pl.BlockSpec((None, CHUNK, tb // CHUNK, e), lambda b, j: (b, 0, j, 0)),
            pl.BlockSpec((None, tb, d), row),
            pl.BlockSpec((None, 1, 3 * d), lambda b, j: (b, 0, 0)),
            pl.BlockSpec((None, 1, 3 * d), lambda b, j: (b, 0, 0)),
            pl.BlockSpec((tb, tb), const2),
            pl.BlockSpec((e, d), const2),
            pl.BlockSpec((1, d), const2),
            pl.BlockSpec((1, d), const2),
            pl.BlockSpec((1, d), const2),
            pl.BlockSpec((d, qk), const2),
            pl.BlockSpec((av, d), const2),
            pl.BlockSpec((qk, d), const2),
            pl.BlockSpec((av, d), const2),
        ],
        out_specs=[
            pl.BlockSpec((None, tb, d), row),
            pl.BlockSpec((None, tb, qk), row),
            t_spec(av),
            t_spec(qk),
            t_spec(av),
        ],
        out_shape=[
            jax.ShapeDtypeStruct((bsz, seq, d), F32),
            jax.ShapeDtypeStruct((bsz, seq, qk), BF16),
            jax.ShapeDtypeStruct((bsz, nb, av, tb), BF16),
            jax.ShapeDtypeStruct((bsz, nb, qk, tb), BF16),
            jax.ShapeDtypeStruct((bsz, nb, av, tb), BF16),
        ],
        compiler_params=_params(("parallel", "parallel")),
        name="mid",
    )(gated_perm.reshape(bsz, CHUNK, seq // CHUNK, e), x, mod0.reshape(bsz, 1, 3 * d),
      mod1.reshape(bsz, 1, 3 * d), perm, wo, g_post0.reshape(1, d), g_kv.reshape(1, d),
      g_pre1.reshape(1, d), wk, wv_t, wq_t, wz_t)


def _attention_kernel(lam_ref, qt_ref, k_ref, vt_ref, zt_ref, g_ref, o_ref, *, out_scale):
    tq = qt_ref.shape[-1]
    i = pl.program_id(2)
    qt = qt_ref[...]
    zero = jnp.zeros((HEAD_DIM, tq), qt.dtype)
    qpad = jnp.concatenate([jnp.concatenate([qt[:HEAD_DIM], zero], axis=1),
                            jnp.concatenate([zero, qt[HEAD_DIM:]], axis=1)], axis=0)

    def step(j, carry, masked):
        m, l, acc = carry
        kb = k_ref[pl.ds(pl.multiple_of(j * tq, tq), tq), :]
        s = _dot(kb, qpad)
        if masked:
            key = lax.broadcasted_iota(jnp.int32, s.shape, 0)
            qry = lax.broadcasted_iota(jnp.int32, s.shape, 1) % tq
            s = jnp.where(key <= qry, s, -jnp.inf)
        m_new = jnp.maximum(m, jnp.max(s, axis=0, keepdims=True))
        alpha = jnp.exp(m - m_new)
        p = jnp.exp(s - m_new)
        l = alpha * l + jnp.sum(p, axis=0, keepdims=True)
        acc = alpha * acc + _dot(vt_ref[j], p.astype(BF16))
        return m_new, l, acc

    init = (jnp.full((1, 2 * tq), -jnp.inf, F32), jnp.zeros((1, 2 * tq), F32),
            jnp.zeros((V_DIM, 2 * tq), F32))
    carry = lax.fori_loop(0, i, functools.partial(step, masked=False), init)
    m, l, acc = step(i, carry, True)
    o = acc / l
    o = o[:, :tq] - lam_ref[0] * o[:, tq:]
    o = o * lax.rsqrt(jnp.mean(o * o, axis=0, keepdims=True) + EPS) * g_ref[...] * out_scale
    o_ref[...] = (o * _silu(zt_ref[...].astype(F32))).astype(BF16)


def _attention(lam, q_t, k, v_t, z_t, g_sub, out_scale):
    bsz, nb, width, tb = q_t.shape
    seq = nb * tb
    hw = 2 * HEAD_DIM
    n_heads = width // hw
    blk = pl.BlockSpec((None, None, hw, tb), lambda b, h, i: (b, i, h, 0))
    return pl.pallas_call(
        functools.partial(_attention_kernel, out_scale=out_scale),
        grid=(bsz, n_heads, nb),
        in_specs=[
            pl.BlockSpec(memory_space=pltpu.SMEM),
            blk,
            pl.BlockSpec((None, seq, hw), lambda b, h, i: (b, 0, h)),
            pl.BlockSpec((None, nb, V_DIM, tb), lambda b, h, i: (b, 0, h, 0)),
            blk,
            pl.BlockSpec((V_DIM, 1), lambda b, h, i: (0, 0)),
        ],
        out_specs=blk,
        out_shape=jax.ShapeDtypeStruct((bsz, nb, width, tb), BF16),
        compiler_params=_params(("parallel", "parallel", "arbitrary")),
        name="attention",
    )(lam, q_t, k, v_t, z_t, g_sub.reshape(V_DIM, 1))


def _final_kernel(a_ref, h_ref, mod1_ref, wo_ref, gpost_ref, o_ref):
    d = h_ref.shape[-1]
    y = _dot_tn(a_ref[...], wo_ref[...])
    gate1 = mod1_ref[:, 2 * d:3 * d]
    o_ref[...] = h_ref[...] + gate1 * _rms_rows(y, gpost_ref[...])


def _final(att_t, h, mod1, wo, g_post1):
    bsz, seq, d = h.shape
    _, nb, width, tb = att_t.shape
    row = lambda b, j: (b, j, 0)
    return pl.pallas_call(
        _final_kernel,
        grid=(bsz, nb),
        in_specs=[
            pl.BlockSpec((None, None, width, tb), lambda b, j: (b, j, 0, 0)),
            pl.BlockSpec((None, tb, d), row),
            pl.BlockSpec((None, 1, 3 * d), lambda b, j: (b, 0, 0)),
            pl.BlockSpec((width, d), lambda b, j: (0, 0)),
            pl.BlockSpec((1, d), lambda b, j: (0, 0)),
        ],
        out_specs=pl.BlockSpec((None, tb, d), row),
        out_shape=jax.ShapeDtypeStruct((bsz, seq, d), F32),
        compiler_params=_params(("parallel", "parallel")),
        name="final",
    )(att_t, h, mod1.reshape(bsz, 1, 3 * d), wo, g_post1.reshape(1, d))


def kernel(x, c, ada_w, ada_b, g_pre, g_post, a_w_in, a_lam_re, a_lam_im, a_log_dt, a_b_re, a_b_im,
           a_c_re, a_c_im, a_d, a_w_glu, a_b_glu, a_w_out, g_kv, w_k, w_v, b_w_in, b_lq1, b_lk1,
           b_lq2, b_lk2, b_g_sub, b_w_out):
    bsz, seq, d = x.shape
    e = a_w_glu.shape[1]
    qk = w_k.shape[1]
    assert seq % (CHUNK * 128) == 0 and seq % GLU_BLOCK == 0 and d % 128 == 0
    assert e % (GROUPS_PER_STEP * SSM_GROUP) == 0

    mod = _modulation(c, ada_w, ada_b)
    perm = _chunk_permutation(TOKEN_BLOCK)

    hperm = _prenorm(x, mod[0], g_pre[0], perm).reshape(bsz, seq, d)
    toep, state_in, state_out, decay = _ssm_operators(
        a_lam_re[0], a_lam_im[0], a_log_dt[0], a_b_re[0], a_b_im[0], a_c_re[0], a_c_im[0], a_d[0])
    w_in_t = a_w_in[0].T.astype(BF16)
    y_t = _ssm(hperm, w_in_t[:e], toep, state_in, state_out, decay)
    gated = _glu(y_t, hperm, a_w_glu[0].T.astype(BF16), a_b_glu[0], w_in_t[e:])

    w_b_t = b_w_in[0].T.astype(BF16)
    h, k, v_t, q_t, z_t = _mid(
        gated, x, mod[0], mod[1], perm, a_w_out[0].astype(BF16), g_post[0], g_kv, g_pre[1],
        w_k.astype(BF16), w_v.T.astype(BF16), w_b_t[:qk], w_b_t[qk:])

    layer = DEPTH // 2
    lambda_init = 0.8 - 0.6 * math.exp(-0.3 * layer)
    lam = (jnp.exp(jnp.sum(b_lq1[0] * b_lk1[0])) - jnp.exp(jnp.sum(b_lq2[0] * b_lk2[0]))
           + lambda_init).reshape(1).astype(F32)
    att_t = _attention(lam, q_t, k, v_t, z_t, b_g_sub[0], 1.0 - lambda_init)
    return _final(att_t, h, mod[1], b_w_out[0].astype(BF16), g_post[1])
```

```python
import functools
import math

import jax
import jax.numpy as jnp
from jax import lax
from jax.experimental import pallas as pl
from jax.experimental.pallas import tpu as pltpu

F32 = jnp.float32
BF16 = jnp.bfloat16

EPS = 1e-6
DEPTH = 2
SSM_GROUP = 16
SSM_STATE = 64
CHUNK = 16
GROUPS_PER_STEP = 16
N_HEADS = 8
HEAD_DIM = 64
V_DIM = 2 * HEAD_DIM
V_ROWS = V_DIM + 16
TOKEN_BLOCK = 256
GLU_BLOCK = 512
ATT_HEADS = 8
ATT_AHEAD = 8
Q_SCALE = HEAD_DIM ** -0.5 * math.log2(math.e)
VMEM_LIMIT = 48 * 1024 * 1024


def _params(semantics):
    return pltpu.CompilerParams(dimension_semantics=semantics, vmem_limit_bytes=VMEM_LIMIT)


def _sigmoid(v):
    return 1.0 / (1.0 + jnp.exp(-v))


def _silu(v):
    return v * _sigmoid(v)


def _gelu_tanh(v):
    return 0.5 * v * (1.0 + jnp.tanh(math.sqrt(2.0 / math.pi) * (v + 0.044715 * (v * v * v))))


def _rms_rows(v, g):
    return v * lax.rsqrt(jnp.mean(v * v, axis=-1, keepdims=True) + EPS) * g


def _dot(a, b):
    return jnp.dot(a, b, preferred_element_type=F32)


def _dot_nt(a, b):
    return lax.dot_general(a, b, (((1,), (1,)), ((), ())), preferred_element_type=F32)


def _dot_tn(a, b):
    return lax.dot_general(a, b, (((0,), (0,)), ((), ())), preferred_element_type=F32)


def _modulation_kernel(c_ref, w_ref, b_ref, o_ref):
    c = c_ref[...]
    o_ref[...] = jnp.dot(_silu(c), w_ref[...], preferred_element_type=F32,
                         precision=lax.Precision.HIGHEST) + b_ref[...]


def _modulation(c, ada_w, ada_b):
    bsz, d = c.shape
    depth, _, n = ada_w.shape
    tn = 512
    return pl.pallas_call(
        _modulation_kernel,
        grid=(depth, n // tn),
        in_specs=[
            pl.BlockSpec((bsz, d), lambda l, j: (0, 0)),
            pl.BlockSpec((None, d, tn), lambda l, j: (l, 0, j)),
            pl.BlockSpec((None, 1, tn), lambda l, j: (l, 0, j)),
        ],
        out_specs=pl.BlockSpec((None, bsz, tn), lambda l, j: (l, 0, j)),
        out_shape=jax.ShapeDtypeStruct((depth, bsz, n), F32),
        compiler_params=_params(("parallel", "parallel")),
        name="modulation",
    )(c, ada_w, ada_b.reshape(depth, 1, n))


def _chunk_permutation(n):
    r = jnp.arange(n)
    src = (r % (n // CHUNK)) * CHUNK + r // (n // CHUNK)
    return (src[:, None] == r[None, :]).astype(BF16)


def _prenorm_kernel(x_ref, mod_ref, g_ref, p_ref, o_ref):
    d = x_ref.shape[-1]
    x = x_ref[...]
    shift = mod_ref[:, 0:d]
    scale = mod_ref[:, d:2 * d]
    h = _rms_rows(x, g_ref[...]) * (1.0 + scale) + shift
    hp = _dot(p_ref[...], h.astype(BF16)).astype(BF16)
    o_ref[...] = hp.reshape(o_ref.shape)


def _prenorm(x, mod0, g_pre0, perm):
    bsz, seq, d = x.shape
    tb = TOKEN_BLOCK
    return pl.pallas_call(
        _prenorm_kernel,
        grid=(bsz, seq // tb),
        in_specs=[
            pl.BlockSpec((None, tb, d), lambda b, j: (b, j, 0)),
            pl.BlockSpec((None, 1, 3 * d), lambda b, j: (b, 0, 0)),
            pl.BlockSpec((1, d), lambda b, j: (0, 0)),
            pl.BlockSpec((tb, tb), lambda b, j: (0, 0)),
        ],
        out_specs=pl.BlockSpec((None, CHUNK, tb // CHUNK, d), lambda b, j: (b, 0, j, 0)),
        out_shape=jax.ShapeDtypeStruct((bsz, CHUNK, seq // CHUNK, d), BF16),
        compiler_params=_params(("parallel", "parallel")),
        name="prenorm",
    )(x, mod0.reshape(bsz, 1, 3 * d), g_pre0.reshape(1, d), perm)


def _ssm_operators(lam_re, lam_im, log_dt, b_re, b_im, c_re, c_im, d_skip):
    g, p = lam_re.shape
    cpg = SSM_GROUP
    dt = jnp.exp(log_dt)[:, None]
    ar, ai = lam_re * dt, lam_im * dt

    def apow(k):
        k = jnp.asarray(k, F32).reshape(-1, 1, 1)
        mag = jnp.exp(k * ar)
        return mag * jnp.cos(k * ai), mag * jnp.sin(k * ai)

    a1r, a1i = jnp.exp(ar) * jnp.cos(ai), jnp.exp(ar) * jnp.sin(ai)
    den = lam_re * lam_re + lam_im * lam_im
    fr = ((a1r - 1.0) * lam_re + a1i * lam_im) / den
    fi = (a1i * lam_re - (a1r - 1.0) * lam_im) / den
    bbr = fr[..., None] * b_re - fi[..., None] * b_im
    bbi = fr[..., None] * b_im + fi[..., None] * b_re

    pr, pi = apow(jnp.arange(CHUNK))
    car = c_re[None] * pr[:, :, None, :] - c_im[None] * pi[:, :, None, :]
    cai = c_re[None] * pi[:, :, None, :] + c_im[None] * pr[:, :, None, :]
    kern = (jnp.einsum("tgcp,gpd->tgcd", car, bbr, precision=lax.Precision.HIGHEST)
            - jnp.einsum("tgcp,gpd->tgcd", cai, bbi, precision=lax.Precision.HIGHEST))
    kern = jnp.concatenate([kern, jnp.zeros_like(kern[:1])], axis=0)
    lag = jnp.arange(CHUNK)[:, None] - jnp.arange(CHUNK)[None, :]
    toep = kern[jnp.where(lag >= 0, lag, CHUNK)]
    toep = toep.transpose(2, 0, 3, 1, 4)
    eye_t = jnp.eye(CHUNK, dtype=F32)[None, :, None, :, None]
    eye_c = jnp.eye(cpg, dtype=F32)[None, None, :, None, :]
    toep = toep + eye_t * eye_c * d_skip.reshape(g, 1, cpg, 1, 1)
    toep = toep.reshape(g, CHUNK * cpg, CHUNK * cpg)

    qr, qi = apow(CHUNK - 1 - jnp.arange(CHUNK))
    sin_r = qr[..., None] * bbr[None] - qi[..., None] * bbi[None]
    sin_i = qr[..., None] * bbi[None] + qi[..., None] * bbr[None]
    state_in = jnp.stack([sin_r, sin_i], axis=0).transpose(2, 0, 3, 1, 4)
    state_in = state_in.reshape(g, 2 * p, CHUNK * cpg)

    wr, wi = apow(jnp.arange(CHUNK) + 1)
    so_r = c_re[None] * wr[:, :, None, :] - c_im[None] * wi[:, :, None, :]
    so_i = c_re[None] * wi[:, :, None, :] + c_im[None] * wr[:, :, None, :]
    state_out = jnp.stack([so_r, -so_i], axis=3).transpose(1, 0, 2, 3, 4)
    state_out = state_out.reshape(g, CHUNK * cpg, 2 * p)

    dr, di = apow(jnp.array([CHUNK]))
    decay = jnp.concatenate([dr[0], di[0]], axis=-1)
    decay = jnp.broadcast_to(decay[..., None], (g, 2 * p, 128))
    return toep.astype(BF16), state_in.astype(BF16), state_out.astype(BF16), decay


def _ssm_kernel(h_ref, wu_ref, toep_ref, sin_ref, sout_ref, decay_ref, y_ref, xs_ref):
    n_chunks = y_ref.shape[-1] // CHUNK
    rows = CHUNK * SSM_GROUP
    n_state = SSM_STATE

    for p in range(CHUNK):
        u = _dot_nt(wu_ref[...], h_ref[p * n_chunks:(p + 1) * n_chunks, :]).astype(BF16)
        for g in range(GROUPS_PER_STEP):
            xs_ref[g, p * SSM_GROUP:(p + 1) * SSM_GROUP, :] = u[g * SSM_GROUP:(g + 1) * SSM_GROUP, :]

    lane = lax.broadcasted_iota(jnp.int32, (n_state, n_chunks), 1)

    def shifted(v, s):
        return jnp.where(lane >= s, pltpu.roll(v, s, 1), 0.0)

    def group(g, carry):
        xg = xs_ref[g]
        y = _dot(toep_ref[g], xg)
        inc = _dot(sin_ref[g], xg)
        er, ei = shifted(inc[:n_state], 1), shifted(inc[n_state:], 1)
        dec = decay_ref[g]
        reps = n_chunks // 128
        ar = jnp.concatenate([dec[:n_state]] * reps, axis=1)
        ai = jnp.concatenate([dec[n_state:]] * reps, axis=1)
        s = 1
        while s < n_chunks:
            sr, si = shifted(er, s), shifted(ei, s)
            er, ei = er + (ar * sr - ai * si), ei + (ar * si + ai * sr)
            ar, ai = ar * ar - ai * ai, 2.0 * (ar * ai)
            s *= 2
        state = jnp.concatenate([er, ei], axis=0).astype(BF16)
        y = y + _dot(sout_ref[g], state)
        act = _gelu_tanh(y).astype(BF16)
        row0 = pl.multiple_of(g * SSM_GROUP, SSM_GROUP)
        for p in range(CHUNK):
            y_ref[pl.ds(row0, SSM_GROUP), p * n_chunks:(p + 1) * n_chunks] = (
                act[p * SSM_GROUP:(p + 1) * SSM_GROUP, :])
        return carry

    lax.fori_loop(0, GROUPS_PER_STEP, group, 0)


def _ssm(hperm, wu_t, toep, state_in, state_out, decay):
    bsz, seq, d = hperm.shape
    e = wu_t.shape[0]
    cb = GROUPS_PER_STEP * SSM_GROUP
    rows = CHUNK * SSM_GROUP
    n_chunks = seq // CHUNK
    gps = GROUPS_PER_STEP
    return pl.pallas_call(
        _ssm_kernel,
        grid=(bsz, e // cb),
        in_specs=[
            pl.BlockSpec((None, seq, d), lambda b, j: (b, 0, 0)),
            pl.BlockSpec((cb, d), lambda b, j: (j, 0)),
            pl.BlockSpec((gps, rows, rows), lambda b, j: (j, 0, 0)),
            pl.BlockSpec((gps, 2 * SSM_STATE, rows), lambda b, j: (j, 0, 0)),
            pl.BlockSpec((gps, rows, 2 * SSM_STATE), lambda b, j: (j, 0, 0)),
            pl.BlockSpec((gps, 2 * SSM_STATE, 128), lambda b, j: (j, 0, 0)),
        ],
        out_specs=pl.BlockSpec((None, cb, seq), lambda b, j: (b, j, 0)),
        out_shape=jax.ShapeDtypeStruct((bsz, e, seq), BF16),
        scratch_shapes=[pltpu.VMEM((gps, rows, n_chunks), BF16)],
        compiler_params=_params(("parallel", "arbitrary")),
        name="ssm",
    )(hperm, wu_t, toep, state_in, state_out, decay)


def _glu_kernel(y_ref, h_ref, wg_ref, bg_ref, wz_ref, o_ref, gt_ref):
    e = y_ref.shape[0]
    rb = 256
    ya = y_ref[...]
    hb = h_ref[...]
    for r in range(e // rb):
        rows = slice(r * rb, (r + 1) * rb)
        gl = _dot(wg_ref[rows, :], ya) + bg_ref[rows, :]
        z = _dot_nt(wz_ref[rows, :], hb)
        yr = y_ref[rows, :].astype(F32)
        gt_ref[rows, :] = yr * _sigmoid(gl) * _silu(z)
    o_ref[...] = gt_ref[...].T.astype(BF16)


def _glu(y_t, hperm, wglu_t, b_glu, wz_t):
    bsz, e, seq = y_t.shape
    d = hperm.shape[-1]
    tn = GLU_BLOCK
    return pl.pallas_call(
        _glu_kernel,
        grid=(bsz, seq // tn),
        in_specs=[
            pl.BlockSpec((None, e, tn), lambda b, j: (b, 0, j)),
            pl.BlockSpec((None, tn, d), lambda b, j: (b, j, 0)),
            pl.BlockSpec((e, e), lambda b, j: (0, 0)),
            pl.BlockSpec((e, 1), lambda b, j: (0, 0)),
            pl.BlockSpec((e, d), lambda b, j: (0, 0)),
        ],
        out_specs=pl.BlockSpec((None, tn, e), lambda b, j: (b, j, 0)),
        out_shape=jax.ShapeDtypeStruct((bsz, seq, e), BF16),
        scratch_shapes=[pltpu.VMEM((e, tn), F32)],
        compiler_params=_params(("parallel", "parallel")),
        name="glu",
    )(y_t, hperm, wglu_t, b_glu.reshape(e, 1), wz_t)


def _mid_kernel(gp_ref, x_ref, mod0_ref, mod1_ref, p_ref, wo_ref, gpost_ref, gkv_ref, gpre_ref,
                wk_ref, wv_ref, wq_ref, wz_ref, h_ref, k_ref, vt_ref, qt_ref, zt_ref):
    d = x_ref.shape[-1]
    tb = x_ref.shape[0]
    gated = _dot(p_ref[...], gp_ref[...].reshape(tb, -1)).astype(BF16)
    y = _dot(gated, wo_ref[...])
    gate0 = mod0_ref[:, 2 * d:3 * d]
    h = x_ref[...] + gate0 * _rms_rows(y, gpost_ref[...])
    h_ref[...] = h
    kv_in = _rms_rows(h, gkv_ref[...]).astype(BF16)
    k_ref[...] = _dot(kv_in, wk_ref[...]).astype(BF16)
    vt = _dot_nt(wv_ref[...], kv_in).astype(BF16)
    ones_row = (lax.broadcasted_iota(jnp.int32, (V_ROWS - V_DIM, tb), 0) == 0).astype(BF16)
    for hd in range(vt.shape[0] // V_DIM):
        vt_ref[hd * V_ROWS:hd * V_ROWS + V_DIM, :] = vt[hd * V_DIM:(hd + 1) * V_DIM, :]
        vt_ref[hd * V_ROWS + V_DIM:(hd + 1) * V_ROWS, :] = ones_row
    shift1 = mod1_ref[:, 0:d]
    scale1 = mod1_ref[:, d:2 * d]
    h_in = (_rms_rows(h, gpre_ref[...]) * (1.0 + scale1) + shift1).astype(BF16)
    qt_ref[...] = (_dot_nt(wq_ref[...], h_in) * Q_SCALE).astype(BF16)
    zt_ref[...] = _dot_nt(wz_ref[...], h_in).astype(BF16)


def _mid(gated_perm, x, mod0, mod1, perm, wo, g_post0, g_kv, g_pre1, wk, wv_t, wq_t, wz_t):
    bsz, seq, d = x.shape
    tb = TOKEN_BLOCK
    nb = seq // tb
    e = gated_perm.shape[-1]
    qk = wk.shape[1]
    av = wv_t.shape[0]
    row = lambda b, j: (b, j, 0)
    const2 = lambda b, j: (0, 0)
    t_spec = lambda n: pl.BlockSpec((None, None, n, tb), lambda b, j: (b, j, 0, 0))
    return pl.pallas_call(
        _mid_kernel,
        grid=(bsz, nb),
        in_specs=[
            pl.BlockSpec((None, CHUNK, tb // CHUNK, e), lambda b, j: (b, 0, j, 0)),
            pl.BlockSpec((None, tb, d), row),
            pl.BlockSpec((None, 1, 3 * d), lambda b, j: (b, 0, 0)),
            pl.BlockSpec((None, 1, 3 * d), lambda b, j: (b, 0, 0)),
            pl.BlockSpec((tb, tb), const2),
            pl.BlockSpec((e, d), const2),
            pl.BlockSpec((1, d), const2),
            pl.BlockSpec((1, d), const2),
            pl.BlockSpec((1, d), const2),
            pl.BlockSpec((d, qk), const2),
            pl.BlockSpec((av, d), const2),
            pl.BlockSpec((qk, d), const2),
            pl.BlockSpec((av, d), const2),
        ],
        out_specs=[
            pl.BlockSpec((None, tb, d), row),
            pl.BlockSpec((None, tb, qk), row),
            t_spec(av // V_DIM * V_ROWS),
            t_spec(qk),
            t_spec(av),
        ],
        out_shape=[
            jax.ShapeDtypeStruct((bsz, seq, d), F32),
            jax.ShapeDtypeStruct((bsz, seq, qk), BF16),
            jax.ShapeDtypeStruct((bsz, nb, av // V_DIM * V_ROWS, tb), BF16),
            jax.ShapeDtypeStruct((bsz, nb, qk, tb), BF16),
            jax.ShapeDtypeStruct((bsz, nb, av, tb), BF16),
        ],
        compiler_params=_params(("parallel", "parallel")),
        name="mid",
    )(gated_perm.reshape(bsz, CHUNK, seq // CHUNK, e), x, mod0.reshape(bsz, 1, 3 * d),
      mod1.reshape(bsz, 1, 3 * d), perm, wo, g_post0.reshape(1, d), g_kv.reshape(1, d),
      g_pre1.reshape(1, d), wk, wv_t, wq_t, wz_t)


def _attention_kernel(lam_ref, qt_ref, k_ref, vt_ref, zt_ref, g_ref, bias_ref, o_ref, qp_ref, m_ref,
                      acc_ref, s_ref, *, out_scale):
    tq = qt_ref.shape[-1]
    hw = 2 * HEAD_DIM
    heads = qt_ref.shape[0] // hw
    i = pl.program_id(2)

    zero = jnp.zeros((HEAD_DIM, tq), qt_ref.dtype)
    for h in range(heads):
        qt = qt_ref[h * hw:(h + 1) * hw, :]
        qp_ref[2 * h] = jnp.concatenate([qt[:HEAD_DIM], zero], axis=0)
        qp_ref[2 * h + 1] = jnp.concatenate([zero, qt[HEAD_DIM:]], axis=0)
    m_ref[...] = jnp.full(m_ref.shape, -jnp.inf, F32)
    acc_ref[...] = jnp.zeros(acc_ref.shape, F32)

    n_strips = 2 * heads

    def scores(j, n):
        h = n // 2
        row0 = pl.multiple_of(j * tq, tq)
        return _dot(k_ref[pl.ds(row0, tq), h * hw:(h + 1) * hw], qp_ref[n])

    def step(j, carry, last):
        for n in range(n_strips):
            s = s_ref[n % ATT_AHEAD]
            if n + ATT_AHEAD < n_strips:
                s_ref[n % ATT_AHEAD] = scores(j, n + ATT_AHEAD)
            elif not last:
                s_ref[n % ATT_AHEAD] = scores(j + 1, n + ATT_AHEAD - n_strips)
            if last:
                s = s + bias_ref[...]
            h = n // 2
            vt = vt_ref[j, h * V_ROWS:(h + 1) * V_ROWS, :]
            m_old = m_ref[n]
            m_new = jnp.maximum(m_old, jnp.max(s, axis=0, keepdims=True))
            alpha = jnp.exp2(m_old - m_new)
            p = jnp.exp2((s - m_new).astype(BF16))
            acc_ref[n] = alpha * acc_ref[n] + _dot(vt, p)
            m_ref[n] = m_new
        return carry

    for n in range(ATT_AHEAD):
        s_ref[n] = scores(0, n)
    lax.fori_loop(0, i, functools.partial(step, last=False), 0)
    step(i, 0, True)

    def normalised(n):
        return acc_ref[n, :V_DIM, :] / acc_ref[n, V_DIM:V_DIM + 1, :]

    for h in range(heads):
        o = normalised(2 * h) - lam_ref[0] * normalised(2 * h + 1)
        rows = slice(h * V_DIM, (h + 1) * V_DIM)
        o = o * lax.rsqrt(jnp.mean(o * o, axis=0, keepdims=True) + EPS) * g_ref[...] * out_scale
        o_ref[rows, :] = (o * _silu(zt_ref[rows, :].astype(F32))).astype(BF16)


def _attention(lam, q_t, k, v_t, z_t, g_sub, out_scale):
    bsz, nb, width, tb = q_t.shape
    seq = nb * tb
    gw = ATT_HEADS * 2 * HEAD_DIM
    assert width % gw == 0 and (2 * ATT_HEADS) % ATT_AHEAD == 0
    pos = jnp.arange(tb)
    bias = jnp.where(pos[:, None] <= pos[None, :], 0.0, -jnp.inf).astype(F32)
    blk = pl.BlockSpec((None, None, gw, tb), lambda b, h, i: (b, i, h, 0))
    return pl.pallas_call(
        functools.partial(_attention_kernel, out_scale=out_scale),
        grid=(bsz, width // gw, nb),
        in_specs=[
            pl.BlockSpec(memory_space=pltpu.SMEM),
            blk,
            pl.BlockSpec((None, seq, gw), lambda b, h, i: (b, 0, h)),
            pl.BlockSpec((None, nb, ATT_HEADS * V_ROWS, tb), lambda b, h, i: (b, 0, h, 0)),
            blk,
            pl.BlockSpec((V_DIM, 1), lambda b, h, i: (0, 0)),
            pl.BlockSpec((tb, tb), lambda b, h, i: (0, 0)),
        ],
        out_specs=blk,
        out_shape=jax.ShapeDtypeStruct((bsz, nb, width, tb), BF16),
        scratch_shapes=[
            pltpu.VMEM((2 * ATT_HEADS, 2 * HEAD_DIM, tb), BF16),
            pltpu.VMEM((2 * ATT_HEADS, 1, tb), F32),
            pltpu.VMEM((2 * ATT_HEADS, V_ROWS, tb), F32),
            pltpu.VMEM((ATT_AHEAD, tb, tb), F32),
        ],
        compiler_params=_params(("parallel", "parallel", "arbitrary")),
        name="attention",
    )(lam, q_t, k, v_t, z_t, g_sub.reshape(V_DIM, 1), bias)


def _final_kernel(a_ref, h_ref, mod1_ref, wo_ref, gpost_ref, o_ref):
    d = h_ref.shape[-1]
    y = _dot_tn(a_ref[...], wo_ref[...])
    gate1 = mod1_ref[:, 2 * d:3 * d]
    o_ref[...] = h_ref[...] + gate1 * _rms_rows(y, gpost_ref[...])


def _final(att_t, h, mod1, wo, g_post1):
    bsz, seq, d = h.shape
    _, nb, width, tb = att_t.shape
    row = lambda b, j: (b, j, 0)
    return pl.pallas_call(
        _final_kernel,
        grid=(bsz, nb),
        in_specs=[
            pl.BlockSpec((None, None, width, tb), lambda b, j: (b, j, 0, 0)),
            pl.BlockSpec((None, tb, d), row),
            pl.BlockSpec((None, 1, 3 * d), lambda b, j: (b, 0, 0)),
            pl.BlockSpec((width, d), lambda b, j: (0, 0)),
            pl.BlockSpec((1, d), lambda b, j: (0, 0)),
        ],
        out_specs=pl.BlockSpec((None, tb, d), row),
        out_shape=jax.ShapeDtypeStruct((bsz, seq, d), F32),
        compiler_params=_params(("parallel", "parallel")),
        name="final",
    )(att_t, h, mod1.reshape(bsz, 1, 3 * d), wo, g_post1.reshape(1, d))


def kernel(x, c, ada_w, ada_b, g_pre, g_post, a_w_in, a_lam_re, a_lam_im, a_log_dt, a_b_re, a_b_im,
           a_c_re, a_c_im, a_d, a_w_glu, a_b_glu, a_w_out, g_kv, w_k, w_v, b_w_in, b_lq1, b_lk1,
           b_lq2, b_lk2, b_g_sub, b_w_out):
    bsz, seq, d = x.shape
    e = a_w_glu.shape[1]
    qk = w_k.shape[1]
    assert seq % (CHUNK * 128) == 0 and seq % GLU_BLOCK == 0 and d % 128 == 0
    assert e % (GROUPS_PER_STEP * SSM_GROUP) == 0

    mod = _modulation(c, ada_w, ada_b)
    perm = _chunk_permutation(TOKEN_BLOCK)

    hperm = _prenorm(x, mod[0], g_pre[0], perm).reshape(bsz, seq, d)
    toep, state_in, state_out, decay = _ssm_operators(
        a_lam_re[0], a_lam_im[0], a_log_dt[0], a_b_re[0], a_b_im[0], a_c_re[0], a_c_im[0], a_d[0])
    w_in_t = a_w_in[0].T.astype(BF16)
    y_t = _ssm(hperm, w_in_t[:e], toep, state_in, state_out, decay)
    gated = _glu(y_t, hperm, a_w_glu[0].T.astype(BF16), a_b_glu[0], w_in_t[e:])

    w_b_t = b_w_in[0].T.astype(BF16)
    h, k, v_t, q_t, z_t = _mid(
        gated, x, mod[0], mod[1], perm, a_w_out[0].astype(BF16), g_post[0], g_kv, g_pre[1],
        w_k.astype(BF16), w_v.T.astype(BF16), w_b_t[:qk], w_b_t[qk:])

    layer = DEPTH // 2
    lambda_init = 0.8 - 0.6 * math.exp(-0.3 * layer)
    lam = (jnp.exp(jnp.sum(b_lq1[0] * b_lk1[0])) - jnp.exp(jnp.sum(b_lq2[0] * b_lk2[0]))
           + lambda_init).reshape(1).astype(F32)
    att_t = _attention(lam, q_t, k, v_t, z_t, b_g_sub[0], 1.0 - lambda_init)
    return _final(att_t, h, mod[1], b_w_out[0].astype(BF16), g_post[1])
```

```python
import functools
import math

import jax
import jax.numpy as jnp
from jax import lax
from jax.experimental import pallas as pl
from jax.experimental.pallas import tpu as pltpu

F32 = jnp.float32
BF16 = jnp.bfloat16

EPS = 1e-6
DEPTH = 2
SSM_GROUP = 16
SSM_STATE = 64
CHUNK = 16
GROUPS_PER_STEP = 16
SSM_UNROLL = 4
N_HEADS = 8
HEAD_DIM = 64
V_DIM = 2 * HEAD_DIM
V_ROWS = V_DIM + 16
TOKEN_BLOCK = 256
PRENORM_BLOCK = 1024
FINAL_BLOCKS = 2
GLU_BLOCK = 512
ATT_HEADS = 8
ATT_AHEAD = 8
Q_SCALE = HEAD_DIM ** -0.5 * math.log2(math.e)
VMEM_LIMIT = 48 * 1024 * 1024


def _params(semantics):
    return pltpu.CompilerParams(dimension_semantics=semantics, vmem_limit_bytes=VMEM_LIMIT)


def _sigmoid(v):
    return 1.0 / (1.0 + jnp.exp(-v))


def _silu(v):
    return v * _sigmoid(v)


def _gelu_tanh(v):
    k = -2.0 * math.sqrt(2.0 / math.pi) * math.log2(math.e)
    return v / (1.0 + jnp.exp2(v * ((k * 0.044715) * (v * v) + k)))


def _rms_rows(v, g):
    return v * lax.rsqrt(jnp.mean(v * v, axis=-1, keepdims=True) + EPS) * g


def _dot(a, b):
    return jnp.dot(a, b, preferred_element_type=F32)


def _dot_nt(a, b):
    return lax.dot_general(a, b, (((1,), (1,)), ((), ())), preferred_element_type=F32)


def _dot_tn(a, b):
    return lax.dot_general(a, b, (((0,), (0,)), ((), ())), preferred_element_type=F32)


def _modulation_kernel(c_ref, w_ref, b_ref, o_ref):
    c = c_ref[...]
    o_ref[...] = jnp.dot(_silu(c), w_ref[...], preferred_element_type=F32,
                         precision=lax.Precision.HIGHEST) + b_ref[...]


def _modulation(c, ada_w, ada_b):
    bsz, d = c.shape
    depth, _, n = ada_w.shape
    tn = 512
    return pl.pallas_call(
        _modulation_kernel,
        grid=(depth, n // tn),
        in_specs=[
            pl.BlockSpec((bsz, d), lambda l, j: (0, 0)),
            pl.BlockSpec((None, d, tn), lambda l, j: (l, 0, j)),
            pl.BlockSpec((None, 1, tn), lambda l, j: (l, 0, j)),
        ],
        out_specs=pl.BlockSpec((None, bsz, tn), lambda l, j: (l, 0, j)),
        out_shape=jax.ShapeDtypeStruct((depth, bsz, n), F32),
        compiler_params=_params(("parallel", "parallel")),
        name="modulation",
    )(c, ada_w, ada_b.reshape(depth, 1, n))


def _chunk_permutation(n):
    r = jnp.arange(n)
    src = (r % (n // CHUNK)) * CHUNK + r // (n // CHUNK)
    return (src[:, None] == r[None, :]).astype(BF16)


def _prenorm_kernel(x_ref, mod_ref, g_ref, p_ref, o_ref):
    d = x_ref.shape[-1]
    sub = p_ref.shape[0]
    shift = mod_ref[:, 0:d]
    scale = mod_ref[:, d:2 * d]
    for r in range(x_ref.shape[0] // sub):
        x = x_ref[r * sub:(r + 1) * sub, :]
        h = _rms_rows(x, g_ref[...]) * (1.0 + scale) + shift
        hp = _dot(p_ref[...], h.astype(BF16)).astype(BF16)
        o_ref[:, r * (sub // CHUNK):(r + 1) * (sub // CHUNK), :] = hp.reshape(CHUNK, sub // CHUNK, d)


def _prenorm(x, mod0, g_pre0, perm):
    bsz, seq, d = x.shape
    tb = PRENORM_BLOCK
    return pl.pallas_call(
        _prenorm_kernel,
        grid=(bsz, seq // tb),
        in_specs=[
            pl.BlockSpec((None, tb, d), lambda b, j: (b, j, 0)),
            pl.BlockSpec((None, 1, 3 * d), lambda b, j: (b, 0, 0)),
            pl.BlockSpec((1, d), lambda b, j: (0, 0)),
            pl.BlockSpec(perm.shape, lambda b, j: (0, 0)),
        ],
        out_specs=pl.BlockSpec((None, CHUNK, tb // CHUNK, d), lambda b, j: (b, 0, j, 0)),
        out_shape=jax.ShapeDtypeStruct((bsz, CHUNK, seq // CHUNK, d), BF16),
        compiler_params=_params(("parallel", "parallel")),
        name="prenorm",
    )(x, mod0.reshape(bsz, 1, 3 * d), g_pre0.reshape(1, d), perm)


def _operators_kernel(lam_re_ref, lam_im_ref, log_dt_ref, bt_re_ref, bt_im_ref, c_re_ref, c_im_ref,
                      d_ref, toep_ref, sin_ref, sout_ref, decay_ref):
    rows = CHUNK * SSM_GROUP
    lanes = 2 * SSM_STATE
    lam_re, lam_im, dt = lam_re_ref[...], lam_im_ref[...], jnp.exp(log_dt_ref[...])
    ar, ai = lam_re * dt, lam_im * dt

    def apow(k):
        mag = jnp.exp(k * ar)
        return mag * jnp.cos(k * ai), mag * jnp.sin(k * ai)

    pos = lax.broadcasted_iota(jnp.int32, (CHUNK, 1), 0).astype(F32)
    a1r, a1i = apow(jnp.ones((1, 1), F32))
    den = lam_re * lam_re + lam_im * lam_im
    fr = ((a1r - 1.0) * lam_re + a1i * lam_im) / den
    fi = (a1i * lam_re - (a1r - 1.0) * lam_im) / den
    bbr = fr * bt_re_ref[...] - fi * bt_im_ref[...]
    bbi = fr * bt_im_ref[...] + fi * bt_re_ref[...]

    r_idx = lax.broadcasted_iota(jnp.int32, (rows, CHUNK), 0)
    k_idx = lax.broadcasted_iota(jnp.int32, (rows, CHUNK), 1)
    rep = (r_idx // SSM_GROUP == k_idx).astype(BF16)
    tile = (r_idx % SSM_GROUP == k_idx).astype(BF16)
    lane_tile = (lax.broadcasted_iota(jnp.int32, (SSM_GROUP, rows), 1) % SSM_GROUP
                 == lax.broadcasted_iota(jnp.int32, (SSM_GROUP, rows), 0)).astype(BF16)

    def split(v):
        hi = v.astype(BF16)
        return hi, (v - hi.astype(F32)).astype(BF16)

    def expand(sel, v):
        hi, lo = split(v)
        return _dot(sel, hi) + _dot(sel, lo)

    def dot3(x, y):
        (xh, xl), (yh, yl) = split(x), split(y)
        return _dot_nt(xh, yh) + _dot_nt(xh, yl) + _dot_nt(xl, yh)

    def times(xr, xi, yr, yi):
        return xr * yr - xi * yi, xr * yi + xi * yr

    left = lax.broadcasted_iota(jnp.int32, (rows, lanes), 1) < SSM_STATE
    halves = lambda v: (jnp.where(left, v, 0.0), jnp.where(left, 0.0, v))

    qr, qi = times(*apow(CHUNK - 1.0 - pos), fr, fi)
    sr, si = times(expand(rep, qr), expand(rep, qi),
                   expand(tile, bt_re_ref[...]), expand(tile, bt_im_ref[...]))
    cr, ci = expand(tile, c_re_ref[...]), expand(tile, c_im_ref[...])
    wr, wi = apow(pos + 1.0)
    our, oui = times(cr, ci, expand(rep, wr), expand(rep, wi))
    for h, (s_r, s_i, o_r, o_i) in enumerate(zip(halves(sr), halves(si), halves(our), halves(oui))):
        sin_ref[h, 0] = s_r.astype(BF16)
        sin_ref[h, 1] = s_i.astype(BF16)
        sout_ref[h, 0] = o_r.astype(BF16)
        sout_ref[h, 1] = (-o_i).astype(BF16)

    pr, pi = apow(pos)
    lr, li = times(cr, ci, expand(rep, pr), expand(rep, pi))
    lane_blk = lax.broadcasted_iota(jnp.int32, (rows, rows), 1) // SSM_GROUP
    diag = (lax.broadcasted_iota(jnp.int32, (rows, rows), 0)
            == lax.broadcasted_iota(jnp.int32, (rows, rows), 1))
    for h, (l_r, l_i) in enumerate(zip(halves(lr), halves(li))):
        kern = dot3(l_r, bbr) - dot3(l_i, bbi)
        k_hi, k_lo = split(kern)
        wide = _dot(k_hi, lane_tile) + _dot(k_lo, lane_tile)
        toep = jnp.where(diag, d_ref[h], 0.0)
        for p in range(CHUNK):
            n = p * SSM_GROUP
            delayed = wide if p == 0 else jnp.concatenate(
                [jnp.zeros((n, rows), F32), wide[:rows - n]], axis=0)
            toep = toep + jnp.where(lane_blk == p, delayed, 0.0)
        toep_ref[h] = toep.astype(BF16)

    dr, di = apow(jnp.full((1, 1), float(CHUNK), F32))
    sub = lax.broadcasted_iota(jnp.int32, (8, lanes), 0)
    decay_ref[...] = jnp.where(sub == 0, dr, jnp.where(sub == 1, di, 0.0))


def _ssm_operators(lam_re, lam_im, log_dt, b_re, b_im, c_re, c_im, d_skip):
    g, p = lam_re.shape
    cpg = SSM_GROUP
    rows = CHUNK * cpg
    pairs = g // 2
    lanes = 2 * p
    row_pair = lambda v: v.reshape(pairs, 1, lanes)
    mat_pair = lambda m: m.reshape(pairs, 2, cpg, p).transpose(0, 2, 1, 3).reshape(pairs, cpg, lanes)
    vec = pl.BlockSpec((None, 1, lanes), lambda q: (q, 0, 0))
    mat = pl.BlockSpec((None, cpg, lanes), lambda q: (q, 0, 0))
    return pl.pallas_call(
        _operators_kernel,
        grid=(pairs,),
        in_specs=[vec, vec, vec, mat, mat, mat, mat,
                  pl.BlockSpec((2, 1, rows), lambda q: (q, 0, 0))],
        out_specs=[
            pl.BlockSpec((2, rows, rows), lambda q: (q, 0, 0)),
            pl.BlockSpec((2, 2, rows, lanes), lambda q: (q, 0, 0, 0)),
            pl.BlockSpec((2, 2, rows, lanes), lambda q: (q, 0, 0, 0)),
            pl.BlockSpec((None, 8, lanes), lambda q: (q, 0, 0)),
        ],
        out_shape=[
            jax.ShapeDtypeStruct((g, rows, rows), BF16),
            jax.ShapeDtypeStruct((g, 2, rows, lanes), BF16),
            jax.ShapeDtypeStruct((g, 2, rows, lanes), BF16),
            jax.ShapeDtypeStruct((pairs, 8, lanes), F32),
        ],
        compiler_params=_params(("parallel",)),
        name="operators",
    )(row_pair(lam_re), row_pair(lam_im),
      row_pair(jnp.broadcast_to(log_dt[:, None], (g, p))),
      mat_pair(b_re.transpose(0, 2, 1)), mat_pair(b_im.transpose(0, 2, 1)),
      mat_pair(c_re), mat_pair(c_im),
      jnp.tile(d_skip.reshape(g, 1, cpg), (1, 1, CHUNK)))


def _ssm_kernel(h_ref, wu_ref, toep_ref, sin_ref, sout_ref, decay_ref, y_ref, xs_ref):
    n_chunks = y_ref.shape[-1] // CHUNK
    rows = CHUNK * SSM_GROUP
    n_state = SSM_STATE

    for p in range(CHUNK):
        u = _dot_nt(wu_ref[...], h_ref[p * n_chunks:(p + 1) * n_chunks, :]).astype(BF16)
        for g in range(GROUPS_PER_STEP):
            xs_ref[g, p * SSM_GROUP:(p + 1) * SSM_GROUP, :] = u[g * SSM_GROUP:(g + 1) * SSM_GROUP, :]

    row = lax.broadcasted_iota(jnp.int32, (n_chunks, 2 * n_state), 0)

    def shift_rows(v, s):
        if s % 8 == 0:
            return jnp.concatenate([jnp.zeros((s, v.shape[1]), v.dtype), v[:-s]], axis=0)
        return jnp.where(row >= s, pltpu.roll(v, s, 0), 0.0)

    def pair(q, carry):
        g0, g1 = 2 * q, 2 * q + 1
        x0, x1 = xs_ref[g0], xs_ref[g1]
        inc_r = _dot_tn(x0, sin_ref[g0, 0]) + _dot_tn(x1, sin_ref[g1, 0])
        inc_i = _dot_tn(x0, sin_ref[g0, 1]) + _dot_tn(x1, sin_ref[g1, 1])
        er, ei = shift_rows(inc_r, 1), shift_rows(inc_i, 1)
        ar, ai = decay_ref[q, 0:1, :], decay_ref[q, 1:2, :]
        s = 1
        while s < n_chunks:
            if s % 8 == 0:
                dr = ar * er[:-s] - ai * ei[:-s]
                di = ar * ei[:-s] + ai * er[:-s]
                er = jnp.concatenate([er[:s], er[s:] + dr], axis=0)
                ei = jnp.concatenate([ei[:s], ei[s:] + di], axis=0)
            else:
                sr, si = shift_rows(er, s), shift_rows(ei, s)
                er, ei = er + (ar * sr - ai * si), ei + (ar * si + ai * sr)
            ar, ai = ar * ar - ai * ai, 2.0 * (ar * ai)
            s *= 2
        sr, si = er.astype(BF16), ei.astype(BF16)
        for g, x in ((g0, x0), (g1, x1)):
            y = (_dot(toep_ref[g], x) + _dot_nt(sout_ref[g, 0], sr)
                 + _dot_nt(sout_ref[g, 1], si))
            act = _gelu_tanh(y).astype(BF16)
            row0 = pl.multiple_of(g * SSM_GROUP, SSM_GROUP)
            for p in range(CHUNK):
                y_ref[pl.ds(row0, SSM_GROUP), p * n_chunks:(p + 1) * n_chunks] = (
                    act[p * SSM_GROUP:(p + 1) * SSM_GROUP, :])
        return carry

    lax.fori_loop(0, GROUPS_PER_STEP // 2, pair, 0, unroll=SSM_UNROLL)


def _ssm(hperm, wu_t, toep, state_in, state_out, decay):
    bsz, seq, d = hperm.shape
    e = wu_t.shape[0]
    cb = GROUPS_PER_STEP * SSM_GROUP
    rows = CHUNK * SSM_GROUP
    n_chunks = seq // CHUNK
    gps = GROUPS_PER_STEP
    return pl.pallas_call(
        _ssm_kernel,
        grid=(bsz, e // cb),
        in_specs=[
            pl.BlockSpec((None, seq, d), lambda b, j: (b, 0, 0)),
            pl.BlockSpec((cb, d), lambda b, j: (j, 0)),
            pl.BlockSpec((gps, rows, rows), lambda b, j: (j, 0, 0)),
            pl.BlockSpec((gps, 2, rows, 2 * SSM_STATE), lambda b, j: (j, 0, 0, 0)),
            pl.BlockSpec((gps, 2, rows, 2 * SSM_STATE), lambda b, j: (j, 0, 0, 0)),
            pl.BlockSpec((gps // 2, 8, 2 * SSM_STATE), lambda b, j: (j, 0, 0)),
        ],
        out_specs=pl.BlockSpec((None, cb, seq), lambda b, j: (b, j, 0)),
        out_shape=jax.ShapeDtypeStruct((bsz, e, seq), BF16),
        scratch_shapes=[pltpu.VMEM((gps, rows, n_chunks), BF16)],
        compiler_params=_params(("parallel", "arbitrary")),
        name="ssm",
    )(hperm, wu_t, toep, state_in, state_out, decay)


def _glu_kernel(y_ref, h_ref, wg_ref, bg_ref, wz_ref, o_ref, gt_ref):
    e = y_ref.shape[0]
    rb = 256
    ya = y_ref[...]
    hb = h_ref[...]
    for r in range(e // rb):
        rows = slice(r * rb, (r + 1) * rb)
        gl = _dot(wg_ref[rows, :], ya) + bg_ref[rows, :]
        z = _dot_nt(wz_ref[rows, :], hb)
        yr = y_ref[rows, :].astype(F32)
        gt_ref[rows, :] = yr * _sigmoid(gl) * _silu(z)
    o_ref[...] = gt_ref[...].T.astype(BF16)


def _glu(y_t, hperm, wglu_t, b_glu, wz_t):
    bsz, e, seq = y_t.shape
    d = hperm.shape[-1]
    tn = GLU_BLOCK
    return pl.pallas_call(
        _glu_kernel,
        grid=(bsz, seq // tn),
        in_specs=[
            pl.BlockSpec((None, e, tn), lambda b, j: (b, 0, j)),
            pl.BlockSpec((None, tn, d), lambda b, j: (b, j, 0)),
            pl.BlockSpec((e, e), lambda b, j: (0, 0)),
            pl.BlockSpec((e, 1), lambda b, j: (0, 0)),
            pl.BlockSpec((e, d), lambda b, j: (0, 0)),
        ],
        out_specs=pl.BlockSpec((None, tn, e), lambda b, j: (b, j, 0)),
        out_shape=jax.ShapeDtypeStruct((bsz, seq, e), BF16),
        scratch_shapes=[pltpu.VMEM((e, tn), F32)],
        compiler_params=_params(("parallel", "parallel")),
        name="glu",
    )(y_t, hperm, wglu_t, b_glu.reshape(e, 1), wz_t)


def _mid_kernel(gp_ref, x_ref, mod0_ref, mod1_ref, p_ref, wo_ref, gpost_ref, gkv_ref, gpre_ref,
                wk_ref, wv_ref, wq_ref, wz_ref, h_ref, k_ref, vt_ref, qt_ref, zt_ref):
    d = x_ref.shape[-1]
    tb = x_ref.shape[0]
    gated = _dot(p_ref[...], gp_ref[...].reshape(tb, -1)).astype(BF16)
    y = _dot(gated, wo_ref[...])
    gate0 = mod0_ref[:, 2 * d:3 * d]
    h = x_ref[...] + gate0 * _rms_rows(y, gpost_ref[...])
    h_ref[...] = h
    kv_in = _rms_rows(h, gkv_ref[...]).astype(BF16)
    k_ref[...] = _dot(kv_in, wk_ref[...]).astype(BF16)
    vt = _dot_nt(wv_ref[...], kv_in).astype(BF16)
    ones_row = (lax.broadcasted_iota(jnp.int32, (V_ROWS - V_DIM, tb), 0) == 0).astype(BF16)
    for hd in range(vt.shape[0] // V_DIM):
        vt_ref[hd * V_ROWS:hd * V_ROWS + V_DIM, :] = vt[hd * V_DIM:(hd + 1) * V_DIM, :]
        vt_ref[hd * V_ROWS + V_DIM:(hd + 1) * V_ROWS, :] = ones_row
    shift1 = mod1_ref[:, 0:d]
    scale1 = mod1_ref[:, d:2 * d]
    h_in = (_rms_rows(h, gpre_ref[...]) * (1.0 + scale1) + shift1).astype(BF16)
    qt_ref[...] = (_dot_nt(wq_ref[...], h_in) * Q_SCALE).astype(BF16)
    zt_ref[...] = _dot_nt(wz_ref[...], h_in).astype(BF16)


def _mid(gated_perm, x, mod0, mod1, perm, wo, g_post0, g_kv, g_pre1, wk, wv_t, wq_t, wz_t):
    bsz, seq, d = x.shape
    tb = TOKEN_BLOCK
    nb = seq // tb
    e = gated_perm.shape[-1]
    qk = wk.shape[1]
    av = wv_t.shape[0]
    row = lambda b, j: (b, j, 0)
    const2 = lambda b, j: (0, 0)
    t_spec = lambda n: pl.BlockSpec((None, None, n, tb), lambda b, j: (b, j, 0, 0))
    return pl.pallas_call(
        _mid_kernel,
        grid=(bsz, nb),
        in_specs=[
            pl.BlockSpec((None, CHUNK, tb // CHUNK, e), lambda b, j: (b, 0, j, 0)),
            pl.BlockSpec((None, tb, d), row),
            pl.BlockSpec((None, 1, 3 * d), lambda b, j: (b, 0, 0)),
            pl.BlockSpec((None, 1, 3 * d), lambda b, j: (b, 0, 0)),
            pl.BlockSpec((tb, tb), const2),
            pl.BlockSpec((e, d), const2),
            pl.BlockSpec((1, d), const2),
            pl.BlockSpec((1, d), const2),
            pl.BlockSpec((1, d), const2),
            pl.BlockSpec((d, qk), const2),
            pl.BlockSpec((av, d), const2),
            pl.BlockSpec((qk, d), const2),
            pl.BlockSpec((av, d), const2),
        ],
        out_specs=[
            pl.BlockSpec((None, tb, d), row),
            pl.BlockSpec((None, tb, qk), row),
            t_spec(av // V_DIM * V_ROWS),
            t_spec(qk),
            t_spec(av),
        ],
        out_shape=[
            jax.ShapeDtypeStruct((bsz, seq, d), F32),
            jax.ShapeDtypeStruct((bsz, seq, qk), BF16),
            jax.ShapeDtypeStruct((bsz, nb, av // V_DIM * V_ROWS, tb), BF16),
            jax.ShapeDtypeStruct((bsz, nb, qk, tb), BF16),
            jax.ShapeDtypeStruct((bsz, nb, av, tb), BF16),
        ],
        compiler_params=_params(("parallel", "parallel")),
        name="mid",
    )(gated_perm.reshape(bsz, CHUNK, seq // CHUNK, e), x, mod0.reshape(bsz, 1, 3 * d),
      mod1.reshape(bsz, 1, 3 * d), perm, wo, g_post0.reshape(1, d), g_kv.reshape(1, d),
      g_pre1.reshape(1, d), wk, wv_t, wq_t, wz_t)


def _attention_kernel(lam_ref, qt_ref, k_ref, vt_ref, zt_ref, g_ref, bias_ref, o_ref, qp_ref, m_ref,
                      acc_ref, s_ref, *, out_scale):
    tq = qt_ref.shape[-1]
    hw = 2 * HEAD_DIM
    heads = qt_ref.shape[0] // hw
    i = pl.program_id(2)

    zero = jnp.zeros((HEAD_DIM, tq), qt_ref.dtype)
    for h in range(heads):
        qt = qt_ref[h * hw:(h + 1) * hw, :]
        qp_ref[2 * h] = jnp.concatenate([qt[:HEAD_DIM], zero], axis=0)
        qp_ref[2 * h + 1] = jnp.concatenate([zero, qt[HEAD_DIM:]], axis=0)
    m_ref[...] = jnp.full(m_ref.shape, -jnp.inf, F32)
    acc_ref[...] = jnp.zeros(acc_ref.shape, F32)

    n_strips = 2 * heads

    def scores(j, n):
        h = n // 2
        row0 = pl.multiple_of(j * tq, tq)
        return _dot(k_ref[pl.ds(row0, tq), h * hw:(h + 1) * hw], qp_ref[n])

    def step(j, carry, last):
        for n in range(n_strips):
            s = s_ref[n % ATT_AHEAD]
            if n + ATT_AHEAD < n_strips:
                s_ref[n % ATT_AHEAD] = scores(j, n + ATT_AHEAD)
            elif not last:
                s_ref[n % ATT_AHEAD] = scores(j + 1, n + ATT_AHEAD - n_strips)
            if last:
                s = s + bias_ref[...]
            m_old = m_ref[n]
            m_new = jnp.maximum(m_old, jnp.max(s, axis=0, keepdims=True))
            alpha = jnp.exp2(m_old - m_new)
            p = jnp.exp2((s - m_new).astype(BF16))
            m_ref[n] = m_new
            h = n // 2
            vt = vt_ref[j, h * V_ROWS:(h + 1) * V_ROWS, :]
            acc_ref[n] = alpha * acc_ref[n] + _dot(vt, p)
        return carry

    for n in range(ATT_AHEAD):
        s_ref[n] = scores(0, n)
    lax.fori_loop(0, i, functools.partial(step, last=False), 0)
    step(i, 0, True)

    def normalised(n):
        return acc_ref[n, :V_DIM, :] / acc_ref[n, V_DIM:V_DIM + 1, :]

    for h in range(heads):
        o = normalised(2 * h) - lam_ref[0] * normalised(2 * h + 1)
        rows = slice(h * V_DIM, (h + 1) * V_DIM)
        o = o * lax.rsqrt(jnp.mean(o * o, axis=0, keepdims=True) + EPS) * g_ref[...] * out_scale
        o_ref[rows, :] = (o * _silu(zt_ref[rows, :].astype(F32))).astype(BF16)


def _attention(lam, q_t, k, v_t, z_t, g_sub, out_scale):
    bsz, nb, width, tb = q_t.shape
    seq = nb * tb
    gw = ATT_HEADS * 2 * HEAD_DIM
    assert width % gw == 0 and (2 * ATT_HEADS) % ATT_AHEAD == 0
    pos = jnp.arange(tb)
    bias = jnp.where(pos[:, None] <= pos[None, :], 0.0, -jnp.inf).astype(F32)
    blk = pl.BlockSpec((None, None, gw, tb), lambda b, h, i: (b, i, h, 0))
    return pl.pallas_call(
        functools.partial(_attention_kernel, out_scale=out_scale),
        grid=(bsz, width // gw, nb),
        in_specs=[
            pl.BlockSpec(memory_space=pltpu.SMEM),
            blk,
            pl.BlockSpec((None, seq, gw), lambda b, h, i: (b, 0, h)),
            pl.BlockSpec((None, nb, ATT_HEADS * V_ROWS, tb), lambda b, h, i: (b, 0, h, 0)),
            blk,
            pl.BlockSpec((V_DIM, 1), lambda b, h, i: (0, 0)),
            pl.BlockSpec((tb, tb), lambda b, h, i: (0, 0)),
        ],
        out_specs=blk,
        out_shape=jax.ShapeDtypeStruct((bsz, nb, width, tb), BF16),
        scratch_shapes=[
            pltpu.VMEM((2 * ATT_HEADS, 2 * HEAD_DIM, tb), BF16),
            pltpu.VMEM((2 * ATT_HEADS, 1, tb), F32),
            pltpu.VMEM((2 * ATT_HEADS, V_ROWS, tb), F32),
            pltpu.VMEM((ATT_AHEAD, tb, tb), F32),
        ],
        compiler_params=_params(("parallel", "parallel", "arbitrary")),
        name="attention",
    )(lam, q_t, k, v_t, z_t, g_sub.reshape(V_DIM, 1), bias)


def _final_kernel(a_ref, h_ref, mod1_ref, wo_ref, gpost_ref, o_ref):
    d = h_ref.shape[-1]
    tb = a_ref.shape[-1]
    gate1 = mod1_ref[:, 2 * d:3 * d]
    for r in range(a_ref.shape[0]):
        rows = slice(r * tb, (r + 1) * tb)
        y = _dot_tn(a_ref[r], wo_ref[...])
        o_ref[rows, :] = h_ref[rows, :] + gate1 * _rms_rows(y, gpost_ref[...])


def _final(att_t, h, mod1, wo, g_post1):
    bsz, seq, d = h.shape
    _, nb, width, tq = att_t.shape
    per = FINAL_BLOCKS
    tb = per * tq
    nb = nb // per
    row = lambda b, j: (b, j, 0)
    return pl.pallas_call(
        _final_kernel,
        grid=(bsz, nb),
        in_specs=[
            pl.BlockSpec((None, per, width, tq), lambda b, j: (b, j, 0, 0)),
            pl.BlockSpec((None, tb, d), row),
            pl.BlockSpec((None, 1, 3 * d), lambda b, j: (b, 0, 0)),
            pl.BlockSpec((width, d), lambda b, j: (0, 0)),
            pl.BlockSpec((1, d), lambda b, j: (0, 0)),
        ],
        out_specs=pl.BlockSpec((None, tb, d), row),
        out_shape=jax.ShapeDtypeStruct((bsz, seq, d), F32),
        compiler_params=_params(("parallel", "parallel")),
        name="final",
    )(att_t, h, mod1.reshape(bsz, 1, 3 * d), wo, g_post1.reshape(1, d))


def kernel(x, c, ada_w, ada_b, g_pre, g_post, a_w_in, a_lam_re, a_lam_im, a_log_dt, a_b_re, a_b_im,
           a_c_re, a_c_im, a_d, a_w_glu, a_b_glu, a_w_out, g_kv, w_k, w_v, b_w_in, b_lq1, b_lk1,
           b_lq2, b_lk2, b_g_sub, b_w_out):
    bsz, seq, d = x.shape
    e = a_w_glu.shape[1]
    qk = w_k.shape[1]
    assert seq % (CHUNK * 128) == 0 and seq % GLU_BLOCK == 0 and d % 128 == 0
    assert e % (GROUPS_PER_STEP * SSM_GROUP) == 0

    mod = _modulation(c, ada_w, ada_b)
    perm = _chunk_permutation(TOKEN_BLOCK)

    hperm = _prenorm(x, mod[0], g_pre[0], perm).reshape(bsz, seq, d)
    toep, state_in, state_out, decay = _ssm_operators(
        a_lam_re[0], a_lam_im[0], a_log_dt[0], a_b_re[0], a_b_im[0], a_c_re[0], a_c_im[0], a_d[0])
    w_in_t = a_w_in[0].T.astype(BF16)
    y_t = _ssm(hperm, w_in_t[:e], toep, state_in, state_out, decay)
    gated = _glu(y_t, hperm, a_w_glu[0].T.astype(BF16), a_b_glu[0], w_in_t[e:])

    w_b_t = b_w_in[0].T.astype(BF16)
    h, k, v_t, q_t, z_t = _mid(
        gated, x, mod[0], mod[1], perm, a_w_out[0].astype(BF16), g_post[0], g_kv, g_pre[1],
        w_k.astype(BF16), w_v.T.astype(BF16), w_b_t[:qk], w_b_t[qk:])

    layer = DEPTH // 2
    lambda_init = 0.8 - 0.6 * math.exp(-0.3 * layer)
    lam = (jnp.exp(jnp.sum(b_lq1[0] * b_lk1[0])) - jnp.exp(jnp.sum(b_lq2[0] * b_lk2[0]))
           + lambda_init).reshape(1).astype(F32)
    att_t = _attention(lam, q_t, k, v_t, z_t, b_g_sub[0], 1.0 - lambda_init)
    return _final(att_t, h, mod[1], b_w_out[0].astype(BF16), g_post[1])
```

```python
import functools
import math

import jax
import jax.numpy as jnp
from jax import lax
from jax.experimental import pallas as pl
from jax.experimental.pallas import tpu as pltpu

F32 = jnp.float32
BF16 = jnp.bfloat16

EPS = 1e-6
DEPTH = 2
SSM_GROUP = 16
SSM_STATE = 64
CHUNK = 16
GROUPS_PER_STEP = 16
SSM_UNROLL = 4
N_HEADS = 8
HEAD_DIM = 64
V_DIM = 2 * HEAD_DIM
V_ROWS = V_DIM + 16
TOKEN_BLOCK = 256
PRENORM_BLOCK = 1024
GLU_BLOCK = 512
ATT_AHEAD = 8
Q_SCALE = HEAD_DIM ** -0.5 * math.log2(math.e)
VMEM_LIMIT = 48 * 1024 * 1024
ATT_VMEM_LIMIT = 56 * 1024 * 1024


def _params(semantics):
    return pltpu.CompilerParams(dimension_semantics=semantics, vmem_limit_bytes=VMEM_LIMIT)


def _sigmoid(v):
    return 1.0 / (1.0 + jnp.exp(-v))


def _silu(v):
    return v * _sigmoid(v)


def _gelu_tanh(v):
    k = -2.0 * math.sqrt(2.0 / math.pi) * math.log2(math.e)
    return v / (1.0 + jnp.exp2(v * ((k * 0.044715) * (v * v) + k)))


def _rms_rows(v, g):
    return v * lax.rsqrt(jnp.mean(v * v, axis=-1, keepdims=True) + EPS) * g


def _dot(a, b):
    return jnp.dot(a, b, preferred_element_type=F32)


def _dot_nt(a, b):
    return lax.dot_general(a, b, (((1,), (1,)), ((), ())), preferred_element_type=F32)


def _dot_tn(a, b):
    return lax.dot_general(a, b, (((0,), (0,)), ((), ())), preferred_element_type=F32)


def _modulation_kernel(c_ref, w_ref, b_ref, o_ref):
    c = c_ref[...]
    o_ref[...] = jnp.dot(_silu(c), w_ref[...], preferred_element_type=F32,
                         precision=lax.Precision.HIGHEST) + b_ref[...]


def _modulation(c, ada_w, ada_b):
    bsz, d = c.shape
    depth, _, n = ada_w.shape
    tn = 512
    return pl.pallas_call(
        _modulation_kernel,
        grid=(depth, n // tn),
        in_specs=[
            pl.BlockSpec((bsz, d), lambda l, j: (0, 0)),
            pl.BlockSpec((None, d, tn), lambda l, j: (l, 0, j)),
            pl.BlockSpec((None, 1, tn), lambda l, j: (l, 0, j)),
        ],
        out_specs=pl.BlockSpec((None, bsz, tn), lambda l, j: (l, 0, j)),
        out_shape=jax.ShapeDtypeStruct((depth, bsz, n), F32),
        compiler_params=_params(("parallel", "parallel")),
        name="modulation",
    )(c, ada_w, ada_b.reshape(depth, 1, n))


def _chunk_permutation(n):
    r = jnp.arange(n)
    src = (r % (n // CHUNK)) * CHUNK + r // (n // CHUNK)
    return (src[:, None] == r[None, :]).astype(BF16)


def _prenorm_kernel(x_ref, mod_ref, g_ref, p_ref, o_ref):
    d = x_ref.shape[-1]
    sub = p_ref.shape[0]
    shift = mod_ref[:, 0:d]
    scale = mod_ref[:, d:2 * d]
    for r in range(x_ref.shape[0] // sub):
        x = x_ref[r * sub:(r + 1) * sub, :]
        h = _rms_rows(x, g_ref[...]) * (1.0 + scale) + shift
        hp = _dot(p_ref[...], h.astype(BF16)).astype(BF16)
        o_ref[:, r * (sub // CHUNK):(r + 1) * (sub // CHUNK), :] = hp.reshape(CHUNK, sub // CHUNK, d)


def _prenorm(x, mod0, g_pre0, perm):
    bsz, seq, d = x.shape
    tb = PRENORM_BLOCK
    return pl.pallas_call(
        _prenorm_kernel,
        grid=(bsz, seq // tb),
        in_specs=[
            pl.BlockSpec((None, tb, d), lambda b, j: (b, j, 0)),
            pl.BlockSpec((None, 1, 3 * d), lambda b, j: (b, 0, 0)),
            pl.BlockSpec((1, d), lambda b, j: (0, 0)),
            pl.BlockSpec(perm.shape, lambda b, j: (0, 0)),
        ],
        out_specs=pl.BlockSpec((None, CHUNK, tb // CHUNK, d), lambda b, j: (b, 0, j, 0)),
        out_shape=jax.ShapeDtypeStruct((bsz, CHUNK, seq // CHUNK, d), BF16),
        compiler_params=_params(("parallel", "parallel")),
        name="prenorm",
    )(x, mod0.reshape(bsz, 1, 3 * d), g_pre0.reshape(1, d), perm)


def _operators_kernel(lam_re_ref, lam_im_ref, log_dt_ref, bt_re_ref, bt_im_ref, c_re_ref, c_im_ref,
                      d_ref, toep_ref, sin_ref, sout_ref, decay_ref):
    rows = CHUNK * SSM_GROUP
    lanes = 2 * SSM_STATE
    lam_re, lam_im, dt = lam_re_ref[...], lam_im_ref[...], jnp.exp(log_dt_ref[...])
    ar, ai = lam_re * dt, lam_im * dt

    def apow(k):
        mag = jnp.exp(k * ar)
        return mag * jnp.cos(k * ai), mag * jnp.sin(k * ai)

    pos = lax.broadcasted_iota(jnp.int32, (CHUNK, 1), 0).astype(F32)
    a1r, a1i = apow(jnp.ones((1, 1), F32))
    den = lam_re * lam_re + lam_im * lam_im
    fr = ((a1r - 1.0) * lam_re + a1i * lam_im) / den
    fi = (a1i * lam_re - (a1r - 1.0) * lam_im) / den
    bbr = fr * bt_re_ref[...] - fi * bt_im_ref[...]
    bbi = fr * bt_im_ref[...] + fi * bt_re_ref[...]

    r_idx = lax.broadcasted_iota(jnp.int32, (rows, CHUNK), 0)
    k_idx = lax.broadcasted_iota(jnp.int32, (rows, CHUNK), 1)
    rep = (r_idx // SSM_GROUP == k_idx).astype(BF16)
    tile = (r_idx % SSM_GROUP == k_idx).astype(BF16)
    lane_tile = (lax.broadcasted_iota(jnp.int32, (SSM_GROUP, rows), 1) % SSM_GROUP
                 == lax.broadcasted_iota(jnp.int32, (SSM_GROUP, rows), 0)).astype(BF16)

    def split(v):
        hi = v.astype(BF16)
        return hi, (v - hi.astype(F32)).astype(BF16)

    def expand(sel, v):
        hi, lo = split(v)
        return _dot(sel, hi) + _dot(sel, lo)

    def dot3(x, y):
        (xh, xl), (yh, yl) = split(x), split(y)
        return _dot_nt(xh, yh) + _dot_nt(xh, yl) + _dot_nt(xl, yh)

    def times(xr, xi, yr, yi):
        return xr * yr - xi * yi, xr * yi + xi * yr

    left = lax.broadcasted_iota(jnp.int32, (rows, lanes), 1) < SSM_STATE
    halves = lambda v: (jnp.where(left, v, 0.0), jnp.where(left, 0.0, v))

    qr, qi = times(*apow(CHUNK - 1.0 - pos), fr, fi)
    sr, si = times(expand(rep, qr), expand(rep, qi),
                   expand(tile, bt_re_ref[...]), expand(tile, bt_im_ref[...]))
    cr, ci = expand(tile, c_re_ref[...]), expand(tile, c_im_ref[...])
    wr, wi = apow(pos + 1.0)
    our, oui = times(cr, ci, expand(rep, wr), expand(rep, wi))
    for h, (s_r, s_i, o_r, o_i) in enumerate(zip(halves(sr), halves(si), halves(our), halves(oui))):
        sin_ref[h, 0] = s_r.astype(BF16)
        sin_ref[h, 1] = s_i.astype(BF16)
        sout_ref[h, 0] = o_r.astype(BF16)
        sout_ref[h, 1] = (-o_i).astype(BF16)

    pr, pi = apow(pos)
    lr, li = times(cr, ci, expand(rep, pr), expand(rep, pi))
    lane_blk = lax.broadcasted_iota(jnp.int32, (rows, rows), 1) // SSM_GROUP
    diag = (lax.broadcasted_iota(jnp.int32, (rows, rows), 0)
            == lax.broadcasted_iota(jnp.int32, (rows, rows), 1))
    for h, (l_r, l_i) in enumerate(zip(halves(lr), halves(li))):
        kern = dot3(l_r, bbr) - dot3(l_i, bbi)
        k_hi, k_lo = split(kern)
        wide = _dot(k_hi, lane_tile) + _dot(k_lo, lane_tile)
        toep = jnp.where(diag, d_ref[h], 0.0)
        for p in range(CHUNK):
            n = p * SSM_GROUP
            delayed = wide if p == 0 else jnp.concatenate(
                [jnp.zeros((n, rows), F32), wide[:rows - n]], axis=0)
            toep = toep + jnp.where(lane_blk == p, delayed, 0.0)
        toep_ref[h] = toep.astype(BF16)

    dr, di = apow(jnp.full((1, 1), float(CHUNK), F32))
    sub = lax.broadcasted_iota(jnp.int32, (8, lanes), 0)
    decay_ref[...] = jnp.where(sub == 0, dr, jnp.where(sub == 1, di, 0.0))


def _ssm_operators(lam_re, lam_im, log_dt, b_re, b_im, c_re, c_im, d_skip):
    g, p = lam_re.shape
    cpg = SSM_GROUP
    rows = CHUNK * cpg
    pairs = g // 2
    lanes = 2 * p
    row_pair = lambda v: v.reshape(pairs, 1, lanes)
    mat_pair = lambda m: m.reshape(pairs, 2, cpg, p).transpose(0, 2, 1, 3).reshape(pairs, cpg, lanes)
    vec = pl.BlockSpec((None, 1, lanes), lambda q: (q, 0, 0))
    mat = pl.BlockSpec((None, cpg, lanes), lambda q: (q, 0, 0))
    return pl.pallas_call(
        _operators_kernel,
        grid=(pairs,),
        in_specs=[vec, vec, vec, mat, mat, mat, mat,
                  pl.BlockSpec((2, 1, rows), lambda q: (q, 0, 0))],
        out_specs=[
            pl.BlockSpec((2, rows, rows), lambda q: (q, 0, 0)),
            pl.BlockSpec((2, 2, rows, lanes), lambda q: (q, 0, 0, 0)),
            pl.BlockSpec((2, 2, rows, lanes), lambda q: (q, 0, 0, 0)),
            pl.BlockSpec((None, 8, lanes), lambda q: (q, 0, 0)),
        ],
        out_shape=[
            jax.ShapeDtypeStruct((g, rows, rows), BF16),
            jax.ShapeDtypeStruct((g, 2, rows, lanes), BF16),
            jax.ShapeDtypeStruct((g, 2, rows, lanes), BF16),
            jax.ShapeDtypeStruct((pairs, 8, lanes), F32),
        ],
        compiler_params=_params(("parallel",)),
        name="operators",
    )(row_pair(lam_re), row_pair(lam_im),
      row_pair(jnp.broadcast_to(log_dt[:, None], (g, p))),
      mat_pair(b_re.transpose(0, 2, 1)), mat_pair(b_im.transpose(0, 2, 1)),
      mat_pair(c_re), mat_pair(c_im),
      jnp.tile(d_skip.reshape(g, 1, cpg), (1, 1, CHUNK)))


def _ssm_kernel(h_ref, wu_ref, toep_ref, sin_ref, sout_ref, decay_ref, y_ref, xs_ref):
    n_chunks = y_ref.shape[-1] // CHUNK
    n_state = SSM_STATE
    j = pl.program_id(1)
    last = pl.num_programs(1) - 1
    fill, drain = j % 2, (j + 1) % 2

    def project(p):
        r0 = pl.multiple_of(p * n_chunks, n_chunks)
        u = _dot_nt(wu_ref[...], h_ref[pl.ds(r0, n_chunks), :]).astype(BF16)
        c0 = pl.multiple_of(p * SSM_GROUP, SSM_GROUP)
        for g in range(GROUPS_PER_STEP):
            xs_ref[fill, g, pl.ds(c0, SSM_GROUP), :] = u[g * SSM_GROUP:(g + 1) * SSM_GROUP, :]

    row = lax.broadcasted_iota(jnp.int32, (n_chunks, 2 * n_state), 0)

    def shift_rows(v, s):
        if s % 8 == 0:
            return jnp.concatenate([jnp.zeros((s, v.shape[1]), v.dtype), v[:-s]], axis=0)
        return jnp.where(row >= s, pltpu.roll(v, s, 0), 0.0)

    def gains(q):
        g0, g1 = 2 * q, 2 * q + 1
        x0, x1 = xs_ref[drain, g0], xs_ref[drain, g1]
        inc_r = _dot_tn(x0, sin_ref[g0, 0]) + _dot_tn(x1, sin_ref[g1, 0])
        inc_i = _dot_tn(x0, sin_ref[g0, 1]) + _dot_tn(x1, sin_ref[g1, 1])
        return inc_r, inc_i

    def finish(q, inc_r, inc_i):
        g0, g1 = 2 * q, 2 * q + 1
        er, ei = shift_rows(inc_r, 1), shift_rows(inc_i, 1)
        ar, ai = decay_ref[q, 0:1, :], decay_ref[q, 1:2, :]
        s = 1
        while s < n_chunks:
            if s % 8 == 0:
                dr = ar * er[:-s] - ai * ei[:-s]
                di = ar * ei[:-s] + ai * er[:-s]
                er = jnp.concatenate([er[:s], er[s:] + dr], axis=0)
                ei = jnp.concatenate([ei[:s], ei[s:] + di], axis=0)
            else:
                sr, si = shift_rows(er, s), shift_rows(ei, s)
                er, ei = er + (ar * sr - ai * si), ei + (ar * si + ai * sr)
            ar, ai = ar * ar - ai * ai, 2.0 * (ar * ai)
            s *= 2
        sr, si = er.astype(BF16), ei.astype(BF16)
        for g in (g0, g1):
            y = (_dot(toep_ref[g], xs_ref[drain, g]) + _dot_nt(sout_ref[g, 0], sr)
                 + _dot_nt(sout_ref[g, 1], si))
            act = _gelu_tanh(y).astype(BF16)
            row0 = pl.multiple_of(g * SSM_GROUP, SSM_GROUP)
            for p in range(CHUNK):
                y_ref[pl.ds(row0, SSM_GROUP), p * n_chunks:(p + 1) * n_chunks] = (
                    act[p * SSM_GROUP:(p + 1) * SSM_GROUP, :])

    n_iter = GROUPS_PER_STEP // 2 // SSM_UNROLL
    per_iter = CHUNK // n_iter

    def body(it, carry, with_scan, with_projection):
        qs = [it * SSM_UNROLL + u for u in range(SSM_UNROLL)]
        started = [gains(q) for q in qs] if with_scan else []
        if with_projection:
            for pp in range(per_iter):
                project(it * per_iter + pp)
        for q, inc in zip(qs, started):
            finish(q, *inc)
        return carry

    @pl.when(j == 0)
    def _():
        lax.fori_loop(0, n_iter, functools.partial(body, with_scan=False, with_projection=True), 0)

    @pl.when(jnp.logical_and(j > 0, j < last))
    def _():
        lax.fori_loop(0, n_iter, functools.partial(body, with_scan=True, with_projection=True), 0)

    @pl.when(j == last)
    def _():
        lax.fori_loop(0, n_iter, functools.partial(body, with_scan=True, with_projection=False), 0)


def _ssm(hperm, wu_t, toep, state_in, state_out, decay):
    bsz, seq, d = hperm.shape
    e = wu_t.shape[0]
    cb = GROUPS_PER_STEP * SSM_GROUP
    rows = CHUNK * SSM_GROUP
    n_chunks = seq // CHUNK
    gps = GROUPS_PER_STEP
    nblk = e // cb
    assert (gps // 2) % SSM_UNROLL == 0 and CHUNK % (gps // 2 // SSM_UNROLL) == 0
    proj = lambda j: jnp.minimum(j, nblk - 1)
    scan = lambda j: jnp.maximum(j - 1, 0)
    return pl.pallas_call(
        _ssm_kernel,
        grid=(bsz, nblk + 1),
        in_specs=[
            pl.BlockSpec((None, seq, d), lambda b, j: (b, 0, 0)),
            pl.BlockSpec((cb, d), lambda b, j: (proj(j), 0)),
            pl.BlockSpec((gps, rows, rows), lambda b, j: (scan(j), 0, 0)),
            pl.BlockSpec((gps, 2, rows, 2 * SSM_STATE), lambda b, j: (scan(j), 0, 0, 0)),
            pl.BlockSpec((gps, 2, rows, 2 * SSM_STATE), lambda b, j: (scan(j), 0, 0, 0)),
            pl.BlockSpec((gps // 2, 8, 2 * SSM_STATE), lambda b, j: (scan(j), 0, 0)),
        ],
        out_specs=pl.BlockSpec((None, cb, seq), lambda b, j: (b, scan(j), 0)),
        out_shape=jax.ShapeDtypeStruct((bsz, e, seq), BF16),
        scratch_shapes=[pltpu.VMEM((2, gps, rows, n_chunks), BF16)],
        compiler_params=_params(("parallel", "arbitrary")),
        name="ssm",
    )(hperm, wu_t, toep, state_in, state_out, decay)


def _glu_kernel(y_ref, h_ref, wg_ref, bg_ref, wz_ref, o_ref, gt_ref):
    e = y_ref.shape[0]
    rb = 256
    ya = y_ref[...]
    hb = h_ref[...]
    for r in range(e // rb):
        rows = slice(r * rb, (r + 1) * rb)
        gl = _dot(wg_ref[rows, :], ya) + bg_ref[rows, :]
        z = _dot_nt(wz_ref[rows, :], hb)
        yr = y_ref[rows, :].astype(F32)
        gt_ref[rows, :] = yr * _sigmoid(gl) * _silu(z)
    o_ref[...] = gt_ref[...].T.astype(BF16)


def _glu(y_t, hperm, wglu_t, b_glu, wz_t):
    bsz, e, seq = y_t.shape
    d = hperm.shape[-1]
    tn = GLU_BLOCK
    return pl.pallas_call(
        _glu_kernel,
        grid=(bsz, seq // tn),
        in_specs=[
            pl.BlockSpec((None, e, tn), lambda b, j: (b, 0, j)),
            pl.BlockSpec((None, tn, d), lambda b, j: (b, j, 0)),
            pl.BlockSpec((e, e), lambda b, j: (0, 0)),
            pl.BlockSpec((e, 1), lambda b, j: (0, 0)),
            pl.BlockSpec((e, d), lambda b, j: (0, 0)),
        ],
        out_specs=pl.BlockSpec((None, tn, e), lambda b, j: (b, j, 0)),
        out_shape=jax.ShapeDtypeStruct((bsz, seq, e), BF16),
        scratch_shapes=[pltpu.VMEM((e, tn), F32)],
        compiler_params=_params(("parallel", "parallel")),
        name="glu",
    )(y_t, hperm, wglu_t, b_glu.reshape(e, 1), wz_t)


def _mid_kernel(gp_ref, x_ref, mod0_ref, mod1_ref, p_ref, wo_ref, gpost_ref, gkv_ref, gpre_ref,
                wk_ref, wv_ref, wq_ref, wz_ref, gsub_ref, h_ref, k_ref, vt_ref, qt_ref, gz_ref):
    d = x_ref.shape[-1]
    tb = x_ref.shape[0]
    gated = _dot(p_ref[...], gp_ref[...].reshape(tb, -1)).astype(BF16)
    y = _dot(gated, wo_ref[...])
    gate0 = mod0_ref[:, 2 * d:3 * d]
    h = x_ref[...] + gate0 * _rms_rows(y, gpost_ref[...])
    h_ref[...] = h
    kv_in = _rms_rows(h, gkv_ref[...]).astype(BF16)
    k_ref[...] = _dot(kv_in, wk_ref[...]).astype(BF16)
    vt = _dot_nt(wv_ref[...], kv_in).astype(BF16)
    ones_row = (lax.broadcasted_iota(jnp.int32, (V_ROWS - V_DIM, tb), 0) == 0).astype(BF16)
    for hd in range(vt.shape[0] // V_DIM):
        vt_ref[hd * V_ROWS:hd * V_ROWS + V_DIM, :] = vt[hd * V_DIM:(hd + 1) * V_DIM, :]
        vt_ref[hd * V_ROWS + V_DIM:(hd + 1) * V_ROWS, :] = ones_row
    shift1 = mod1_ref[:, 0:d]
    scale1 = mod1_ref[:, d:2 * d]
    h_in = (_rms_rows(h, gpre_ref[...]) * (1.0 + scale1) + shift1).astype(BF16)
    qt_ref[...] = (_dot_nt(wq_ref[...], h_in) * Q_SCALE).astype(BF16)
    gz_ref[...] = (_silu(_dot_nt(wz_ref[...], h_in)) * gsub_ref[...]).astype(BF16)


def _mid(gated_perm, x, mod0, mod1, perm, wo, g_post0, g_kv, g_pre1, wk, wv_t, wq_t, wz_t, gsub):
    bsz, seq, d = x.shape
    tb = TOKEN_BLOCK
    nb = seq // tb
    e = gated_perm.shape[-1]
    qk = wk.shape[1]
    av = wv_t.shape[0]
    row = lambda b, j: (b, j, 0)
    const2 = lambda b, j: (0, 0)
    t_spec = lambda n: pl.BlockSpec((None, None, n, tb), lambda b, j: (b, j, 0, 0))
    return pl.pallas_call(
        _mid_kernel,
        grid=(bsz, nb),
        in_specs=[
            pl.BlockSpec((None, CHUNK, tb // CHUNK, e), lambda b, j: (b, 0, j, 0)),
            pl.BlockSpec((None, tb, d), row),
            pl.BlockSpec((None, 1, 3 * d), lambda b, j: (b, 0, 0)),
            pl.BlockSpec((None, 1, 3 * d), lambda b, j: (b, 0, 0)),
            pl.BlockSpec((tb, tb), const2),
            pl.BlockSpec((e, d), const2),
            pl.BlockSpec((1, d), const2),
            pl.BlockSpec((1, d), const2),
            pl.BlockSpec((1, d), const2),
            pl.BlockSpec((d, qk), const2),
            pl.BlockSpec((av, d), const2),
            pl.BlockSpec((qk, d), const2),
            pl.BlockSpec((av, d), const2),
            pl.BlockSpec((av, 1), const2),
        ],
        out_specs=[
            pl.BlockSpec((None, tb, d), row),
            pl.BlockSpec((None, tb, qk), row),
            t_spec(av // V_DIM * V_ROWS),
            t_spec(qk),
            t_spec(av),
        ],
        out_shape=[
            jax.ShapeDtypeStruct((bsz, seq, d), F32),
            jax.ShapeDtypeStruct((bsz, seq, qk), BF16),
            jax.ShapeDtypeStruct((bsz, nb, av // V_DIM * V_ROWS, tb), BF16),
            jax.ShapeDtypeStruct((bsz, nb, qk, tb), BF16),
            jax.ShapeDtypeStruct((bsz, nb, av, tb), BF16),
        ],
        compiler_params=_params(("parallel", "parallel")),
        name="mid",
    )(gated_perm.reshape(bsz, CHUNK, seq // CHUNK, e), x, mod0.reshape(bsz, 1, 3 * d),
      mod1.reshape(bsz, 1, 3 * d), perm, wo, g_post0.reshape(1, d), g_kv.reshape(1, d),
      g_pre1.reshape(1, d), wk, wv_t, wq_t, wz_t, gsub.reshape(av, 1))


def _attention_kernel(lam_ref, qt_ref, k_ref, vt_ref, gz_ref, bias_ref, h_ref, mod1_ref, wo_ref,
                      gpost_ref, o_ref, qp_ref, m_ref, acc_ref, s_ref, og_ref):
    tq = qt_ref.shape[-1]
    hw = 2 * HEAD_DIM
    heads = qt_ref.shape[0] // hw
    i = pl.program_id(1)

    zero = jnp.zeros((HEAD_DIM, tq), qt_ref.dtype)
    for h in range(heads):
        qt = qt_ref[h * hw:(h + 1) * hw, :]
        qp_ref[2 * h] = jnp.concatenate([qt[:HEAD_DIM], zero], axis=0)
        qp_ref[2 * h + 1] = jnp.concatenate([zero, qt[HEAD_DIM:]], axis=0)
    n_strips = 2 * heads

    def scores(j, n):
        h = n // 2
        row0 = pl.multiple_of(j * tq, tq)
        return _dot(k_ref[pl.ds(row0, tq), h * hw:(h + 1) * hw], qp_ref[n])

    def step(j, carry, last):
        for n in range(n_strips):
            s = s_ref[n % ATT_AHEAD]
            if n + ATT_AHEAD < n_strips:
                s_ref[n % ATT_AHEAD] = scores(j, n + ATT_AHEAD)
            elif not last:
                s_ref[n % ATT_AHEAD] = scores(j + 1, n + ATT_AHEAD - n_strips)
            if last:
                s = s + bias_ref[...]
            m_old = m_ref[n]
            m_new = jnp.maximum(m_old, jnp.max(s, axis=0, keepdims=True))
            alpha = jnp.exp2(m_old - m_new)
            p = jnp.exp2((s - m_new).astype(BF16))
            m_ref[n] = m_new
            h = n // 2
            vt = vt_ref[j, h * V_ROWS:(h + 1) * V_ROWS, :]
            acc_ref[n] = alpha * acc_ref[n] + _dot(vt, p)
        return carry

    for n in range(ATT_AHEAD):
        s_ref[n] = scores(0, n)
    m_ref[...] = jnp.full(m_ref.shape, -jnp.inf, F32)
    acc_ref[...] = jnp.zeros(acc_ref.shape, F32)
    lax.fori_loop(0, i, functools.partial(step, last=False), 0)
    step(i, 0, True)

    for h in range(heads):
        r0 = 1.0 / acc_ref[2 * h, V_DIM:V_DIM + 1, :]
        r1 = lam_ref[0] / acc_ref[2 * h + 1, V_DIM:V_DIM + 1, :]
        o = acc_ref[2 * h, :V_DIM, :] * r0 - acc_ref[2 * h + 1, :V_DIM, :] * r1
        rows = slice(h * V_DIM, (h + 1) * V_DIM)
        inv_rms = lax.rsqrt(jnp.mean(o * o, axis=0, keepdims=True) + EPS)
        og_ref[rows, :] = (o * inv_rms * gz_ref[rows, :].astype(F32)).astype(BF16)
        if h % 2 == 1:
            pr = slice((h - 1) * V_DIM, (h + 1) * V_DIM)
            part = _dot_tn(og_ref[pr, :], wo_ref[pr, :])
            y = part if h == 1 else y + part
    d = h_ref.shape[-1]
    gate1 = mod1_ref[:, 2 * d:3 * d]
    o_ref[...] = h_ref[...] + gate1 * _rms_rows(y, gpost_ref[...])


def _attention(lam, q_t, k, v_t, gz_t, h, mod1, wo, g_post1):
    bsz, nb, width, tb = q_t.shape
    seq = nb * tb
    d = h.shape[-1]
    heads = width // (2 * HEAD_DIM)
    assert heads % 2 == 0 and (2 * heads) % ATT_AHEAD == 0
    pos = jnp.arange(tb)
    bias = jnp.where(pos[:, None] <= pos[None, :], 0.0, -jnp.inf).astype(F32)
    blk = pl.BlockSpec((None, None, width, tb), lambda b, i: (b, i, 0, 0))
    once = pl.Buffered(1)
    return pl.pallas_call(
        _attention_kernel,
        grid=(bsz, nb),
        in_specs=[
            pl.BlockSpec(memory_space=pltpu.SMEM),
            blk,
            pl.BlockSpec((None, seq, width), lambda b, i: (b, 0, 0)),
            pl.BlockSpec((None, nb, heads * V_ROWS, tb), lambda b, i: (b, 0, 0, 0)),
            blk,
            pl.BlockSpec((tb, tb), lambda b, i: (0, 0), pipeline_mode=once),
            pl.BlockSpec((None, tb, d), lambda b, i: (b, i, 0)),
            pl.BlockSpec((None, 1, 3 * d), lambda b, i: (b, 0, 0)),
            pl.BlockSpec((width, d), lambda b, i: (0, 0), pipeline_mode=once),
            pl.BlockSpec((1, d), lambda b, i: (0, 0), pipeline_mode=once),
        ],
        out_specs=pl.BlockSpec((None, tb, d), lambda b, i: (b, i, 0)),
        out_shape=jax.ShapeDtypeStruct((bsz, seq, d), F32),
        scratch_shapes=[
            pltpu.VMEM((2 * heads, 2 * HEAD_DIM, tb), BF16),
            pltpu.VMEM((2 * heads, 1, tb), F32),
            pltpu.VMEM((2 * heads, V_ROWS, tb), F32),
            pltpu.VMEM((ATT_AHEAD, tb, tb), F32),
            pltpu.VMEM((width, tb), BF16),
        ],
        compiler_params=pltpu.CompilerParams(dimension_semantics=("parallel", "arbitrary"),
                                             vmem_limit_bytes=ATT_VMEM_LIMIT),
        name="attention",
    )(lam, q_t, k, v_t, gz_t, bias, h, mod1.reshape(bsz, 1, 3 * d), wo, g_post1.reshape(1, d))


def kernel(x, c, ada_w, ada_b, g_pre, g_post, a_w_in, a_lam_re, a_lam_im, a_log_dt, a_b_re, a_b_im,
           a_c_re, a_c_im, a_d, a_w_glu, a_b_glu, a_w_out, g_kv, w_k, w_v, b_w_in, b_lq1, b_lk1,
           b_lq2, b_lk2, b_g_sub, b_w_out):
    bsz, seq, d = x.shape
    e = a_w_glu.shape[1]
    qk = w_k.shape[1]
    assert seq % (CHUNK * 128) == 0 and seq % GLU_BLOCK == 0 and d % 128 == 0
    assert e % (GROUPS_PER_STEP * SSM_GROUP) == 0

    mod = _modulation(c, ada_w, ada_b)
    perm = _chunk_permutation(TOKEN_BLOCK)

    hperm = _prenorm(x, mod[0], g_pre[0], perm).reshape(bsz, seq, d)
    toep, state_in, state_out, decay = _ssm_operators(
        a_lam_re[0], a_lam_im[0], a_log_dt[0], a_b_re[0], a_b_im[0], a_c_re[0], a_c_im[0], a_d[0])
    w_in_t = a_w_in[0].T.astype(BF16)
    y_t = _ssm(hperm, w_in_t[:e], toep, state_in, state_out, decay)
    gated = _glu(y_t, hperm, a_w_glu[0].T.astype(BF16), a_b_glu[0], w_in_t[e:])

    layer = DEPTH // 2
    lambda_init = 0.8 - 0.6 * math.exp(-0.3 * layer)
    w_b_t = b_w_in[0].T.astype(BF16)
    gsub = jnp.tile(b_g_sub[0] * (1.0 - lambda_init), w_v.shape[1] // V_DIM)
    h, k, v_t, q_t, gz_t = _mid(
        gated, x, mod[0], mod[1], perm, a_w_out[0].astype(BF16), g_post[0], g_kv, g_pre[1],
        w_k.astype(BF16), w_v.T.astype(BF16), w_b_t[:qk], w_b_t[qk:], gsub)

    lam = (jnp.exp(jnp.sum(b_lq1[0] * b_lk1[0])) - jnp.exp(jnp.sum(b_lq2[0] * b_lk2[0]))
           + lambda_init).reshape(1).astype(F32)
    return _attention(lam, q_t, k, v_t, gz_t, h, mod[1], b_w_out[0].astype(BF16), g_post[1])
```

```python
import functools
import math

import jax
import jax.numpy as jnp
from jax import lax
from jax.experimental import pallas as pl
from jax.experimental.pallas import tpu as pltpu

F32 = jnp.float32
BF16 = jnp.bfloat16

EPS = 1e-6
DEPTH = 2
SSM_GROUP = 16
SSM_STATE = 64
CHUNK = 16
GROUPS_PER_STEP = 16
SSM_UNROLL = 4
N_HEADS = 8
HEAD_DIM = 64
V_DIM = 2 * HEAD_DIM
V_ROWS = V_DIM + 16
TOKEN_BLOCK = 256
PRENORM_BLOCK = 1024
GLU_BLOCK = 512
MID_SUB = 2
ATT_AHEAD = 8
Q_SCALE = HEAD_DIM ** -0.5 * math.log2(math.e)
VMEM_LIMIT = 48 * 1024 * 1024
ATT_VMEM_LIMIT = 56 * 1024 * 1024


def _params(semantics):
    return pltpu.CompilerParams(dimension_semantics=semantics, vmem_limit_bytes=VMEM_LIMIT)


def _sigmoid(v):
    return 1.0 / (1.0 + jnp.exp(-v))


def _silu(v):
    return v * _sigmoid(v)


def _gelu_tanh(v):
    k = -2.0 * math.sqrt(2.0 / math.pi) * math.log2(math.e)
    return v / (1.0 + jnp.exp2(v * ((k * 0.044715) * (v * v) + k)))


def _rms_rows(v, g):
    return v * lax.rsqrt(jnp.mean(v * v, axis=-1, keepdims=True) + EPS) * g


def _dot(a, b):
    return jnp.dot(a, b, preferred_element_type=F32)


def _dot_nt(a, b):
    return lax.dot_general(a, b, (((1,), (1,)), ((), ())), preferred_element_type=F32)


def _dot_tn(a, b):
    return lax.dot_general(a, b, (((0,), (0,)), ((), ())), preferred_element_type=F32)


def _modulation_kernel(c_ref, w_ref, b_ref, o_ref):
    c = c_ref[...]
    o_ref[...] = jnp.dot(_silu(c), w_ref[...], preferred_element_type=F32,
                         precision=lax.Precision.HIGHEST) + b_ref[...]


def _modulation(c, ada_w, ada_b):
    bsz, d = c.shape
    depth, _, n = ada_w.shape
    tn = 512
    return pl.pallas_call(
        _modulation_kernel,
        grid=(depth, n // tn),
        in_specs=[
            pl.BlockSpec((bsz, d), lambda l, j: (0, 0)),
            pl.BlockSpec((None, d, tn), lambda l, j: (l, 0, j)),
            pl.BlockSpec((None, 1, tn), lambda l, j: (l, 0, j)),
        ],
        out_specs=pl.BlockSpec((None, bsz, tn), lambda l, j: (l, 0, j)),
        out_shape=jax.ShapeDtypeStruct((depth, bsz, n), F32),
        compiler_params=_params(("parallel", "parallel")),
        name="modulation",
    )(c, ada_w, ada_b.reshape(depth, 1, n))


def _chunk_permutation(n):
    r = jnp.arange(n)
    src = (r % (n // CHUNK)) * CHUNK + r // (n // CHUNK)
    return (src[:, None] == r[None, :]).astype(BF16)


def _prenorm_kernel(x_ref, mod_ref, g_ref, p_ref, o_ref):
    d = x_ref.shape[-1]
    sub = p_ref.shape[0]
    shift = mod_ref[:, 0:d]
    scale = mod_ref[:, d:2 * d]
    for r in range(x_ref.shape[0] // sub):
        x = x_ref[r * sub:(r + 1) * sub, :]
        h = _rms_rows(x, g_ref[...]) * (1.0 + scale) + shift
        hp = _dot(p_ref[...], h.astype(BF16)).astype(BF16)
        o_ref[:, r * (sub // CHUNK):(r + 1) * (sub // CHUNK), :] = hp.reshape(CHUNK, sub // CHUNK, d)


def _prenorm(x, mod0, g_pre0, perm):
    bsz, seq, d = x.shape
    tb = PRENORM_BLOCK
    return pl.pallas_call(
        _prenorm_kernel,
        grid=(bsz, seq // tb),
        in_specs=[
            pl.BlockSpec((None, tb, d), lambda b, j: (b, j, 0)),
            pl.BlockSpec((None, 1, 3 * d), lambda b, j: (b, 0, 0)),
            pl.BlockSpec((1, d), lambda b, j: (0, 0)),
            pl.BlockSpec(perm.shape, lambda b, j: (0, 0)),
        ],
        out_specs=pl.BlockSpec((None, CHUNK, tb // CHUNK, d), lambda b, j: (b, 0, j, 0)),
        out_shape=jax.ShapeDtypeStruct((bsz, CHUNK, seq // CHUNK, d), BF16),
        compiler_params=_params(("parallel", "parallel")),
        name="prenorm",
    )(x, mod0.reshape(bsz, 1, 3 * d), g_pre0.reshape(1, d), perm)


def _operators_kernel(lam_re_ref, lam_im_ref, log_dt_ref, bt_re_ref, bt_im_ref, c_re_ref, c_im_ref,
                      d_ref, toep_ref, sin_ref, sout_ref, decay_ref):
    rows = CHUNK * SSM_GROUP
    lanes = 2 * SSM_STATE
    lam_re, lam_im, dt = lam_re_ref[...], lam_im_ref[...], jnp.exp(log_dt_ref[...])
    ar, ai = lam_re * dt, lam_im * dt

    def apow(k):
        mag = jnp.exp(k * ar)
        return mag * jnp.cos(k * ai), mag * jnp.sin(k * ai)

    pos = lax.broadcasted_iota(jnp.int32, (CHUNK, 1), 0).astype(F32)
    a1r, a1i = apow(jnp.ones((1, 1), F32))
    den = lam_re * lam_re + lam_im * lam_im
    fr = ((a1r - 1.0) * lam_re + a1i * lam_im) / den
    fi = (a1i * lam_re - (a1r - 1.0) * lam_im) / den
    bbr = fr * bt_re_ref[...] - fi * bt_im_ref[...]
    bbi = fr * bt_im_ref[...] + fi * bt_re_ref[...]

    r_idx = lax.broadcasted_iota(jnp.int32, (rows, CHUNK), 0)
    k_idx = lax.broadcasted_iota(jnp.int32, (rows, CHUNK), 1)
    rep = (r_idx // SSM_GROUP == k_idx).astype(BF16)
    tile = (r_idx % SSM_GROUP == k_idx).astype(BF16)
    lane_tile = (lax.broadcasted_iota(jnp.int32, (SSM_GROUP, rows), 1) % SSM_GROUP
                 == lax.broadcasted_iota(jnp.int32, (SSM_GROUP, rows), 0)).astype(BF16)

    def split(v):
        hi = v.astype(BF16)
        return hi, (v - hi.astype(F32)).astype(BF16)

    def expand(sel, v):
        hi, lo = split(v)
        return _dot(sel, hi) + _dot(sel, lo)

    def dot3(x, y):
        (xh, xl), (yh, yl) = split(x), split(y)
        return _dot_nt(xh, yh) + _dot_nt(xh, yl) + _dot_nt(xl, yh)

    def times(xr, xi, yr, yi):
        return xr * yr - xi * yi, xr * yi + xi * yr

    left = lax.broadcasted_iota(jnp.int32, (rows, lanes), 1) < SSM_STATE
    halves = lambda v: (jnp.where(left, v, 0.0), jnp.where(left, 0.0, v))

    qr, qi = times(*apow(CHUNK - 1.0 - pos), fr, fi)
    sr, si = times(expand(rep, qr), expand(rep, qi),
                   expand(tile, bt_re_ref[...]), expand(tile, bt_im_ref[...]))
    cr, ci = expand(tile, c_re_ref[...]), expand(tile, c_im_ref[...])
    wr, wi = apow(pos + 1.0)
    our, oui = times(cr, ci, expand(rep, wr), expand(rep, wi))
    for h, (s_r, s_i, o_r, o_i) in enumerate(zip(halves(sr), halves(si), halves(our), halves(oui))):
        sin_ref[h, 0] = s_r.astype(BF16)
        sin_ref[h, 1] = s_i.astype(BF16)
        sout_ref[h, 0] = o_r.astype(BF16)
        sout_ref[h, 1] = (-o_i).astype(BF16)

    pr, pi = apow(pos)
    lr, li = times(cr, ci, expand(rep, pr), expand(rep, pi))
    lane_blk = lax.broadcasted_iota(jnp.int32, (rows, rows), 1) // SSM_GROUP
    diag = (lax.broadcasted_iota(jnp.int32, (rows, rows), 0)
            == lax.broadcasted_iota(jnp.int32, (rows, rows), 1))
    for h, (l_r, l_i) in enumerate(zip(halves(lr), halves(li))):
        kern = dot3(l_r, bbr) - dot3(l_i, bbi)
        k_hi, k_lo = split(kern)
        wide = _dot(k_hi, lane_tile) + _dot(k_lo, lane_tile)
        toep = jnp.where(diag, d_ref[h], 0.0)
        for p in range(CHUNK):
            n = p * SSM_GROUP
            delayed = wide if p == 0 else jnp.concatenate(
                [jnp.zeros((n, rows), F32), wide[:rows - n]], axis=0)
            toep = toep + jnp.where(lane_blk == p, delayed, 0.0)
        toep_ref[h] = toep.astype(BF16)

    dr, di = apow(jnp.full((1, 1), float(CHUNK), F32))
    sub = lax.broadcasted_iota(jnp.int32, (8, lanes), 0)
    decay_ref[...] = jnp.where(sub == 0, dr, jnp.where(sub == 1, di, 0.0))


def _ssm_operators(lam_re, lam_im, log_dt, b_re, b_im, c_re, c_im, d_skip):
    g, p = lam_re.shape
    cpg = SSM_GROUP
    rows = CHUNK * cpg
    pairs = g // 2
    lanes = 2 * p
    row_pair = lambda v: v.reshape(pairs, 1, lanes)
    mat_pair = lambda m: m.reshape(pairs, 2, cpg, p).transpose(0, 2, 1, 3).reshape(pairs, cpg, lanes)
    vec = pl.BlockSpec((None, 1, lanes), lambda q: (q, 0, 0))
    mat = pl.BlockSpec((None, cpg, lanes), lambda q: (q, 0, 0))
    return pl.pallas_call(
        _operators_kernel,
        grid=(pairs,),
        in_specs=[vec, vec, vec, mat, mat, mat, mat,
                  pl.BlockSpec((2, 1, rows), lambda q: (q, 0, 0))],
        out_specs=[
            pl.BlockSpec((2, rows, rows), lambda q: (q, 0, 0)),
            pl.BlockSpec((2, 2, rows, lanes), lambda q: (q, 0, 0, 0)),
            pl.BlockSpec((2, 2, rows, lanes), lambda q: (q, 0, 0, 0)),
            pl.BlockSpec((None, 8, lanes), lambda q: (q, 0, 0)),
        ],
        out_shape=[
            jax.ShapeDtypeStruct((g, rows, rows), BF16),
            jax.ShapeDtypeStruct((g, 2, rows, lanes), BF16),
            jax.ShapeDtypeStruct((g, 2, rows, lanes), BF16),
            jax.ShapeDtypeStruct((pairs, 8, lanes), F32),
        ],
        compiler_params=_params(("parallel",)),
        name="operators",
    )(row_pair(lam_re), row_pair(lam_im),
      row_pair(jnp.broadcast_to(log_dt[:, None], (g, p))),
      mat_pair(b_re.transpose(0, 2, 1)), mat_pair(b_im.transpose(0, 2, 1)),
      mat_pair(c_re), mat_pair(c_im),
      jnp.tile(d_skip.reshape(g, 1, cpg), (1, 1, CHUNK)))


def _ssm_kernel(h_ref, wu_ref, toep_ref, sin_ref, sout_ref, decay_ref, y_ref, xs_ref):
    n_chunks = y_ref.shape[-1] // CHUNK
    n_state = SSM_STATE
    j = pl.program_id(1)
    last = pl.num_programs(1) - 1
    fill, drain = j % 2, (j + 1) % 2

    def project(p):
        r0 = pl.multiple_of(p * n_chunks, n_chunks)
        u = _dot_nt(wu_ref[...], h_ref[pl.ds(r0, n_chunks), :]).astype(BF16)
        c0 = pl.multiple_of(p * SSM_GROUP, SSM_GROUP)
        for g in range(GROUPS_PER_STEP):
            xs_ref[fill, g, pl.ds(c0, SSM_GROUP), :] = u[g * SSM_GROUP:(g + 1) * SSM_GROUP, :]

    row = lax.broadcasted_iota(jnp.int32, (n_chunks, 2 * n_state), 0)

    def shift_rows(v, s):
        if s % 8 == 0:
            return jnp.concatenate([jnp.zeros((s, v.shape[1]), v.dtype), v[:-s]], axis=0)
        return jnp.where(row >= s, pltpu.roll(v, s, 0), 0.0)

    def gains(q):
        g0, g1 = 2 * q, 2 * q + 1
        x0, x1 = xs_ref[drain, g0], xs_ref[drain, g1]
        inc_r = _dot_tn(x0, sin_ref[g0, 0]) + _dot_tn(x1, sin_ref[g1, 0])
        inc_i = _dot_tn(x0, sin_ref[g0, 1]) + _dot_tn(x1, sin_ref[g1, 1])
        return inc_r, inc_i

    def finish(q, inc_r, inc_i):
        g0, g1 = 2 * q, 2 * q + 1
        er, ei = shift_rows(inc_r, 1), shift_rows(inc_i, 1)
        ar, ai = decay_ref[q, 0:1, :], decay_ref[q, 1:2, :]
        s = 1
        while s < n_chunks:
            if s % 8 == 0:
                dr = ar * er[:-s] - ai * ei[:-s]
                di = ar * ei[:-s] + ai * er[:-s]
                er = jnp.concatenate([er[:s], er[s:] + dr], axis=0)
                ei = jnp.concatenate([ei[:s], ei[s:] + di], axis=0)
            else:
                sr, si = shift_rows(er, s), shift_rows(ei, s)
                er, ei = er + (ar * sr - ai * si), ei + (ar * si + ai * sr)
            ar, ai = ar * ar - ai * ai, 2.0 * (ar * ai)
            s *= 2
        sr, si = er.astype(BF16), ei.astype(BF16)
        for g in (g0, g1):
            y = (_dot(toep_ref[g], xs_ref[drain, g]) + _dot_nt(sout_ref[g, 0], sr)
                 + _dot_nt(sout_ref[g, 1], si))
            act = _gelu_tanh(y).astype(BF16)
            row0 = pl.multiple_of(g * SSM_GROUP, SSM_GROUP)
            for p in range(CHUNK):
                y_ref[pl.ds(row0, SSM_GROUP), p * n_chunks:(p + 1) * n_chunks] = (
                    act[p * SSM_GROUP:(p + 1) * SSM_GROUP, :])

    n_iter = GROUPS_PER_STEP // 2 // SSM_UNROLL
    per_iter = CHUNK // n_iter

    def body(it, carry, with_scan, with_projection):
        qs = [it * SSM_UNROLL + u for u in range(SSM_UNROLL)]
        started = [gains(q) for q in qs] if with_scan else []
        if with_projection:
            for pp in range(per_iter):
                project(it * per_iter + pp)
        for q, inc in zip(qs, started):
            finish(q, *inc)
        return carry

    @pl.when(j == 0)
    def _():
        lax.fori_loop(0, n_iter, functools.partial(body, with_scan=False, with_projection=True), 0)

    @pl.when(jnp.logical_and(j > 0, j < last))
    def _():
        lax.fori_loop(0, n_iter, functools.partial(body, with_scan=True, with_projection=True), 0)

    @pl.when(j == last)
    def _():
        lax.fori_loop(0, n_iter, functools.partial(body, with_scan=True, with_projection=False), 0)


def _ssm(hperm, wu_t, toep, state_in, state_out, decay):
    bsz, seq, d = hperm.shape
    e = wu_t.shape[0]
    cb = GROUPS_PER_STEP * SSM_GROUP
    rows = CHUNK * SSM_GROUP
    n_chunks = seq // CHUNK
    gps = GROUPS_PER_STEP
    nblk = e // cb
    assert (gps // 2) % SSM_UNROLL == 0 and CHUNK % (gps // 2 // SSM_UNROLL) == 0
    proj = lambda j: jnp.minimum(j, nblk - 1)
    scan = lambda j: jnp.maximum(j - 1, 0)
    return pl.pallas_call(
        _ssm_kernel,
        grid=(bsz, nblk + 1),
        in_specs=[
            pl.BlockSpec((None, seq, d), lambda b, j: (b, 0, 0)),
            pl.BlockSpec((cb, d), lambda b, j: (proj(j), 0)),
            pl.BlockSpec((gps, rows, rows), lambda b, j: (scan(j), 0, 0)),
            pl.BlockSpec((gps, 2, rows, 2 * SSM_STATE), lambda b, j: (scan(j), 0, 0, 0)),
            pl.BlockSpec((gps, 2, rows, 2 * SSM_STATE), lambda b, j: (scan(j), 0, 0, 0)),
            pl.BlockSpec((gps // 2, 8, 2 * SSM_STATE), lambda b, j: (scan(j), 0, 0)),
        ],
        out_specs=pl.BlockSpec((None, cb, seq), lambda b, j: (b, scan(j), 0)),
        out_shape=jax.ShapeDtypeStruct((bsz, e, seq), BF16),
        scratch_shapes=[pltpu.VMEM((2, gps, rows, n_chunks), BF16)],
        compiler_params=_params(("parallel", "arbitrary")),
        name="ssm",
    )(hperm, wu_t, toep, state_in, state_out, decay)


def _glu_kernel(y_ref, h_ref, wg_ref, bg_ref, wz_ref, o_ref):
    e = y_ref.shape[0]
    rb = 256
    ya = y_ref[...]
    hb = h_ref[...]
    for r in range(e // rb):
        rows = slice(r * rb, (r + 1) * rb)
        gl = _dot(wg_ref[rows, :], ya) + bg_ref[rows, :]
        z = _dot_nt(wz_ref[rows, :], hb)
        yr = y_ref[rows, :].astype(F32)
        gated = yr * _sigmoid(gl) * _silu(z)
        o_ref[:, rows] = gated.T.astype(BF16)


def _glu(y_t, hperm, wglu_t, b_glu, wz_t):
    bsz, e, seq = y_t.shape
    d = hperm.shape[-1]
    tn = GLU_BLOCK
    return pl.pallas_call(
        _glu_kernel,
        grid=(bsz, seq // tn),
        in_specs=[
            pl.BlockSpec((None, e, tn), lambda b, j: (b, 0, j)),
            pl.BlockSpec((None, tn, d), lambda b, j: (b, j, 0)),
            pl.BlockSpec((e, e), lambda b, j: (0, 0)),
            pl.BlockSpec((e, 1), lambda b, j: (0, 0)),
            pl.BlockSpec((e, d), lambda b, j: (0, 0)),
        ],
        out_specs=pl.BlockSpec((None, tn, e), lambda b, j: (b, j, 0)),
        out_shape=jax.ShapeDtypeStruct((bsz, seq, e), BF16),
        compiler_params=_params(("parallel", "parallel")),
        name="glu",
    )(y_t, hperm, wglu_t, b_glu.reshape(e, 1), wz_t)


def _mid_kernel(gp_ref, x_ref, mod0_ref, mod1_ref, p_ref, wo_ref, gpost_ref, gkv_ref, gpre_ref,
                wk_ref, wv_ref, wq_ref, wz_ref, gsub_ref, h_ref, k_ref, vt_ref, qt_ref, gz_ref):
    d = x_ref.shape[-1]
    tb = p_ref.shape[0]
    subs = range(x_ref.shape[0] // tb)
    cps = tb // CHUNK
    gate0 = mod0_ref[:, 2 * d:3 * d]
    shift1 = mod1_ref[:, 0:d]
    scale1 = mod1_ref[:, d:2 * d]
    ones_row = (lax.broadcasted_iota(jnp.int32, (V_ROWS - V_DIM, tb), 0) == 0).astype(BF16)

    ys = []
    for s in subs:
        gp = gp_ref[:, s * cps:(s + 1) * cps, :].reshape(tb, -1)
        ys.append(_dot(_dot(p_ref[...], gp).astype(BF16), wo_ref[...]))
    ins = []
    for s, y in zip(subs, ys):
        rows = slice(s * tb, (s + 1) * tb)
        h = x_ref[rows, :] + gate0 * _rms_rows(y, gpost_ref[...])
        h_ref[rows, :] = h
        kv_in = _rms_rows(h, gkv_ref[...]).astype(BF16)
        h_in = (_rms_rows(h, gpre_ref[...]) * (1.0 + scale1) + shift1).astype(BF16)
        ins.append((kv_in, h_in))
    for s, (kv_in, h_in) in zip(subs, ins):
        k_ref[s * tb:(s + 1) * tb, :] = _dot(kv_in, wk_ref[...]).astype(BF16)
        vt = _dot_nt(wv_ref[...], kv_in).astype(BF16)
        for hd in range(vt.shape[0] // V_DIM):
            vt_ref[s, hd * V_ROWS:hd * V_ROWS + V_DIM, :] = vt[hd * V_DIM:(hd + 1) * V_DIM, :]
            vt_ref[s, hd * V_ROWS + V_DIM:(hd + 1) * V_ROWS, :] = ones_row
        qt_ref[s] = (_dot_nt(wq_ref[...], h_in) * Q_SCALE).astype(BF16)
        gz_ref[s] = (_silu(_dot_nt(wz_ref[...], h_in)) * gsub_ref[...]).astype(BF16)


def _mid(gated_perm, x, mod0, mod1, perm, wo, g_post0, g_kv, g_pre1, wk, wv_t, wq_t, wz_t, gsub):
    bsz, seq, d = x.shape
    tb = TOKEN_BLOCK
    nb = seq // tb
    sub = MID_SUB
    e = gated_perm.shape[-1]
    qk = wk.shape[1]
    av = wv_t.shape[0]
    row = lambda b, j: (b, j, 0)
    const2 = lambda b, j: (0, 0)
    t_spec = lambda n: pl.BlockSpec((None, sub, n, tb), lambda b, j: (b, j, 0, 0))
    return pl.pallas_call(
        _mid_kernel,
        grid=(bsz, nb // sub),
        in_specs=[
            pl.BlockSpec((None, CHUNK, sub * tb // CHUNK, e), lambda b, j: (b, 0, j, 0)),
            pl.BlockSpec((None, sub * tb, d), row),
            pl.BlockSpec((None, 1, 3 * d), lambda b, j: (b, 0, 0)),
            pl.BlockSpec((None, 1, 3 * d), lambda b, j: (b, 0, 0)),
            pl.BlockSpec((tb, tb), const2),
            pl.BlockSpec((e, d), const2),
            pl.BlockSpec((1, d), const2),
            pl.BlockSpec((1, d), const2),
            pl.BlockSpec((1, d), const2),
            pl.BlockSpec((d, qk), const2),
            pl.BlockSpec((av, d), const2),
            pl.BlockSpec((qk, d), const2),
            pl.BlockSpec((av, d), const2),
            pl.BlockSpec((av, 1), const2),
        ],
        out_specs=[
            pl.BlockSpec((None, sub * tb, d), row),
            pl.BlockSpec((None, sub * tb, qk), row),
            t_spec(av // V_DIM * V_ROWS),
            t_spec(qk),
            t_spec(av),
        ],
        out_shape=[
            jax.ShapeDtypeStruct((bsz, seq, d), F32),
            jax.ShapeDtypeStruct((bsz, seq, qk), BF16),
            jax.ShapeDtypeStruct((bsz, nb, av // V_DIM * V_ROWS, tb), BF16),
            jax.ShapeDtypeStruct((bsz, nb, qk, tb), BF16),
            jax.ShapeDtypeStruct((bsz, nb, av, tb), BF16),
        ],
        compiler_params=_params(("parallel", "parallel")),
        name="mid",
    )(gated_perm.reshape(bsz, CHUNK, seq // CHUNK, e), x, mod0.reshape(bsz, 1, 3 * d),
      mod1.reshape(bsz, 1, 3 * d), perm, wo, g_post0.reshape(1, d), g_kv.reshape(1, d),
      g_pre1.reshape(1, d), wk, wv_t, wq_t, wz_t, gsub.reshape(av, 1))


def _attention_kernel(lam_ref, qt_ref, k_ref, vt_ref, gz_ref, bias_ref, h_ref, mod1_ref, wo_ref,
                      gpost_ref, o_ref, qp_ref, m_ref, acc_ref, s_ref, og_ref):
    tq = qt_ref.shape[-1]
    hw = 2 * HEAD_DIM
    heads = qt_ref.shape[0] // hw
    i = pl.program_id(1)

    zero = jnp.zeros((HEAD_DIM, tq), qt_ref.dtype)
    for h in range(heads):
        qt = qt_ref[h * hw:(h + 1) * hw, :]
        qp_ref[2 * h] = jnp.concatenate([qt[:HEAD_DIM], zero], axis=0)
        qp_ref[2 * h + 1] = jnp.concatenate([zero, qt[HEAD_DIM:]], axis=0)
    n_strips = 2 * heads

    def scores(j, n):
        h = n // 2
        row0 = pl.multiple_of(j * tq, tq)
        return _dot(k_ref[pl.ds(row0, tq), h * hw:(h + 1) * hw], qp_ref[n])

    def step(j, last=False):
        for n in range(n_strips):
            s = s_ref[n % ATT_AHEAD]
            if n + ATT_AHEAD < n_strips:
                s_ref[n % ATT_AHEAD] = scores(j, n + ATT_AHEAD)
            elif not last:
                s_ref[n % ATT_AHEAD] = scores(j + 1, n + ATT_AHEAD - n_strips)
            if last:
                s = s + bias_ref[...]
            m_old = m_ref[n]
            m_new = jnp.maximum(m_old, jnp.max(s, axis=0, keepdims=True))
            alpha = jnp.exp2(m_old - m_new)
            p = jnp.exp2((s - m_new).astype(BF16))
            m_ref[n] = m_new
            h = n // 2
            vt = vt_ref[j, h * V_ROWS:(h + 1) * V_ROWS, :]
            acc_ref[n] = alpha * acc_ref[n] + _dot(vt, p)

    for n in range(ATT_AHEAD):
        s_ref[n] = scores(0, n)
    m_ref[...] = jnp.full(m_ref.shape, -jnp.inf, F32)
    acc_ref[...] = jnp.zeros(acc_ref.shape, F32)

    def two_steps(t, carry):
        step(2 * t)
        step(2 * t + 1)
        return carry

    lax.fori_loop(0, i // 2, two_steps, 0)

    @pl.when(i % 2 == 1)
    def _():
        step(i - 1)

    step(i, last=True)

    for h in range(heads):
        r0 = 1.0 / acc_ref[2 * h, V_DIM:V_DIM + 1, :]
        r1 = lam_ref[0] / acc_ref[2 * h + 1, V_DIM:V_DIM + 1, :]
        o = acc_ref[2 * h, :V_DIM, :] * r0 - acc_ref[2 * h + 1, :V_DIM, :] * r1
        rows = slice(h * V_DIM, (h + 1) * V_DIM)
        inv_rms = lax.rsqrt(jnp.mean(o * o, axis=0, keepdims=True) + EPS)
        og_ref[rows, :] = (o * inv_rms * gz_ref[rows, :].astype(F32)).astype(BF16)
        if h % 2 == 1:
            pr = slice((h - 1) * V_DIM, (h + 1) * V_DIM)
            part = _dot_tn(og_ref[pr, :], wo_ref[pr, :])
            y = part if h == 1 else y + part
    d = h_ref.shape[-1]
    gate1 = mod1_ref[:, 2 * d:3 * d]
    o_ref[...] = h_ref[...] + gate1 * _rms_rows(y, gpost_ref[...])


def _attention(lam, q_t, k, v_t, gz_t, h, mod1, wo, g_post1):
    bsz, nb, width, tb = q_t.shape
    seq = nb * tb
    d = h.shape[-1]
    heads = width // (2 * HEAD_DIM)
    assert heads % 2 == 0 and (2 * heads) % ATT_AHEAD == 0
    pos = jnp.arange(tb)
    bias = jnp.where(pos[:, None] <= pos[None, :], 0.0, -jnp.inf).astype(F32)
    blk = pl.BlockSpec((None, None, width, tb), lambda b, i: (b, i, 0, 0))
    once = pl.Buffered(1)
    return pl.pallas_call(
        _attention_kernel,
        grid=(bsz, nb),
        in_specs=[
            pl.BlockSpec(memory_space=pltpu.SMEM),
            blk,
            pl.BlockSpec((None, seq, width), lambda b, i: (b, 0, 0)),
            pl.BlockSpec((None, nb, heads * V_ROWS, tb), lambda b, i: (b, 0, 0, 0)),
            blk,
            pl.BlockSpec((tb, tb), lambda b, i: (0, 0), pipeline_mode=once),
            pl.BlockSpec((None, tb, d), lambda b, i: (b, i, 0)),
            pl.BlockSpec((None, 1, 3 * d), lambda b, i: (b, 0, 0)),
            pl.BlockSpec((width, d), lambda b, i: (0, 0), pipeline_mode=once),
            pl.BlockSpec((1, d), lambda b, i: (0, 0), pipeline_mode=once),
        ],
        out_specs=pl.BlockSpec((None, tb, d), lambda b, i: (b, i, 0)),
        out_shape=jax.ShapeDtypeStruct((bsz, seq, d), F32),
        scratch_shapes=[
            pltpu.VMEM((2 * heads, 2 * HEAD_DIM, tb), BF16),
            pltpu.VMEM((2 * heads, 1, tb), F32),
            pltpu.VMEM((2 * heads, V_ROWS, tb), F32),
            pltpu.VMEM((ATT_AHEAD, tb, tb), F32),
            pltpu.VMEM((width, tb), BF16),
        ],
        compiler_params=pltpu.CompilerParams(dimension_semantics=("parallel", "arbitrary"),
                                             vmem_limit_bytes=ATT_VMEM_LIMIT),
        name="attention",
    )(lam, q_t, k, v_t, gz_t, bias, h, mod1.reshape(bsz, 1, 3 * d), wo, g_post1.reshape(1, d))


def kernel(x, c, ada_w, ada_b, g_pre, g_post, a_w_in, a_lam_re, a_lam_im, a_log_dt, a_b_re, a_b_im,
           a_c_re, a_c_im, a_d, a_w_glu, a_b_glu, a_w_out, g_kv, w_k, w_v, b_w_in, b_lq1, b_lk1,
           b_lq2, b_lk2, b_g_sub, b_w_out):
    bsz, seq, d = x.shape
    e = a_w_glu.shape[1]
    qk = w_k.shape[1]
    assert seq % (CHUNK * 128) == 0 and seq % GLU_BLOCK == 0 and d % 128 == 0
    assert e % (GROUPS_PER_STEP * SSM_GROUP) == 0

    mod = _modulation(c, ada_w, ada_b)
    perm = _chunk_permutation(TOKEN_BLOCK)

    hperm = _prenorm(x, mod[0], g_pre[0], perm).reshape(bsz, seq, d)
    toep, state_in, state_out, decay = _ssm_operators(
        a_lam_re[0], a_lam_im[0], a_log_dt[0], a_b_re[0], a_b_im[0], a_c_re[0], a_c_im[0], a_d[0])
    w_in_t = a_w_in[0].T.astype(BF16)
    y_t = _ssm(hperm, w_in_t[:e], toep, state_in, state_out, decay)
    gated = _glu(y_t, hperm, a_w_glu[0].T.astype(BF16), a_b_glu[0], w_in_t[e:])

    layer = DEPTH // 2
    lambda_init = 0.8 - 0.6 * math.exp(-0.3 * layer)
    w_b_t = b_w_in[0].T.astype(BF16)
    gsub = jnp.tile(b_g_sub[0] * (1.0 - lambda_init), w_v.shape[1] // V_DIM)
    h, k, v_t, q_t, gz_t = _mid(
        gated, x, mod[0], mod[1], perm, a_w_out[0].astype(BF16), g_post[0], g_kv, g_pre[1],
        w_k.astype(BF16), w_v.T.astype(BF16), w_b_t[:qk], w_b_t[qk:], gsub)

    lam = (jnp.exp(jnp.sum(b_lq1[0] * b_lk1[0])) - jnp.exp(jnp.sum(b_lq2[0] * b_lk2[0]))
           + lambda_init).reshape(1).astype(F32)
    return _attention(lam, q_t, k, v_t, gz_t, h, mod[1], b_w_out[0].astype(BF16), g_post[1])
```

```python
import functools
import math

import jax
import jax.numpy as jnp
from jax import lax
from jax.experimental import pallas as pl
from jax.experimental.pallas import tpu as pltpu

F32 = jnp.float32
BF16 = jnp.bfloat16

EPS = 1e-6
DEPTH = 2
SSM_GROUP = 16
SSM_STATE = 64
CHUNK = 16
GROUPS_PER_STEP = 16
SSM_UNROLL = 4
N_HEADS = 8
HEAD_DIM = 64
V_DIM = 2 * HEAD_DIM
V_ROWS = V_DIM + 16
TOKEN_BLOCK = 256
PRENORM_BLOCK = 1024
GLU_BLOCK = 512
MID_SUB = 2
ATT_AHEAD = 8
Q_SCALE = HEAD_DIM ** -0.5 * math.log2(math.e)
VMEM_LIMIT = 48 * 1024 * 1024
ATT_VMEM_LIMIT = 56 * 1024 * 1024


def _params(semantics):
    return pltpu.CompilerParams(dimension_semantics=semantics, vmem_limit_bytes=VMEM_LIMIT)


def _sigmoid(v):
    return 1.0 / (1.0 + jnp.exp(-v))


def _silu(v):
    return v * _sigmoid(v)


def _gelu_tanh(v):
    k = -2.0 * math.sqrt(2.0 / math.pi) * math.log2(math.e)
    return v / (1.0 + jnp.exp2(v * ((k * 0.044715) * (v * v) + k)))


def _rms_rows(v, g):
    return v * lax.rsqrt(jnp.mean(v * v, axis=-1, keepdims=True) + EPS) * g


def _dot(a, b):
    return jnp.dot(a, b, preferred_element_type=F32)


def _dot_nt(a, b):
    return lax.dot_general(a, b, (((1,), (1,)), ((), ())), preferred_element_type=F32)


def _dot_tn(a, b):
    return lax.dot_general(a, b, (((0,), (0,)), ((), ())), preferred_element_type=F32)


def _modulation_kernel(c_ref, w_ref, b_ref, o_ref):
    def split(v):
        hi = v.astype(BF16)
        return hi, (v - hi.astype(F32)).astype(BF16)

    (sh, sl), (wh, wl) = split(_silu(c_ref[...])), split(w_ref[...])
    o_ref[...] = _dot(sh, wh) + _dot(sh, wl) + _dot(sl, wh) + b_ref[...]


def _modulation(c, ada_w, ada_b):
    bsz, d = c.shape
    depth, _, n = ada_w.shape
    tn = 1024
    return pl.pallas_call(
        _modulation_kernel,
        grid=(depth, n // tn),
        in_specs=[
            pl.BlockSpec((bsz, d), lambda l, j: (0, 0)),
            pl.BlockSpec((None, d, tn), lambda l, j: (l, 0, j)),
            pl.BlockSpec((None, 1, tn), lambda l, j: (l, 0, j)),
        ],
        out_specs=pl.BlockSpec((None, bsz, tn), lambda l, j: (l, 0, j)),
        out_shape=jax.ShapeDtypeStruct((depth, bsz, n), F32),
        compiler_params=_params(("parallel", "parallel")),
        name="modulation",
    )(c, ada_w, ada_b.reshape(depth, 1, n))


def _chunk_permutation(n):
    r = jnp.arange(n)
    src = (r % (n // CHUNK)) * CHUNK + r // (n // CHUNK)
    return (src[:, None] == r[None, :]).astype(BF16)


def _prenorm_kernel(x_ref, mod_ref, g_ref, p_ref, o_ref):
    d = x_ref.shape[-1]
    sub = p_ref.shape[0]
    shift = mod_ref[:, 0:d]
    scale = mod_ref[:, d:2 * d]
    for r in range(x_ref.shape[0] // sub):
        x = x_ref[r * sub:(r + 1) * sub, :]
        h = _rms_rows(x, g_ref[...]) * (1.0 + scale) + shift
        hp = _dot(p_ref[...], h.astype(BF16)).astype(BF16)
        o_ref[:, r * (sub // CHUNK):(r + 1) * (sub // CHUNK), :] = hp.reshape(CHUNK, sub // CHUNK, d)


def _prenorm(x, mod0, g_pre0, perm):
    bsz, seq, d = x.shape
    tb = PRENORM_BLOCK
    return pl.pallas_call(
        _prenorm_kernel,
        grid=(bsz, seq // tb),
        in_specs=[
            pl.BlockSpec((None, tb, d), lambda b, j: (b, j, 0)),
            pl.BlockSpec((None, 1, 3 * d), lambda b, j: (b, 0, 0)),
            pl.BlockSpec((1, d), lambda b, j: (0, 0)),
            pl.BlockSpec(perm.shape, lambda b, j: (0, 0)),
        ],
        out_specs=pl.BlockSpec((None, CHUNK, tb // CHUNK, d), lambda b, j: (b, 0, j, 0)),
        out_shape=jax.ShapeDtypeStruct((bsz, CHUNK, seq // CHUNK, d), BF16),
        compiler_params=_params(("parallel", "parallel")),
        name="prenorm",
    )(x, mod0.reshape(bsz, 1, 3 * d), g_pre0.reshape(1, d), perm)


def _operators_kernel(lam_re_ref, lam_im_ref, log_dt_ref, bt_re_ref, bt_im_ref, c_re_ref, c_im_ref,
                      d_ref, toep_ref, sin_ref, sout_ref, decay_ref):
    rows = CHUNK * SSM_GROUP
    lanes = 2 * SSM_STATE
    lam_re, lam_im, dt = lam_re_ref[...], lam_im_ref[...], jnp.exp(log_dt_ref[...])
    ar, ai = lam_re * dt, lam_im * dt

    def apow(k):
        mag = jnp.exp(k * ar)
        return mag * jnp.cos(k * ai), mag * jnp.sin(k * ai)

    pos = lax.broadcasted_iota(jnp.int32, (CHUNK, 1), 0).astype(F32)
    a1r, a1i = apow(jnp.ones((1, 1), F32))
    den = lam_re * lam_re + lam_im * lam_im
    fr = ((a1r - 1.0) * lam_re + a1i * lam_im) / den
    fi = (a1i * lam_re - (a1r - 1.0) * lam_im) / den
    bbr = fr * bt_re_ref[...] - fi * bt_im_ref[...]
    bbi = fr * bt_im_ref[...] + fi * bt_re_ref[...]

    r_idx = lax.broadcasted_iota(jnp.int32, (rows, CHUNK), 0)
    k_idx = lax.broadcasted_iota(jnp.int32, (rows, CHUNK), 1)
    rep = (r_idx // SSM_GROUP == k_idx).astype(BF16)
    tile = (r_idx % SSM_GROUP == k_idx).astype(BF16)
    lane_tile = (lax.broadcasted_iota(jnp.int32, (SSM_GROUP, rows), 1) % SSM_GROUP
                 == lax.broadcasted_iota(jnp.int32, (SSM_GROUP, rows), 0)).astype(BF16)

    def split(v):
        hi = v.astype(BF16)
        return hi, (v - hi.astype(F32)).astype(BF16)

    def expand(sel, v):
        hi, lo = split(v)
        return _dot(sel, hi) + _dot(sel, lo)

    def dot3(x, y):
        (xh, xl), (yh, yl) = split(x), split(y)
        return _dot_nt(xh, yh) + _dot_nt(xh, yl) + _dot_nt(xl, yh)

    def times(xr, xi, yr, yi):
        return xr * yr - xi * yi, xr * yi + xi * yr

    left = lax.broadcasted_iota(jnp.int32, (rows, lanes), 1) < SSM_STATE
    halves = lambda v: (jnp.where(left, v, 0.0), jnp.where(left, 0.0, v))

    qr, qi = times(*apow(CHUNK - 1.0 - pos), fr, fi)
    sr, si = times(expand(rep, qr), expand(rep, qi),
                   expand(tile, bt_re_ref[...]), expand(tile, bt_im_ref[...]))
    cr, ci = expand(tile, c_re_ref[...]), expand(tile, c_im_ref[...])
    wr, wi = apow(pos + 1.0)
    our, oui = times(cr, ci, expand(rep, wr), expand(rep, wi))
    for h, (s_r, s_i, o_r, o_i) in enumerate(zip(halves(sr), halves(si), halves(our), halves(oui))):
        sin_ref[h, 0] = s_r.astype(BF16)
        sin_ref[h, 1] = s_i.astype(BF16)
        sout_ref[h, 0] = o_r.astype(BF16)
        sout_ref[h, 1] = (-o_i).astype(BF16)

    pr, pi = apow(pos)
    lr, li = times(cr, ci, expand(rep, pr), expand(rep, pi))
    lane_blk = lax.broadcasted_iota(jnp.int32, (rows, rows), 1) // SSM_GROUP
    diag = (lax.broadcasted_iota(jnp.int32, (rows, rows), 0)
            == lax.broadcasted_iota(jnp.int32, (rows, rows), 1))
    for h, (l_r, l_i) in enumerate(zip(halves(lr), halves(li))):
        kern = dot3(l_r, bbr) - dot3(l_i, bbi)
        k_hi, k_lo = split(kern)
        wide = _dot(k_hi, lane_tile) + _dot(k_lo, lane_tile)
        toep = jnp.where(diag, d_ref[h], 0.0)
        for p in range(CHUNK):
            n = p * SSM_GROUP
            delayed = wide if p == 0 else jnp.concatenate(
                [jnp.zeros((n, rows), F32), wide[:rows - n]], axis=0)
            toep = toep + jnp.where(lane_blk == p, delayed, 0.0)
        toep_ref[h] = toep.astype(BF16)

    dr, di = apow(jnp.full((1, 1), float(CHUNK), F32))
    sub = lax.broadcasted_iota(jnp.int32, (8, lanes), 0)
    decay_ref[...] = jnp.where(sub == 0, dr, jnp.where(sub == 1, di, 0.0))


def _ssm_operators(lam_re, lam_im, log_dt, b_re, b_im, c_re, c_im, d_skip):
    g, p = lam_re.shape
    cpg = SSM_GROUP
    rows = CHUNK * cpg
    pairs = g // 2
    lanes = 2 * p
    row_pair = lambda v: v.reshape(pairs, 1, lanes)
    mat_pair = lambda m: m.reshape(pairs, 2, cpg, p).transpose(0, 2, 1, 3).reshape(pairs, cpg, lanes)
    vec = pl.BlockSpec((None, 1, lanes), lambda q: (q, 0, 0))
    mat = pl.BlockSpec((None, cpg, lanes), lambda q: (q, 0, 0))
    return pl.pallas_call(
        _operators_kernel,
        grid=(pairs,),
        in_specs=[vec, vec, vec, mat, mat, mat, mat,
                  pl.BlockSpec((2, 1, rows), lambda q: (q, 0, 0))],
        out_specs=[
            pl.BlockSpec((2, rows, rows), lambda q: (q, 0, 0)),
            pl.BlockSpec((2, 2, rows, lanes), lambda q: (q, 0, 0, 0)),
            pl.BlockSpec((2, 2, rows, lanes), lambda q: (q, 0, 0, 0)),
            pl.BlockSpec((None, 8, lanes), lambda q: (q, 0, 0)),
        ],
        out_shape=[
            jax.ShapeDtypeStruct((g, rows, rows), BF16),
            jax.ShapeDtypeStruct((g, 2, rows, lanes), BF16),
            jax.ShapeDtypeStruct((g, 2, rows, lanes), BF16),
            jax.ShapeDtypeStruct((pairs, 8, lanes), F32),
        ],
        compiler_params=_params(("parallel",)),
        name="operators",
    )(row_pair(lam_re), row_pair(lam_im),
      row_pair(jnp.broadcast_to(log_dt[:, None], (g, p))),
      mat_pair(b_re.transpose(0, 2, 1)), mat_pair(b_im.transpose(0, 2, 1)),
      mat_pair(c_re), mat_pair(c_im),
      jnp.tile(d_skip.reshape(g, 1, cpg), (1, 1, CHUNK)))


def _ssm_kernel(h_ref, wu_ref, toep_ref, sin_ref, sout_ref, decay_ref, y_ref, xs_ref):
    n_chunks = y_ref.shape[-1] // CHUNK
    n_state = SSM_STATE
    j = pl.program_id(1)
    last = pl.num_programs(1) - 1
    fill, drain = j % 2, (j + 1) % 2

    def project(p):
        r0 = pl.multiple_of(p * n_chunks, n_chunks)
        u = _dot_nt(wu_ref[...], h_ref[pl.ds(r0, n_chunks), :]).astype(BF16)
        c0 = pl.multiple_of(p * SSM_GROUP, SSM_GROUP)
        for g in range(GROUPS_PER_STEP):
            xs_ref[fill, g, pl.ds(c0, SSM_GROUP), :] = u[g * SSM_GROUP:(g + 1) * SSM_GROUP, :]

    row = lax.broadcasted_iota(jnp.int32, (n_chunks, 2 * n_state), 0)

    def shift_rows(v, s):
        if s % 8 == 0:
            return jnp.concatenate([jnp.zeros((s, v.shape[1]), v.dtype), v[:-s]], axis=0)
        return jnp.where(row >= s, pltpu.roll(v, s, 0), 0.0)

    def gains(q):
        g0, g1 = 2 * q, 2 * q + 1
        x0, x1 = xs_ref[drain, g0], xs_ref[drain, g1]
        inc_r = _dot_tn(x0, sin_ref[g0, 0]) + _dot_tn(x1, sin_ref[g1, 0])
        inc_i = _dot_tn(x0, sin_ref[g0, 1]) + _dot_tn(x1, sin_ref[g1, 1])
        return inc_r, inc_i

    def finish(q, inc_r, inc_i):
        g0, g1 = 2 * q, 2 * q + 1
        er, ei = shift_rows(inc_r, 1), shift_rows(inc_i, 1)
        ar, ai = decay_ref[q, 0:1, :], decay_ref[q, 1:2, :]
        s = 1
        while s < n_chunks:
            if s % 8 == 0:
                dr = ar * er[:-s] - ai * ei[:-s]
                di = ar * ei[:-s] + ai * er[:-s]
                er = jnp.concatenate([er[:s], er[s:] + dr], axis=0)
                ei = jnp.concatenate([ei[:s], ei[s:] + di], axis=0)
            else:
                sr, si = shift_rows(er, s), shift_rows(ei, s)
                er, ei = er + (ar * sr - ai * si), ei + (ar * si + ai * sr)
            ar, ai = ar * ar - ai * ai, 2.0 * (ar * ai)
            s *= 2
        sr, si = er.astype(BF16), ei.astype(BF16)
        for g in (g0, g1):
            y = (_dot(toep_ref[g], xs_ref[drain, g]) + _dot_nt(sout_ref[g, 0], sr)
                 + _dot_nt(sout_ref[g, 1], si))
            act = _gelu_tanh(y).astype(BF16)
            row0 = pl.multiple_of(g * SSM_GROUP, SSM_GROUP)
            for p in range(CHUNK):
                y_ref[pl.ds(row0, SSM_GROUP), p * n_chunks:(p + 1) * n_chunks] = (
                    act[p * SSM_GROUP:(p + 1) * SSM_GROUP, :])

    n_iter = GROUPS_PER_STEP // 2 // SSM_UNROLL
    per_iter = CHUNK // n_iter

    def body(it, carry, with_scan, with_projection):
        qs = [it * SSM_UNROLL + u for u in range(SSM_UNROLL)]
        started = [gains(q) for q in qs] if with_scan else []
        if with_projection:
            for pp in range(per_iter):
                project(it * per_iter + pp)
        for q, inc in zip(qs, started):
            finish(q, *inc)
        return carry

    @pl.when(j == 0)
    def _():
        lax.fori_loop(0, n_iter, functools.partial(body, with_scan=False, with_projection=True), 0)

    @pl.when(jnp.logical_and(j > 0, j < last))
    def _():
        lax.fori_loop(0, n_iter, functools.partial(body, with_scan=True, with_projection=True), 0)

    @pl.when(j == last)
    def _():
        lax.fori_loop(0, n_iter, functools.partial(body, with_scan=True, with_projection=False), 0)


def _ssm(hperm, wu_t, toep, state_in, state_out, decay):
    bsz, seq, d = hperm.shape
    e = wu_t.shape[0]
    cb = GROUPS_PER_STEP * SSM_GROUP
    rows = CHUNK * SSM_GROUP
    n_chunks = seq // CHUNK
    gps = GROUPS_PER_STEP
    nblk = e // cb
    assert (gps // 2) % SSM_UNROLL == 0 and CHUNK % (gps // 2 // SSM_UNROLL) == 0
    proj = lambda j: jnp.minimum(j, nblk - 1)
    scan = lambda j: jnp.maximum(j - 1, 0)
    return pl.pallas_call(
        _ssm_kernel,
        grid=(bsz, nblk + 1),
        in_specs=[
            pl.BlockSpec((None, seq, d), lambda b, j: (b, 0, 0)),
            pl.BlockSpec((cb, d), lambda b, j: (proj(j), 0)),
            pl.BlockSpec((gps, rows, rows), lambda b, j: (scan(j), 0, 0)),
            pl.BlockSpec((gps, 2, rows, 2 * SSM_STATE), lambda b, j: (scan(j), 0, 0, 0)),
            pl.BlockSpec((gps, 2, rows, 2 * SSM_STATE), lambda b, j: (scan(j), 0, 0, 0)),
            pl.BlockSpec((gps // 2, 8, 2 * SSM_STATE), lambda b, j: (scan(j), 0, 0)),
        ],
        out_specs=pl.BlockSpec((None, cb, seq), lambda b, j: (b, scan(j), 0)),
        out_shape=jax.ShapeDtypeStruct((bsz, e, seq), BF16),
        scratch_shapes=[pltpu.VMEM((2, gps, rows, n_chunks), BF16)],
        compiler_params=_params(("parallel", "arbitrary")),
        name="ssm",
    )(hperm, wu_t, toep, state_in, state_out, decay)


def _glu_kernel(y_ref, h_ref, wg_ref, bg_ref, wz_ref, o_ref):
    e = y_ref.shape[0]
    rb = 256
    ya = y_ref[...]
    hb = h_ref[...]
    for r in range(e // rb):
        rows = slice(r * rb, (r + 1) * rb)
        gl = _dot(wg_ref[rows, :], ya) + bg_ref[rows, :]
        z = _dot_nt(wz_ref[rows, :], hb)
        yr = y_ref[rows, :].astype(F32)
        gated = yr * _sigmoid(gl) * _silu(z)
        o_ref[:, rows] = gated.T.astype(BF16)


def _glu(y_t, hperm, wglu_t, b_glu, wz_t):
    bsz, e, seq = y_t.shape
    d = hperm.shape[-1]
    tn = GLU_BLOCK
    return pl.pallas_call(
        _glu_kernel,
        grid=(bsz, seq // tn),
        in_specs=[
            pl.BlockSpec((None, e, tn), lambda b, j: (b, 0, j)),
            pl.BlockSpec((None, tn, d), lambda b, j: (b, j, 0)),
            pl.BlockSpec((e, e), lambda b, j: (0, 0)),
            pl.BlockSpec((e, 1), lambda b, j: (0, 0)),
            pl.BlockSpec((e, d), lambda b, j: (0, 0)),
        ],
        out_specs=pl.BlockSpec((None, tn, e), lambda b, j: (b, j, 0)),
        out_shape=jax.ShapeDtypeStruct((bsz, seq, e), BF16),
        compiler_params=_params(("parallel", "parallel")),
        name="glu",
    )(y_t, hperm, wglu_t, b_glu.reshape(e, 1), wz_t)


def _mid_kernel(gp_ref, x_ref, mod0_ref, mod1_ref, p_ref, wo_ref, gpost_ref, gkv_ref, gpre_ref,
                wk_ref, wv_ref, wq_ref, wz_ref, gsub_ref, h_ref, k_ref, vt_ref, qt_ref, gz_ref):
    d = x_ref.shape[-1]
    tb = p_ref.shape[0]
    subs = range(x_ref.shape[0] // tb)
    cps = tb // CHUNK
    gate0 = mod0_ref[:, 2 * d:3 * d]
    shift1 = mod1_ref[:, 0:d]
    scale1 = mod1_ref[:, d:2 * d]
    ones_row = (lax.broadcasted_iota(jnp.int32, (V_ROWS - V_DIM, tb), 0) == 0).astype(BF16)

    ys = []
    for s in subs:
        gp = gp_ref[:, s * cps:(s + 1) * cps, :].reshape(tb, -1)
        ys.append(_dot(_dot(p_ref[...], gp).astype(BF16), wo_ref[...]))
    ins = []
    for s, y in zip(subs, ys):
        rows = slice(s * tb, (s + 1) * tb)
        h = x_ref[rows, :] + gate0 * _rms_rows(y, gpost_ref[...])
        h_ref[rows, :] = h
        kv_in = _rms_rows(h, gkv_ref[...]).astype(BF16)
        h_in = (_rms_rows(h, gpre_ref[...]) * (1.0 + scale1) + shift1).astype(BF16)
        ins.append((kv_in, h_in))
    for s, (kv_in, h_in) in zip(subs, ins):
        k_ref[s * tb:(s + 1) * tb, :] = _dot(kv_in, wk_ref[...]).astype(BF16)
        vt = _dot_nt(wv_ref[...], kv_in).astype(BF16)
        for hd in range(vt.shape[0] // V_DIM):
            vt_ref[s, hd * V_ROWS:hd * V_ROWS + V_DIM, :] = vt[hd * V_DIM:(hd + 1) * V_DIM, :]
            vt_ref[s, hd * V_ROWS + V_DIM:(hd + 1) * V_ROWS, :] = ones_row
        qt_ref[s] = (_dot_nt(wq_ref[...], h_in) * Q_SCALE).astype(BF16)
        gz_ref[s] = (_silu(_dot_nt(wz_ref[...], h_in)) * gsub_ref[...]).astype(BF16)


def _mid(gated_perm, x, mod0, mod1, perm, wo, g_post0, g_kv, g_pre1, wk, wv_t, wq_t, wz_t, gsub):
    bsz, seq, d = x.shape
    tb = TOKEN_BLOCK
    nb = seq // tb
    sub = MID_SUB
    e = gated_perm.shape[-1]
    qk = wk.shape[1]
    av = wv_t.shape[0]
    row = lambda b, j: (b, j, 0)
    const2 = lambda b, j: (0, 0)
    t_spec = lambda n: pl.BlockSpec((None, sub, n, tb), lambda b, j: (b, j, 0, 0))
    return pl.pallas_call(
        _mid_kernel,
        grid=(bsz, nb // sub),
        in_specs=[
            pl.BlockSpec((None, CHUNK, sub * tb // CHUNK, e), lambda b, j: (b, 0, j, 0)),
            pl.BlockSpec((None, sub * tb, d), row),
            pl.BlockSpec((None, 1, 3 * d), lambda b, j: (b, 0, 0)),
            pl.BlockSpec((None, 1, 3 * d), lambda b, j: (b, 0, 0)),
            pl.BlockSpec((tb, tb), const2),
            pl.BlockSpec((e, d), const2),
            pl.BlockSpec((1, d), const2),
            pl.BlockSpec((1, d), const2),
            pl.BlockSpec((1, d), const2),
            pl.BlockSpec((d, qk), const2),
            pl.BlockSpec((av, d), const2),
            pl.BlockSpec((qk, d), const2),
            pl.BlockSpec((av, d), const2),
            pl.BlockSpec((av, 1), const2),
        ],
        out_specs=[
            pl.BlockSpec((None, sub * tb, d), row),
            pl.BlockSpec((None, sub * tb, qk), row),
            t_spec(av // V_DIM * V_ROWS),
            t_spec(qk),
            t_spec(av),
        ],
        out_shape=[
            jax.ShapeDtypeStruct((bsz, seq, d), F32),
            jax.ShapeDtypeStruct((bsz, seq, qk), BF16),
            jax.ShapeDtypeStruct((bsz, nb, av // V_DIM * V_ROWS, tb), BF16),
            jax.ShapeDtypeStruct((bsz, nb, qk, tb), BF16),
            jax.ShapeDtypeStruct((bsz, nb, av, tb), BF16),
        ],
        compiler_params=_params(("parallel", "parallel")),
        name="mid",
    )(gated_perm.reshape(bsz, CHUNK, seq // CHUNK, e), x, mod0.reshape(bsz, 1, 3 * d),
      mod1.reshape(bsz, 1, 3 * d), perm, wo, g_post0.reshape(1, d), g_kv.reshape(1, d),
      g_pre1.reshape(1, d), wk, wv_t, wq_t, wz_t, gsub.reshape(av, 1))


def _attention_kernel(lam_ref, qt_ref, qn_ref, k_ref, vt_ref, gz_ref, bias_ref, h_ref, mod1_ref, wo_ref,
                      gpost_ref, o_ref, qp_ref, m_ref, acc_ref, s_ref, og_ref):
    tq = qt_ref.shape[-1]
    hw = 2 * HEAD_DIM
    heads = qt_ref.shape[0] // hw
    i = pl.program_id(1)

    zero = jnp.zeros((HEAD_DIM, tq), qt_ref.dtype)

    def pad_queries(q_ref, n_heads):
        for h in range(n_heads):
            qt = q_ref[h * hw:(h + 1) * hw, :]
            qp_ref[2 * h] = jnp.concatenate([qt[:HEAD_DIM], zero], axis=0)
            qp_ref[2 * h + 1] = jnp.concatenate([zero, qt[HEAD_DIM:]], axis=0)

    pad_queries(qt_ref, heads)
    n_strips = 2 * heads

    def scores(j, n):
        h = n // 2
        row0 = pl.multiple_of(j * tq, tq)
        return _dot(k_ref[pl.ds(row0, tq), h * hw:(h + 1) * hw], qp_ref[n])

    def step(j, last=False):
        for n in range(n_strips):
            s = s_ref[n % ATT_AHEAD]
            if n + ATT_AHEAD < n_strips:
                s_ref[n % ATT_AHEAD] = scores(j, n + ATT_AHEAD)
            elif not last:
                s_ref[n % ATT_AHEAD] = scores(j + 1, n + ATT_AHEAD - n_strips)
            m_old = m_ref[n]
            if last:
                hq = tq // 2
                s00 = s[:hq, :hq] + bias_ref[...]
                s01 = s[:hq, hq:]
                s11 = s[hq:, hq:] + bias_ref[...]
                m_blk = jnp.concatenate(
                    [jnp.max(s00, axis=0, keepdims=True),
                     jnp.maximum(jnp.max(s01, axis=0, keepdims=True),
                                 jnp.max(s11, axis=0, keepdims=True))], axis=1)
                m_new = jnp.maximum(m_old, m_blk)
                e = lambda v, m: jnp.exp2((v - m).astype(BF16))
                p = jnp.concatenate(
                    [jnp.concatenate([e(s00, m_new[:, :hq]), e(s01, m_new[:, hq:])], axis=1),
                     jnp.concatenate([jnp.zeros((hq, hq), BF16), e(s11, m_new[:, hq:])], axis=1)],
                    axis=0)
            else:
                m_new = jnp.maximum(m_old, jnp.max(s, axis=0, keepdims=True))
                p = jnp.exp2((s - m_new).astype(BF16))
            alpha = jnp.exp2(m_old - m_new)
            m_ref[n] = m_new
            h = n // 2
            vt = vt_ref[j, h * V_ROWS:(h + 1) * V_ROWS, :]
            acc_ref[n] = alpha * acc_ref[n] + _dot(vt, p)

    @pl.when(i == 0)
    def _():
        for n in range(ATT_AHEAD):
            s_ref[n] = scores(0, n)

    m_ref[...] = jnp.full(m_ref.shape, -jnp.inf, F32)
    acc_ref[...] = jnp.zeros(acc_ref.shape, F32)

    def two_steps(t, carry):
        step(2 * t)
        step(2 * t + 1)
        return carry

    lax.fori_loop(0, i // 2, two_steps, 0)

    @pl.when(i % 2 == 1)
    def _():
        step(i - 1)

    step(i, last=True)

    pad_queries(qn_ref, ATT_AHEAD // 2)
    for n in range(ATT_AHEAD):
        s_ref[n] = scores(0, n)

    for h in range(heads):
        r0 = 1.0 / acc_ref[2 * h, V_DIM:V_DIM + 1, :]
        r1 = lam_ref[0] / acc_ref[2 * h + 1, V_DIM:V_DIM + 1, :]
        o = acc_ref[2 * h, :V_DIM, :] * r0 - acc_ref[2 * h + 1, :V_DIM, :] * r1
        rows = slice(h * V_DIM, (h + 1) * V_DIM)
        inv_rms = lax.rsqrt(jnp.mean(o * o, axis=0, keepdims=True) + EPS)
        og_ref[rows, :] = (o * inv_rms * gz_ref[rows, :].astype(F32)).astype(BF16)
        if h % 2 == 1:
            pr = slice((h - 1) * V_DIM, (h + 1) * V_DIM)
            part = _dot_tn(og_ref[pr, :], wo_ref[pr, :])
            y = part if h == 1 else y + part
    d = h_ref.shape[-1]
    gate1 = mod1_ref[:, 2 * d:3 * d]
    o_ref[...] = h_ref[...] + gate1 * _rms_rows(y, gpost_ref[...])


def _attention(lam, q_t, k, v_t, gz_t, h, mod1, wo, g_post1):
    bsz, nb, width, tb = q_t.shape
    seq = nb * tb
    d = h.shape[-1]
    heads = width // (2 * HEAD_DIM)
    assert heads % 2 == 0 and (2 * heads) % ATT_AHEAD == 0
    pos = jnp.arange(tb // 2)
    bias = jnp.where(pos[:, None] <= pos[None, :], 0.0, -jnp.inf).astype(F32)
    blk = pl.BlockSpec((None, None, width, tb), lambda b, i: (b, i, 0, 0))
    once = pl.Buffered(1)
    return pl.pallas_call(
        _attention_kernel,
        grid=(bsz, nb),
        in_specs=[
            pl.BlockSpec(memory_space=pltpu.SMEM),
            blk,
            pl.BlockSpec((None, None, width, tb), lambda b, i: (b, jnp.minimum(i + 1, nb - 1), 0, 0)),
            pl.BlockSpec((None, seq, width), lambda b, i: (b, 0, 0)),
            pl.BlockSpec((None, nb, heads * V_ROWS, tb), lambda b, i: (b, 0, 0, 0)),
            blk,
            pl.BlockSpec((tb // 2, tb // 2), lambda b, i: (0, 0), pipeline_mode=once),
            pl.BlockSpec((None, tb, d), lambda b, i: (b, i, 0)),
            pl.BlockSpec((None, 1, 3 * d), lambda b, i: (b, 0, 0)),
            pl.BlockSpec((width, d), lambda b, i: (0, 0), pipeline_mode=once),
            pl.BlockSpec((1, d), lambda b, i: (0, 0), pipeline_mode=once),
        ],
        out_specs=pl.BlockSpec((None, tb, d), lambda b, i: (b, i, 0)),
        out_shape=jax.ShapeDtypeStruct((bsz, seq, d), F32),
        scratch_shapes=[
            pltpu.VMEM((2 * heads, 2 * HEAD_DIM, tb), BF16),
            pltpu.VMEM((2 * heads, 1, tb), F32),
            pltpu.VMEM((2 * heads, V_ROWS, tb), F32),
            pltpu.VMEM((ATT_AHEAD, tb, tb), F32),
            pltpu.VMEM((width, tb), BF16),
        ],
        compiler_params=pltpu.CompilerParams(dimension_semantics=("parallel", "arbitrary"),
                                             vmem_limit_bytes=ATT_VMEM_LIMIT),
        name="attention",
    )(lam, q_t, q_t, k, v_t, gz_t, bias, h, mod1.reshape(bsz, 1, 3 * d), wo, g_post1.reshape(1, d))


def kernel(x, c, ada_w, ada_b, g_pre, g_post, a_w_in, a_lam_re, a_lam_im, a_log_dt, a_b_re, a_b_im,
           a_c_re, a_c_im, a_d, a_w_glu, a_b_glu, a_w_out, g_kv, w_k, w_v, b_w_in, b_lq1, b_lk1,
           b_lq2, b_lk2, b_g_sub, b_w_out):
    bsz, seq, d = x.shape
    e = a_w_glu.shape[1]
    qk = w_k.shape[1]
    assert seq % (CHUNK * 128) == 0 and seq % GLU_BLOCK == 0 and d % 128 == 0
    assert e % (GROUPS_PER_STEP * SSM_GROUP) == 0

    mod = _modulation(c, ada_w, ada_b)
    perm = _chunk_permutation(TOKEN_BLOCK)

    hperm = _prenorm(x, mod[0], g_pre[0], perm).reshape(bsz, seq, d)
    toep, state_in, state_out, decay = _ssm_operators(
        a_lam_re[0], a_lam_im[0], a_log_dt[0], a_b_re[0], a_b_im[0], a_c_re[0], a_c_im[0], a_d[0])
    w_in_t = a_w_in[0].T.astype(BF16)
    y_t = _ssm(hperm, w_in_t[:e], toep, state_in, state_out, decay)
    gated = _glu(y_t, hperm, a_w_glu[0].T.astype(BF16), a_b_glu[0], w_in_t[e:])

    layer = DEPTH // 2
    lambda_init = 0.8 - 0.6 * math.exp(-0.3 * layer)
    w_b_t = b_w_in[0].T.astype(BF16)
    gsub = jnp.tile(b_g_sub[0] * (1.0 - lambda_init), w_v.shape[1] // V_DIM)
    h, k, v_t, q_t, gz_t = _mid(
        gated, x, mod[0], mod[1], perm, a_w_out[0].astype(BF16), g_post[0], g_kv, g_pre[1],
        w_k.astype(BF16), w_v.T.astype(BF16), w_b_t[:qk], w_b_t[qk:], gsub)

    lam = (jnp.exp(jnp.sum(b_lq1[0] * b_lk1[0])) - jnp.exp(jnp.sum(b_lq2[0] * b_lk2[0]))
           + lambda_init).reshape(1).astype(F32)
    return _attention(lam, q_t, k, v_t, gz_t, h, mod[1], b_w_out[0].astype(BF16), g_post[1])
```

```python
import functools
import math

import jax
import jax.numpy as jnp
from jax import lax
from jax.experimental import pallas as pl
from jax.experimental.pallas import tpu as pltpu

F32 = jnp.float32
BF16 = jnp.bfloat16

EPS = 1e-6
DEPTH = 2
SSM_GROUP = 16
SSM_STATE = 64
CHUNK = 16
GROUPS_PER_STEP = 16
SSM_UNROLL = 8
HEAD_DIM = 64
V_DIM = 2 * HEAD_DIM
V_ROWS = V_DIM + 16
TOKEN_BLOCK = 256
PRENORM_BLOCK = 1024
GLU_BLOCK = 512
MID_SUB = 2
ATT_AHEAD = 8
Q_SCALE = HEAD_DIM ** -0.5 * math.log2(math.e)
VMEM_LIMIT = 48 * 1024 * 1024
ATT_VMEM_LIMIT = 56 * 1024 * 1024


def _params(semantics):
    return pltpu.CompilerParams(dimension_semantics=semantics, vmem_limit_bytes=VMEM_LIMIT)


def _sigmoid(v):
    return 1.0 / (1.0 + jnp.exp(-v))


def _silu(v):
    return v * _sigmoid(v)


def _gelu_tanh(v):
    k = -2.0 * math.sqrt(2.0 / math.pi) * math.log2(math.e)
    return v / (1.0 + jnp.exp2(v * ((k * 0.044715) * (v * v) + k)))


def _rms_rows(v, g):
    return v * lax.rsqrt(jnp.mean(v * v, axis=-1, keepdims=True) + EPS) * g


def _dot(a, b):
    return jnp.dot(a, b, preferred_element_type=F32)


def _dot_nt(a, b):
    return lax.dot_general(a, b, (((1,), (1,)), ((), ())), preferred_element_type=F32)


def _dot_tn(a, b):
    return lax.dot_general(a, b, (((0,), (0,)), ((), ())), preferred_element_type=F32)


def _modulation_kernel(c_ref, w_ref, b_ref, o_ref):
    def split(v):
        hi = v.astype(BF16)
        return hi, (v - hi.astype(F32)).astype(BF16)

    (sh, sl), (wh, wl) = split(_silu(c_ref[...])), split(w_ref[...])
    o_ref[...] = _dot(sh, wh) + _dot(sh, wl) + _dot(sl, wh) + b_ref[...]


def _modulation(c, ada_w, ada_b):
    bsz, d = c.shape
    depth, _, n = ada_w.shape
    tn = 1024
    return pl.pallas_call(
        _modulation_kernel,
        grid=(depth, n // tn),
        in_specs=[
            pl.BlockSpec((bsz, d), lambda l, j: (0, 0)),
            pl.BlockSpec((None, d, tn), lambda l, j: (l, 0, j)),
            pl.BlockSpec((None, 1, tn), lambda l, j: (l, 0, j)),
        ],
        out_specs=pl.BlockSpec((None, bsz, tn), lambda l, j: (l, 0, j)),
        out_shape=jax.ShapeDtypeStruct((depth, bsz, n), F32),
        compiler_params=_params(("parallel", "parallel")),
        name="modulation",
    )(c, ada_w, ada_b.reshape(depth, 1, n))


def _chunk_permutation(n):
    r = jnp.arange(n)
    src = (r % (n // CHUNK)) * CHUNK + r // (n // CHUNK)
    return (src[:, None] == r[None, :]).astype(BF16)


def _prenorm_kernel(x_ref, mod_ref, g_ref, p_ref, o_ref):
    d = x_ref.shape[-1]
    sub = p_ref.shape[0]
    shift = mod_ref[:, 0:d]
    scale = mod_ref[:, d:2 * d]
    for r in range(x_ref.shape[0] // sub):
        x = x_ref[r * sub:(r + 1) * sub, :]
        h = _rms_rows(x, g_ref[...]) * (1.0 + scale) + shift
        hp = _dot(p_ref[...], h.astype(BF16)).astype(BF16)
        o_ref[:, r * (sub // CHUNK):(r + 1) * (sub // CHUNK), :] = hp.reshape(CHUNK, sub // CHUNK, d)


def _prenorm(x, mod0, g_pre0, perm):
    bsz, seq, d = x.shape
    tb = PRENORM_BLOCK
    return pl.pallas_call(
        _prenorm_kernel,
        grid=(bsz, seq // tb),
        in_specs=[
            pl.BlockSpec((None, tb, d), lambda b, j: (b, j, 0)),
            pl.BlockSpec((None, 1, 3 * d), lambda b, j: (b, 0, 0)),
            pl.BlockSpec((1, d), lambda b, j: (0, 0)),
            pl.BlockSpec(perm.shape, lambda b, j: (0, 0)),
        ],
        out_specs=pl.BlockSpec((None, CHUNK, tb // CHUNK, d), lambda b, j: (b, 0, j, 0)),
        out_shape=jax.ShapeDtypeStruct((bsz, CHUNK, seq // CHUNK, d), BF16),
        compiler_params=_params(("parallel", "parallel")),
        name="prenorm",
    )(x, mod0.reshape(bsz, 1, 3 * d), g_pre0.reshape(1, d), perm)


def _operators_kernel(lam_re_ref, lam_im_ref, log_dt_ref, bt_re_ref, bt_im_ref, c_re_ref, c_im_ref,
                      d_ref, toep_ref, sin_ref, sout_ref, decay_ref):
    rows = CHUNK * SSM_GROUP
    lanes = 2 * SSM_STATE
    lam_re, lam_im, dt = lam_re_ref[...], lam_im_ref[...], jnp.exp(log_dt_ref[...])
    ar, ai = lam_re * dt, lam_im * dt

    def apow(k):
        mag = jnp.exp(k * ar)
        return mag * jnp.cos(k * ai), mag * jnp.sin(k * ai)

    pos = lax.broadcasted_iota(jnp.int32, (CHUNK, 1), 0).astype(F32)
    a1r, a1i = apow(jnp.ones((1, 1), F32))
    den = lam_re * lam_re + lam_im * lam_im
    fr = ((a1r - 1.0) * lam_re + a1i * lam_im) / den
    fi = (a1i * lam_re - (a1r - 1.0) * lam_im) / den
    bbr = fr * bt_re_ref[...] - fi * bt_im_ref[...]
    bbi = fr * bt_im_ref[...] + fi * bt_re_ref[...]

    r_idx = lax.broadcasted_iota(jnp.int32, (rows, CHUNK), 0)
    k_idx = lax.broadcasted_iota(jnp.int32, (rows, CHUNK), 1)
    rep = (r_idx // SSM_GROUP == k_idx).astype(BF16)
    tile = (r_idx % SSM_GROUP == k_idx).astype(BF16)
    lane_tile = (lax.broadcasted_iota(jnp.int32, (SSM_GROUP, rows), 1) % SSM_GROUP
                 == lax.broadcasted_iota(jnp.int32, (SSM_GROUP, rows), 0)).astype(BF16)

    def split(v):
        hi = v.astype(BF16)
        return hi, (v - hi.astype(F32)).astype(BF16)

    def expand(sel, v):
        hi, lo = split(v)
        return _dot(sel, hi) + _dot(sel, lo)

    def dot3(x, y):
        (xh, xl), (yh, yl) = split(x), split(y)
        return _dot_nt(xh, yh) + _dot_nt(xh, yl) + _dot_nt(xl, yh)

    def times(xr, xi, yr, yi):
        return xr * yr - xi * yi, xr * yi + xi * yr

    left = lax.broadcasted_iota(jnp.int32, (rows, lanes), 1) < SSM_STATE
    halves = lambda v: (jnp.where(left, v, 0.0), jnp.where(left, 0.0, v))

    qr, qi = times(*apow(CHUNK - 1.0 - pos), fr, fi)
    sr, si = times(expand(rep, qr), expand(rep, qi),
                   expand(tile, bt_re_ref[...]), expand(tile, bt_im_ref[...]))
    cr, ci = expand(tile, c_re_ref[...]), expand(tile, c_im_ref[...])
    wr, wi = apow(pos + 1.0)
    our, oui = times(cr, ci, expand(rep, wr), expand(rep, wi))
    for h, (s_r, s_i, o_r, o_i) in enumerate(zip(halves(sr), halves(si), halves(our), halves(oui))):
        sin_ref[h, 0] = s_r.astype(BF16)
        sin_ref[h, 1] = s_i.astype(BF16)
        sout_ref[h, 0] = o_r.astype(BF16)
        sout_ref[h, 1] = (-o_i).astype(BF16)

    pr, pi = apow(pos)
    lr, li = times(cr, ci, expand(rep, pr), expand(rep, pi))
    lane_blk = lax.broadcasted_iota(jnp.int32, (rows, rows), 1) // SSM_GROUP
    diag = (lax.broadcasted_iota(jnp.int32, (rows, rows), 0)
            == lax.broadcasted_iota(jnp.int32, (rows, rows), 1))
    for h, (l_r, l_i) in enumerate(zip(halves(lr), halves(li))):
        kern = dot3(l_r, bbr) - dot3(l_i, bbi)
        k_hi, k_lo = split(kern)
        wide = _dot(k_hi, lane_tile) + _dot(k_lo, lane_tile)
        toep = jnp.where(diag, d_ref[h], 0.0)
        for p in range(CHUNK):
            n = p * SSM_GROUP
            delayed = wide if p == 0 else jnp.concatenate(
                [jnp.zeros((n, rows), F32), wide[:rows - n]], axis=0)
            toep = toep + jnp.where(lane_blk == p, delayed, 0.0)
        toep_ref[h] = toep.astype(BF16)

    dr, di = apow(jnp.full((1, 1), float(CHUNK), F32))
    sub = lax.broadcasted_iota(jnp.int32, (8, lanes), 0)
    decay_ref[...] = jnp.where(sub == 0, dr, jnp.where(sub == 1, di, 0.0))


def _ssm_operators(lam_re, lam_im, log_dt, b_re, b_im, c_re, c_im, d_skip):
    g, p = lam_re.shape
    cpg = SSM_GROUP
    rows = CHUNK * cpg
    pairs = g // 2
    lanes = 2 * p
    row_pair = lambda v: v.reshape(pairs, 1, lanes)
    mat_pair = lambda m: m.reshape(pairs, 2, cpg, p).transpose(0, 2, 1, 3).reshape(pairs, cpg, lanes)
    vec = pl.BlockSpec((None, 1, lanes), lambda q: (q, 0, 0))
    mat = pl.BlockSpec((None, cpg, lanes), lambda q: (q, 0, 0))
    return pl.pallas_call(
        _operators_kernel,
        grid=(pairs,),
        in_specs=[vec, vec, vec, mat, mat, mat, mat,
                  pl.BlockSpec((2, 1, rows), lambda q: (q, 0, 0))],
        out_specs=[
            pl.BlockSpec((2, rows, rows), lambda q: (q, 0, 0)),
            pl.BlockSpec((2, 2, rows, lanes), lambda q: (q, 0, 0, 0)),
            pl.BlockSpec((2, 2, rows, lanes), lambda q: (q, 0, 0, 0)),
            pl.BlockSpec((None, 8, lanes), lambda q: (q, 0, 0)),
        ],
        out_shape=[
            jax.ShapeDtypeStruct((g, rows, rows), BF16),
            jax.ShapeDtypeStruct((g, 2, rows, lanes), BF16),
            jax.ShapeDtypeStruct((g, 2, rows, lanes), BF16),
            jax.ShapeDtypeStruct((pairs, 8, lanes), F32),
        ],
        compiler_params=_params(("parallel",)),
        name="operators",
    )(row_pair(lam_re), row_pair(lam_im),
      row_pair(jnp.broadcast_to(log_dt[:, None], (g, p))),
      mat_pair(b_re.transpose(0, 2, 1)), mat_pair(b_im.transpose(0, 2, 1)),
      mat_pair(c_re), mat_pair(c_im),
      jnp.tile(d_skip.reshape(g, 1, cpg), (1, 1, CHUNK)))


def _ssm_kernel(h_ref, wu_ref, toep_ref, sin_ref, sout_ref, decay_ref, y_ref, xs_ref):
    n_chunks = y_ref.shape[-1] // CHUNK
    n_state = SSM_STATE
    j = pl.program_id(1)
    last = pl.num_programs(1) - 1
    fill, drain = j % 2, (j + 1) % 2

    def project(p):
        r0 = pl.multiple_of(p * n_chunks, n_chunks)
        u = _dot_nt(wu_ref[...], h_ref[pl.ds(r0, n_chunks), :]).astype(BF16)
        c0 = pl.multiple_of(p * SSM_GROUP, SSM_GROUP)
        for g in range(GROUPS_PER_STEP):
            xs_ref[fill, g, pl.ds(c0, SSM_GROUP), :] = u[g * SSM_GROUP:(g + 1) * SSM_GROUP, :]

    row = lax.broadcasted_iota(jnp.int32, (n_chunks, 2 * n_state), 0)

    def shift_rows(v, s):
        if s % 8 == 0:
            return jnp.concatenate([jnp.zeros((s, v.shape[1]), v.dtype), v[:-s]], axis=0)
        return jnp.where(row >= s, pltpu.roll(v, s, 0), 0.0)

    def gains(q):
        g0, g1 = 2 * q, 2 * q + 1
        x0, x1 = xs_ref[drain, g0], xs_ref[drain, g1]
        inc_r = _dot_tn(x0, sin_ref[g0, 0]) + _dot_tn(x1, sin_ref[g1, 0])
        inc_i = _dot_tn(x0, sin_ref[g0, 1]) + _dot_tn(x1, sin_ref[g1, 1])
        return inc_r, inc_i

    def finish(q, inc_r, inc_i):
        g0, g1 = 2 * q, 2 * q + 1
        er, ei = shift_rows(inc_r, 1), shift_rows(inc_i, 1)
        ar, ai = decay_ref[q, 0:1, :], decay_ref[q, 1:2, :]
        s = 1
        while s < n_chunks:
            if s % 8 == 0:
                dr = ar * er[:-s] - ai * ei[:-s]
                di = ar * ei[:-s] + ai * er[:-s]
                er = jnp.concatenate([er[:s], er[s:] + dr], axis=0)
                ei = jnp.concatenate([ei[:s], ei[s:] + di], axis=0)
            else:
                sr, si = shift_rows(er, s), shift_rows(ei, s)
                er, ei = er + (ar * sr - ai * si), ei + (ar * si + ai * sr)
            ar, ai = ar * ar - ai * ai, 2.0 * (ar * ai)
            s *= 2
        sr, si = er.astype(BF16), ei.astype(BF16)
        for g in (g0, g1):
            y = (_dot(toep_ref[g], xs_ref[drain, g]) + _dot_nt(sout_ref[g, 0], sr)
                 + _dot_nt(sout_ref[g, 1], si))
            act = _gelu_tanh(y).astype(BF16)
            row0 = pl.multiple_of(g * SSM_GROUP, SSM_GROUP)
            for p in range(CHUNK):
                y_ref[pl.ds(row0, SSM_GROUP), p * n_chunks:(p + 1) * n_chunks] = (
                    act[p * SSM_GROUP:(p + 1) * SSM_GROUP, :])

    n_iter = GROUPS_PER_STEP // 2 // SSM_UNROLL
    per_iter = CHUNK // n_iter

    def body(it, carry, with_scan, with_projection):
        qs = [it * SSM_UNROLL + u for u in range(SSM_UNROLL)]
        started = [gains(q) for q in qs] if with_scan else []
        if with_projection:
            for pp in range(per_iter):
                project(it * per_iter + pp)
        for q, inc in zip(qs, started):
            finish(q, *inc)
        return carry

    @pl.when(j == 0)
    def _():
        lax.fori_loop(0, n_iter, functools.partial(body, with_scan=False, with_projection=True), 0)

    @pl.when(jnp.logical_and(j > 0, j < last))
    def _():
        lax.fori_loop(0, n_iter, functools.partial(body, with_scan=True, with_projection=True), 0)

    @pl.when(j == last)
    def _():
        lax.fori_loop(0, n_iter, functools.partial(body, with_scan=True, with_projection=False), 0)


def _ssm(hperm, wu_t, toep, state_in, state_out, decay):
    bsz, seq, d = hperm.shape
    e = wu_t.shape[0]
    cb = GROUPS_PER_STEP * SSM_GROUP
    rows = CHUNK * SSM_GROUP
    n_chunks = seq // CHUNK
    gps = GROUPS_PER_STEP
    nblk = e // cb
    assert (gps // 2) % SSM_UNROLL == 0 and CHUNK % (gps // 2 // SSM_UNROLL) == 0
    proj = lambda j: jnp.minimum(j, nblk - 1)
    scan = lambda j: jnp.maximum(j - 1, 0)
    return pl.pallas_call(
        _ssm_kernel,
        grid=(bsz, nblk + 1),
        in_specs=[
            pl.BlockSpec((None, seq, d), lambda b, j: (b, 0, 0)),
            pl.BlockSpec((cb, d), lambda b, j: (proj(j), 0)),
            pl.BlockSpec((gps, rows, rows), lambda b, j: (scan(j), 0, 0)),
            pl.BlockSpec((gps, 2, rows, 2 * SSM_STATE), lambda b, j: (scan(j), 0, 0, 0)),
            pl.BlockSpec((gps, 2, rows, 2 * SSM_STATE), lambda b, j: (scan(j), 0, 0, 0)),
            pl.BlockSpec((gps // 2, 8, 2 * SSM_STATE), lambda b, j: (scan(j), 0, 0)),
        ],
        out_specs=pl.BlockSpec((None, cb, seq), lambda b, j: (b, scan(j), 0)),
        out_shape=jax.ShapeDtypeStruct((bsz, e, seq), BF16),
        scratch_shapes=[pltpu.VMEM((2, gps, rows, n_chunks), BF16)],
        compiler_params=_params(("parallel", "arbitrary")),
        name="ssm",
    )(hperm, wu_t, toep, state_in, state_out, decay)


def _glu_kernel(y_ref, h_ref, wg_ref, bg_ref, wz_ref, o_ref):
    e = y_ref.shape[0]
    rb = 256
    ya = y_ref[...]
    hb = h_ref[...]
    for r in range(e // rb):
        rows = slice(r * rb, (r + 1) * rb)
        gl = _dot(wg_ref[rows, :], ya) + bg_ref[rows, :]
        z = _dot_nt(wz_ref[rows, :], hb)
        yr = y_ref[rows, :].astype(F32)
        gated = yr * _sigmoid(gl) * _silu(z)
        o_ref[:, rows] = gated.T.astype(BF16)


def _glu(y_t, hperm, wglu_t, b_glu, wz_t):
    bsz, e, seq = y_t.shape
    d = hperm.shape[-1]
    tn = GLU_BLOCK
    return pl.pallas_call(
        _glu_kernel,
        grid=(bsz, seq // tn),
        in_specs=[
            pl.BlockSpec((None, e, tn), lambda b, j: (b, 0, j)),
            pl.BlockSpec((None, tn, d), lambda b, j: (b, j, 0)),
            pl.BlockSpec((e, e), lambda b, j: (0, 0)),
            pl.BlockSpec((e, 1), lambda b, j: (0, 0)),
            pl.BlockSpec((e, d), lambda b, j: (0, 0)),
        ],
        out_specs=pl.BlockSpec((None, tn, e), lambda b, j: (b, j, 0)),
        out_shape=jax.ShapeDtypeStruct((bsz, seq, e), BF16),
        compiler_params=_params(("parallel", "parallel")),
        name="glu",
    )(y_t, hperm, wglu_t, b_glu.reshape(e, 1), wz_t)


def _mid_kernel(gp_ref, x_ref, mod0_ref, mod1_ref, p_ref, wo_ref, gpost_ref, gkv_ref, gpre_ref,
                wk_ref, wv_ref, wq_ref, wz_ref, gsub_ref, h_ref, k_ref, vt_ref, qt_ref, gz_ref):
    d = x_ref.shape[-1]
    tb = p_ref.shape[0]
    subs = range(x_ref.shape[0] // tb)
    cps = tb // CHUNK
    gate0 = mod0_ref[:, 2 * d:3 * d]
    shift1 = mod1_ref[:, 0:d]
    scale1 = mod1_ref[:, d:2 * d]
    ones_row = (lax.broadcasted_iota(jnp.int32, (V_ROWS - V_DIM, tb), 0) == 0).astype(BF16)

    ys = []
    for s in subs:
        gp = gp_ref[:, s * cps:(s + 1) * cps, :].reshape(tb, -1)
        ys.append(_dot(_dot(p_ref[...], gp).astype(BF16), wo_ref[...]))
    ins = []
    for s, y in zip(subs, ys):
        rows = slice(s * tb, (s + 1) * tb)
        h = x_ref[rows, :] + gate0 * _rms_rows(y, gpost_ref[...])
        h_ref[rows, :] = h
        kv_in = _rms_rows(h, gkv_ref[...]).astype(BF16)
        h_in = (_rms_rows(h, gpre_ref[...]) * (1.0 + scale1) + shift1).astype(BF16)
        ins.append((kv_in, h_in))
    for s, (kv_in, h_in) in zip(subs, ins):
        k_ref[s * tb:(s + 1) * tb, :] = _dot(kv_in, wk_ref[...]).astype(BF16)
        vt = _dot_nt(wv_ref[...], kv_in).astype(BF16)
        for hd in range(vt.shape[0] // V_DIM):
            vt_ref[s, hd * V_ROWS:hd * V_ROWS + V_DIM, :] = vt[hd * V_DIM:(hd + 1) * V_DIM, :]
            vt_ref[s, hd * V_ROWS + V_DIM:(hd + 1) * V_ROWS, :] = ones_row
        qt_ref[s] = (_dot_nt(wq_ref[...], h_in) * Q_SCALE).astype(BF16)
        gz_ref[s] = (_silu(_dot_nt(wz_ref[...], h_in)) * gsub_ref[...]).astype(BF16)


def _mid(gated_perm, x, mod0, mod1, perm, wo, g_post0, g_kv, g_pre1, wk, wv_t, wq_t, wz_t, gsub):
    bsz, seq, d = x.shape
    tb = TOKEN_BLOCK
    nb = seq // tb
    sub = MID_SUB
    e = gated_perm.shape[-1]
    qk = wk.shape[1]
    av = wv_t.shape[0]
    row = lambda b, j: (b, j, 0)
    const2 = lambda b, j: (0, 0)
    t_spec = lambda n: pl.BlockSpec((None, sub, n, tb), lambda b, j: (b, j, 0, 0))
    return pl.pallas_call(
        _mid_kernel,
        grid=(bsz, nb // sub),
        in_specs=[
            pl.BlockSpec((None, CHUNK, sub * tb // CHUNK, e), lambda b, j: (b, 0, j, 0)),
            pl.BlockSpec((None, sub * tb, d), row),
            pl.BlockSpec((None, 1, 3 * d), lambda b, j: (b, 0, 0)),
            pl.BlockSpec((None, 1, 3 * d), lambda b, j: (b, 0, 0)),
            pl.BlockSpec((tb, tb), const2),
            pl.BlockSpec((e, d), const2),
            pl.BlockSpec((1, d), const2),
            pl.BlockSpec((1, d), const2),
            pl.BlockSpec((1, d), const2),
            pl.BlockSpec((d, qk), const2),
            pl.BlockSpec((av, d), const2),
            pl.BlockSpec((qk, d), const2),
            pl.BlockSpec((av, d), const2),
            pl.BlockSpec((av, 1), const2),
        ],
        out_specs=[
            pl.BlockSpec((None, sub * tb, d), row),
            pl.BlockSpec((None, sub * tb, qk), row),
            t_spec(av // V_DIM * V_ROWS),
            t_spec(qk),
            t_spec(av),
        ],
        out_shape=[
            jax.ShapeDtypeStruct((bsz, seq, d), F32),
            jax.ShapeDtypeStruct((bsz, seq, qk), BF16),
            jax.ShapeDtypeStruct((bsz, nb, av // V_DIM * V_ROWS, tb), BF16),
            jax.ShapeDtypeStruct((bsz, nb, qk, tb), BF16),
            jax.ShapeDtypeStruct((bsz, nb, av, tb), BF16),
        ],
        compiler_params=_params(("parallel", "parallel")),
        name="mid",
    )(gated_perm.reshape(bsz, CHUNK, seq // CHUNK, e), x, mod0.reshape(bsz, 1, 3 * d),
      mod1.reshape(bsz, 1, 3 * d), perm, wo, g_post0.reshape(1, d), g_kv.reshape(1, d),
      g_pre1.reshape(1, d), wk, wv_t, wq_t, wz_t, gsub.reshape(av, 1))


def _attention_kernel(lam_ref, qt_ref, qn_ref, k_ref, vt_ref, gz_ref, bias_ref, h_ref, mod1_ref, wo_ref,
                      gpost_ref, o_ref, qp_ref, m_ref, acc_ref, s_ref, og_ref):
    tq = qt_ref.shape[-1]
    hw = 2 * HEAD_DIM
    heads = qt_ref.shape[0] // hw
    i = pl.program_id(1)

    zero = jnp.zeros((HEAD_DIM, tq), qt_ref.dtype)

    def pad_queries(q_ref, n_heads):
        for h in range(n_heads):
            qt = q_ref[h * hw:(h + 1) * hw, :]
            qp_ref[2 * h] = jnp.concatenate([qt[:HEAD_DIM], zero], axis=0)
            qp_ref[2 * h + 1] = jnp.concatenate([zero, qt[HEAD_DIM:]], axis=0)

    pad_queries(qt_ref, heads)
    n_strips = 2 * heads

    def scores(j, n):
        h = n // 2
        row0 = pl.multiple_of(j * tq, tq)
        return _dot(k_ref[pl.ds(row0, tq), h * hw:(h + 1) * hw], qp_ref[n])

    def step(j, last=False):
        for n in range(n_strips):
            s = s_ref[n % ATT_AHEAD]
            if n + ATT_AHEAD < n_strips:
                s_ref[n % ATT_AHEAD] = scores(j, n + ATT_AHEAD)
            elif not last:
                s_ref[n % ATT_AHEAD] = scores(j + 1, n + ATT_AHEAD - n_strips)
            m_old = m_ref[n]
            if last:
                hq = tq // 2
                s00 = s[:hq, :hq] + bias_ref[...]
                s01 = s[:hq, hq:]
                s11 = s[hq:, hq:] + bias_ref[...]
                m_blk = jnp.concatenate(
                    [jnp.max(s00, axis=0, keepdims=True),
                     jnp.maximum(jnp.max(s01, axis=0, keepdims=True),
                                 jnp.max(s11, axis=0, keepdims=True))], axis=1)
                m_new = jnp.maximum(m_old, m_blk)
                e = lambda v, m: jnp.exp2((v - m).astype(BF16))
                p = jnp.concatenate(
                    [jnp.concatenate([e(s00, m_new[:, :hq]), e(s01, m_new[:, hq:])], axis=1),
                     jnp.concatenate([jnp.zeros((hq, hq), BF16), e(s11, m_new[:, hq:])], axis=1)],
                    axis=0)
            else:
                m_new = jnp.maximum(m_old, jnp.max(s, axis=0, keepdims=True))
                p = jnp.exp2((s - m_new).astype(BF16))
            alpha = jnp.exp2(m_old - m_new)
            m_ref[n] = m_new
            h = n // 2
            vt = vt_ref[j, h * V_ROWS:(h + 1) * V_ROWS, :]
            acc_ref[n] = alpha * acc_ref[n] + _dot(vt, p)

    @pl.when(i == 0)
    def _():
        for n in range(ATT_AHEAD):
            s_ref[n] = scores(0, n)

    m_ref[...] = jnp.full(m_ref.shape, -jnp.inf, F32)
    acc_ref[...] = jnp.zeros(acc_ref.shape, F32)

    @pl.when(i == 0)
    def _():
        step(0, last=True)

    @pl.when(i > 0)
    def _():
        first = (i - 1) % 2

        @pl.when(first == 1)
        def _():
            step(0)

        def two_steps(t, carry):
            step(first + 2 * t)
            step(first + 2 * t + 1)
            return carry

        lax.fori_loop(0, (i - 1) // 2, two_steps, 0)
        step(i - 1)
        step(i, last=True)

    pad_queries(qn_ref, ATT_AHEAD // 2)
    for n in range(ATT_AHEAD):
        s_ref[n] = scores(0, n)

    for h in range(heads):
        r0 = 1.0 / acc_ref[2 * h, V_DIM:V_DIM + 1, :]
        r1 = lam_ref[0] / acc_ref[2 * h + 1, V_DIM:V_DIM + 1, :]
        o = acc_ref[2 * h, :V_DIM, :] * r0 - acc_ref[2 * h + 1, :V_DIM, :] * r1
        rows = slice(h * V_DIM, (h + 1) * V_DIM)
        inv_rms = lax.rsqrt(jnp.mean(o * o, axis=0, keepdims=True) + EPS)
        og_ref[rows, :] = (o * inv_rms * gz_ref[rows, :].astype(F32)).astype(BF16)
        if h % 2 == 1:
            pr = slice((h - 1) * V_DIM, (h + 1) * V_DIM)
            part = _dot_tn(og_ref[pr, :], wo_ref[pr, :])
            y = part if h == 1 else y + part
    d = h_ref.shape[-1]
    gate1 = mod1_ref[:, 2 * d:3 * d]
    o_ref[...] = h_ref[...] + gate1 * _rms_rows(y, gpost_ref[...])


def _attention(lam, q_t, k, v_t, gz_t, h, mod1, wo, g_post1):
    bsz, nb, width, tb = q_t.shape
    seq = nb * tb
    d = h.shape[-1]
    heads = width // (2 * HEAD_DIM)
    assert heads % 2 == 0 and (2 * heads) % ATT_AHEAD == 0
    pos = jnp.arange(tb // 2)
    bias = jnp.where(pos[:, None] <= pos[None, :], 0.0, -jnp.inf).astype(F32)
    blk = pl.BlockSpec((None, None, width, tb), lambda b, i: (b, i, 0, 0))
    once = pl.Buffered(1)
    return pl.pallas_call(
        _attention_kernel,
        grid=(bsz, nb),
        in_specs=[
            pl.BlockSpec(memory_space=pltpu.SMEM),
            blk,
            pl.BlockSpec((None, None, width, tb), lambda b, i: (b, jnp.minimum(i + 1, nb - 1), 0, 0)),
            pl.BlockSpec((None, seq, width), lambda b, i: (b, 0, 0)),
            pl.BlockSpec((None, nb, heads * V_ROWS, tb), lambda b, i: (b, 0, 0, 0)),
            blk,
            pl.BlockSpec((tb // 2, tb // 2), lambda b, i: (0, 0), pipeline_mode=once),
            pl.BlockSpec((None, tb, d), lambda b, i: (b, i, 0)),
            pl.BlockSpec((None, 1, 3 * d), lambda b, i: (b, 0, 0)),
            pl.BlockSpec((width, d), lambda b, i: (0, 0), pipeline_mode=once),
            pl.BlockSpec((1, d), lambda b, i: (0, 0), pipeline_mode=once),
        ],
        out_specs=pl.BlockSpec((None, tb, d), lambda b, i: (b, i, 0)),
        out_shape=jax.ShapeDtypeStruct((bsz, seq, d), F32),
        scratch_shapes=[
            pltpu.VMEM((2 * heads, 2 * HEAD_DIM, tb), BF16),
            pltpu.VMEM((2 * heads, 1, tb), F32),
            pltpu.VMEM((2 * heads, V_ROWS, tb), F32),
            pltpu.VMEM((ATT_AHEAD, tb, tb), F32),
            pltpu.VMEM((width, tb), BF16),
        ],
        compiler_params=pltpu.CompilerParams(dimension_semantics=("parallel", "arbitrary"),
                                             vmem_limit_bytes=ATT_VMEM_LIMIT),
        name="attention",
    )(lam, q_t, q_t, k, v_t, gz_t, bias, h, mod1.reshape(bsz, 1, 3 * d), wo, g_post1.reshape(1, d))


def kernel(x, c, ada_w, ada_b, g_pre, g_post, a_w_in, a_lam_re, a_lam_im, a_log_dt, a_b_re, a_b_im,
           a_c_re, a_c_im, a_d, a_w_glu, a_b_glu, a_w_out, g_kv, w_k, w_v, b_w_in, b_lq1, b_lk1,
           b_lq2, b_lk2, b_g_sub, b_w_out):
    bsz, seq, d = x.shape
    e = a_w_glu.shape[1]
    qk = w_k.shape[1]
    assert seq % (CHUNK * 128) == 0 and seq % GLU_BLOCK == 0 and d % 128 == 0
    assert e % (GROUPS_PER_STEP * SSM_GROUP) == 0

    mod = _modulation(c, ada_w, ada_b)
    perm = _chunk_permutation(TOKEN_BLOCK)

    hperm = _prenorm(x, mod[0], g_pre[0], perm).reshape(bsz, seq, d)
    toep, state_in, state_out, decay = _ssm_operators(
        a_lam_re[0], a_lam_im[0], a_log_dt[0], a_b_re[0], a_b_im[0], a_c_re[0], a_c_im[0], a_d[0])
    w_in_t = a_w_in[0].T.astype(BF16)
    y_t = _ssm(hperm, w_in_t[:e], toep, state_in, state_out, decay)
    gated = _glu(y_t, hperm, a_w_glu[0].T.astype(BF16), a_b_glu[0], w_in_t[e:])

    layer = DEPTH // 2
    lambda_init = 0.8 - 0.6 * math.exp(-0.3 * layer)
    w_b_t = b_w_in[0].T.astype(BF16)
    gsub = jnp.tile(b_g_sub[0] * (1.0 - lambda_init), w_v.shape[1] // V_DIM)
    h, k, v_t, q_t, gz_t = _mid(
        gated, x, mod[0], mod[1], perm, a_w_out[0].astype(BF16), g_post[0], g_kv, g_pre[1],
        w_k.astype(BF16), w_v.T.astype(BF16), w_b_t[:qk], w_b_t[qk:], gsub)

    lam = (jnp.exp(jnp.sum(b_lq1[0] * b_lk1[0])) - jnp.exp(jnp.sum(b_lq2[0] * b_lk2[0]))
           + lambda_init).reshape(1).astype(F32)
    return _attention(lam, q_t, k, v_t, gz_t, h, mod[1], b_w_out[0].astype(BF16), g_post[1])
```

```python
import functools
import math

import jax
import jax.numpy as jnp
from jax import lax
from jax.experimental import pallas as pl
from jax.experimental.pallas import tpu as pltpu

F32 = jnp.float32
BF16 = jnp.bfloat16

EPS = 1e-6
DEPTH = 2
SSM_GROUP = 16
SSM_STATE = 64
CHUNK = 16
GROUPS_PER_STEP = 16
SSM_UNROLL = 8
HEAD_DIM = 64
V_DIM = 2 * HEAD_DIM
V_ROWS = V_DIM + 16
TOKEN_BLOCK = 256
PRENORM_BLOCK = 1024
GLU_BLOCK = 1024
MID_SUB = 2
ATT_AHEAD = 8
Q_SCALE = HEAD_DIM ** -0.5 * math.log2(math.e)
VMEM_LIMIT = 48 * 1024 * 1024
ATT_VMEM_LIMIT = 56 * 1024 * 1024


def _params(semantics):
    return pltpu.CompilerParams(dimension_semantics=semantics, vmem_limit_bytes=VMEM_LIMIT)


def _sigmoid(v):
    return 1.0 / (1.0 + jnp.exp(-v))


def _silu(v):
    return v * _sigmoid(v)


def _gelu_tanh(v):
    k = -2.0 * math.sqrt(2.0 / math.pi) * math.log2(math.e)
    return v / (1.0 + jnp.exp2(v * ((k * 0.044715) * (v * v) + k)))


def _rms_rows(v, g):
    return v * lax.rsqrt(jnp.mean(v * v, axis=-1, keepdims=True) + EPS) * g


def _dot(a, b):
    return jnp.dot(a, b, preferred_element_type=F32)


def _dot_nt(a, b):
    return lax.dot_general(a, b, (((1,), (1,)), ((), ())), preferred_element_type=F32)


def _dot_tn(a, b):
    return lax.dot_general(a, b, (((0,), (0,)), ((), ())), preferred_element_type=F32)


def _modulation_kernel(c_ref, w_ref, b_ref, o_ref):
    def split(v):
        hi = v.astype(BF16)
        return hi, (v - hi.astype(F32)).astype(BF16)

    (sh, sl), (wh, wl) = split(_silu(c_ref[...])), split(w_ref[...])
    o_ref[...] = _dot(sh, wh) + _dot(sh, wl) + _dot(sl, wh) + b_ref[...]


def _modulation(c, ada_w, ada_b):
    bsz, d = c.shape
    depth, _, n = ada_w.shape
    tn = 1024
    return pl.pallas_call(
        _modulation_kernel,
        grid=(depth, n // tn),
        in_specs=[
            pl.BlockSpec((bsz, d), lambda l, j: (0, 0)),
            pl.BlockSpec((None, d, tn), lambda l, j: (l, 0, j)),
            pl.BlockSpec((None, 1, tn), lambda l, j: (l, 0, j)),
        ],
        out_specs=pl.BlockSpec((None, bsz, tn), lambda l, j: (l, 0, j)),
        out_shape=jax.ShapeDtypeStruct((depth, bsz, n), F32),
        compiler_params=_params(("parallel", "parallel")),
        name="modulation",
    )(c, ada_w, ada_b.reshape(depth, 1, n))


def _chunk_permutation(n):
    r = jnp.arange(n)
    src = (r % (n // CHUNK)) * CHUNK + r // (n // CHUNK)
    return (src[:, None] == r[None, :]).astype(BF16)


def _prenorm_kernel(x_ref, mod_ref, g_ref, p_ref, o_ref):
    d = x_ref.shape[-1]
    sub = p_ref.shape[0]
    shift = mod_ref[:, 0:d]
    scale = mod_ref[:, d:2 * d]
    for r in range(x_ref.shape[0] // sub):
        x = x_ref[r * sub:(r + 1) * sub, :]
        h = _rms_rows(x, g_ref[...]) * (1.0 + scale) + shift
        hp = _dot(p_ref[...], h.astype(BF16)).astype(BF16)
        o_ref[:, r * (sub // CHUNK):(r + 1) * (sub // CHUNK), :] = hp.reshape(CHUNK, sub // CHUNK, d)


def _operators_kernel(lam_re_ref, lam_im_ref, log_dt_ref, bt_re_ref, bt_im_ref, c_re_ref, c_im_ref,
                      d_ref, toep_ref, sin_ref, sout_ref, decay_ref):
    rows = CHUNK * SSM_GROUP
    lanes = 2 * SSM_STATE
    lam_re, lam_im, dt = lam_re_ref[...], lam_im_ref[...], jnp.exp(log_dt_ref[...])
    ar, ai = lam_re * dt, lam_im * dt

    def apow(k):
        mag = jnp.exp(k * ar)
        return mag * jnp.cos(k * ai), mag * jnp.sin(k * ai)

    pos = lax.broadcasted_iota(jnp.int32, (CHUNK, 1), 0).astype(F32)
    a1r, a1i = apow(jnp.ones((1, 1), F32))
    den = lam_re * lam_re + lam_im * lam_im
    fr = ((a1r - 1.0) * lam_re + a1i * lam_im) / den
    fi = (a1i * lam_re - (a1r - 1.0) * lam_im) / den
    bbr = fr * bt_re_ref[...] - fi * bt_im_ref[...]
    bbi = fr * bt_im_ref[...] + fi * bt_re_ref[...]

    r_idx = lax.broadcasted_iota(jnp.int32, (rows, CHUNK), 0)
    k_idx = lax.broadcasted_iota(jnp.int32, (rows, CHUNK), 1)
    rep = (r_idx // SSM_GROUP == k_idx).astype(BF16)
    tile = (r_idx % SSM_GROUP == k_idx).astype(BF16)
    lane_tile = (lax.broadcasted_iota(jnp.int32, (SSM_GROUP, rows), 1) % SSM_GROUP
                 == lax.broadcasted_iota(jnp.int32, (SSM_GROUP, rows), 0)).astype(BF16)

    def split(v):
        hi = v.astype(BF16)
        return hi, (v - hi.astype(F32)).astype(BF16)

    def expand(sel, v):
        hi, lo = split(v)
        return _dot(sel, hi) + _dot(sel, lo)

    def dot3(x, y):
        (xh, xl), (yh, yl) = split(x), split(y)
        return _dot_nt(xh, yh) + _dot_nt(xh, yl) + _dot_nt(xl, yh)

    def times(xr, xi, yr, yi):
        return xr * yr - xi * yi, xr * yi + xi * yr

    left = lax.broadcasted_iota(jnp.int32, (rows, lanes), 1) < SSM_STATE
    halves = lambda v: (jnp.where(left, v, 0.0), jnp.where(left, 0.0, v))

    qr, qi = times(*apow(CHUNK - 1.0 - pos), fr, fi)
    sr, si = times(expand(rep, qr), expand(rep, qi),
                   expand(tile, bt_re_ref[...]), expand(tile, bt_im_ref[...]))
    cr, ci = expand(tile, c_re_ref[...]), expand(tile, c_im_ref[...])
    wr, wi = apow(pos + 1.0)
    our, oui = times(cr, ci, expand(rep, wr), expand(rep, wi))
    for h, (s_r, s_i, o_r, o_i) in enumerate(zip(halves(sr), halves(si), halves(our), halves(oui))):
        sin_ref[h, 0] = s_r.astype(BF16)
        sin_ref[h, 1] = s_i.astype(BF16)
        sout_ref[h, 0] = o_r.astype(BF16)
        sout_ref[h, 1] = (-o_i).astype(BF16)

    pr, pi = apow(pos)
    lr, li = times(cr, ci, expand(rep, pr), expand(rep, pi))
    lane_blk = lax.broadcasted_iota(jnp.int32, (rows, rows), 1) // SSM_GROUP
    diag = (lax.broadcasted_iota(jnp.int32, (rows, rows), 0)
            == lax.broadcasted_iota(jnp.int32, (rows, rows), 1))
    for h, (l_r, l_i) in enumerate(zip(halves(lr), halves(li))):
        kern = dot3(l_r, bbr) - dot3(l_i, bbi)
        k_hi, k_lo = split(kern)
        wide = _dot(k_hi, lane_tile) + _dot(k_lo, lane_tile)
        toep = jnp.where(diag, d_ref[h], 0.0)
        for p in range(CHUNK):
            n = p * SSM_GROUP
            delayed = wide if p == 0 else jnp.concatenate(
                [jnp.zeros((n, rows), F32), wide[:rows - n]], axis=0)
            toep = toep + jnp.where(lane_blk == p, delayed, 0.0)
        toep_ref[h] = toep.astype(BF16)

    dr, di = apow(jnp.full((1, 1), float(CHUNK), F32))
    sub = lax.broadcasted_iota(jnp.int32, (8, lanes), 0)
    decay_ref[...] = jnp.where(sub == 0, dr, jnp.where(sub == 1, di, 0.0))


def _operators_block_kernel(*refs):
    for k in range(refs[0].shape[0]):
        two = pl.ds(2 * k, 2)
        _operators_kernel(*[r.at[k] for r in refs[:7]], refs[7].at[two], refs[8].at[two],
                          refs[9].at[two], refs[10].at[two], refs[11].at[k])


def _prenorm_operators_kernel(*refs):
    _prenorm_kernel(*refs[:4], refs[12])
    _operators_block_kernel(*refs[4:12], *refs[13:])


def _prenorm_and_operators(x, mod0, g_pre0, perm, lam_re, lam_im, log_dt, b_re, b_im, c_re, c_im,
                           d_skip):
    bsz, seq, d = x.shape
    tb = PRENORM_BLOCK
    nblk = seq // tb
    g, p = lam_re.shape
    cpg = SSM_GROUP
    rows = CHUNK * cpg
    pairs = g // 2
    lanes = 2 * p
    row_pair = lambda v: v.reshape(pairs, 1, lanes)
    mat_pair = lambda m: m.reshape(pairs, 2, cpg, p).transpose(0, 2, 1, 3).reshape(pairs, cpg, lanes)
    op_args = (row_pair(lam_re), row_pair(lam_im),
               row_pair(jnp.broadcast_to(log_dt[:, None], (g, p))),
               mat_pair(b_re.transpose(0, 2, 1)), mat_pair(b_im.transpose(0, 2, 1)),
               mat_pair(c_re), mat_pair(c_im),
               jnp.tile(d_skip.reshape(g, 1, cpg), (1, 1, CHUNK)))
    op_shapes = [
        jax.ShapeDtypeStruct((g, rows, rows), BF16),
        jax.ShapeDtypeStruct((g, 2, rows, lanes), BF16),
        jax.ShapeDtypeStruct((g, 2, rows, lanes), BF16),
        jax.ShapeDtypeStruct((pairs, 8, lanes), F32),
    ]

    def op_specs(per, at):
        vec = pl.BlockSpec((per, 1, lanes), lambda *i: (at(*i), 0, 0))
        mat = pl.BlockSpec((per, cpg, lanes), lambda *i: (at(*i), 0, 0))
        ins = [vec, vec, vec, mat, mat, mat, mat,
               pl.BlockSpec((2 * per, 1, rows), lambda *i: (at(*i), 0, 0))]
        outs = [
            pl.BlockSpec((2 * per, rows, rows), lambda *i: (at(*i), 0, 0)),
            pl.BlockSpec((2 * per, 2, rows, lanes), lambda *i: (at(*i), 0, 0, 0)),
            pl.BlockSpec((2 * per, 2, rows, lanes), lambda *i: (at(*i), 0, 0, 0)),
            pl.BlockSpec((per, 8, lanes), lambda *i: (at(*i), 0, 0)),
        ]
        return ins, outs

    pre_args = (x, mod0.reshape(bsz, 1, 3 * d), g_pre0.reshape(1, d), perm)
    pre_specs = [
        pl.BlockSpec((None, tb, d), lambda b, j: (b, j, 0)),
        pl.BlockSpec((None, 1, 3 * d), lambda b, j: (b, 0, 0)),
        pl.BlockSpec((1, d), lambda b, j: (0, 0)),
        pl.BlockSpec(perm.shape, lambda b, j: (0, 0)),
    ]
    pre_out = pl.BlockSpec((None, CHUNK, tb // CHUNK, d), lambda b, j: (b, 0, j, 0))
    pre_shape = jax.ShapeDtypeStruct((bsz, CHUNK, seq // CHUNK, d), BF16)

    steps = bsz * nblk
    if pairs % steps == 0:
        ins, outs = op_specs(pairs // steps, lambda b, j: b * nblk + j)
        return pl.pallas_call(
            _prenorm_operators_kernel,
            grid=(bsz, nblk),
            in_specs=pre_specs + ins,
            out_specs=[pre_out] + outs,
            out_shape=[pre_shape] + op_shapes,
            compiler_params=_params(("parallel", "parallel")),
            name="prenorm_operators",
        )(*pre_args, *op_args)
    hperm = pl.pallas_call(
        _prenorm_kernel, grid=(bsz, nblk), in_specs=pre_specs, out_specs=pre_out,
        out_shape=pre_shape, compiler_params=_params(("parallel", "parallel")), name="prenorm",
    )(*pre_args)
    ins, outs = op_specs(1, lambda q: q)
    return (hperm, *pl.pallas_call(
        _operators_block_kernel, grid=(pairs,), in_specs=ins, out_specs=outs, out_shape=op_shapes,
        compiler_params=_params(("parallel",)), name="operators",
    )(*op_args))


def _ssm_kernel(h_ref, wu_ref, toep_ref, sin_ref, sout_ref, decay_ref, y_ref, xs_ref):
    n_chunks = y_ref.shape[-1] // CHUNK
    n_state = SSM_STATE
    j = pl.program_id(1)
    last = pl.num_programs(1) - 1
    fill, drain = j % 2, (j + 1) % 2

    def project(p):
        r0 = pl.multiple_of(p * n_chunks, n_chunks)
        u = _dot_nt(wu_ref[...], h_ref[pl.ds(r0, n_chunks), :]).astype(BF16)
        c0 = pl.multiple_of(p * SSM_GROUP, SSM_GROUP)
        for g in range(GROUPS_PER_STEP):
            xs_ref[fill, g, pl.ds(c0, SSM_GROUP), :] = u[g * SSM_GROUP:(g + 1) * SSM_GROUP, :]

    row = lax.broadcasted_iota(jnp.int32, (n_chunks, 2 * n_state), 0)

    def shift_rows(v, s):
        if s % 8 == 0:
            return jnp.concatenate([jnp.zeros((s, v.shape[1]), v.dtype), v[:-s]], axis=0)
        return jnp.where(row >= s, pltpu.roll(v, s, 0), 0.0)

    def gains(q):
        g0, g1 = 2 * q, 2 * q + 1
        x0, x1 = xs_ref[drain, g0], xs_ref[drain, g1]
        inc_r = _dot_tn(x0, sin_ref[g0, 0]) + _dot_tn(x1, sin_ref[g1, 0])
        inc_i = _dot_tn(x0, sin_ref[g0, 1]) + _dot_tn(x1, sin_ref[g1, 1])
        return inc_r, inc_i

    def finish(q, inc_r, inc_i):
        g0, g1 = 2 * q, 2 * q + 1
        er, ei = shift_rows(inc_r, 1), shift_rows(inc_i, 1)
        ar, ai = decay_ref[q, 0:1, :], decay_ref[q, 1:2, :]
        s = 1
        while s < n_chunks:
            if s % 8 == 0:
                dr = ar * er[:-s] - ai * ei[:-s]
                di = ar * ei[:-s] + ai * er[:-s]
                er = jnp.concatenate([er[:s], er[s:] + dr], axis=0)
                ei = jnp.concatenate([ei[:s], ei[s:] + di], axis=0)
            else:
                sr, si = shift_rows(er, s), shift_rows(ei, s)
                er, ei = er + (ar * sr - ai * si), ei + (ar * si + ai * sr)
            ar, ai = ar * ar - ai * ai, 2.0 * (ar * ai)
            s *= 2
        sr, si = er.astype(BF16), ei.astype(BF16)
        for g in (g0, g1):
            y = (_dot(toep_ref[g], xs_ref[drain, g]) + _dot_nt(sout_ref[g, 0], sr)
                 + _dot_nt(sout_ref[g, 1], si))
            act = _gelu_tanh(y).astype(BF16)
            row0 = pl.multiple_of(g * SSM_GROUP, SSM_GROUP)
            for p in range(CHUNK):
                y_ref[pl.ds(row0, SSM_GROUP), p * n_chunks:(p + 1) * n_chunks] = (
                    act[p * SSM_GROUP:(p + 1) * SSM_GROUP, :])

    n_iter = GROUPS_PER_STEP // 2 // SSM_UNROLL
    per_iter = CHUNK // n_iter

    def body(it, carry, with_scan, with_projection):
        qs = [it * SSM_UNROLL + u for u in range(SSM_UNROLL)]
        started = [gains(q) for q in qs] if with_scan else []
        if with_projection:
            for pp in range(per_iter):
                project(it * per_iter + pp)
        for q, inc in zip(qs, started):
            finish(q, *inc)
        return carry

    @pl.when(j == 0)
    def _():
        lax.fori_loop(0, n_iter, functools.partial(body, with_scan=False, with_projection=True), 0)

    @pl.when(jnp.logical_and(j > 0, j < last))
    def _():
        lax.fori_loop(0, n_iter, functools.partial(body, with_scan=True, with_projection=True), 0)

    @pl.when(j == last)
    def _():
        lax.fori_loop(0, n_iter, functools.partial(body, with_scan=True, with_projection=False), 0)


def _ssm(hperm, wu_t, toep, state_in, state_out, decay):
    bsz, seq, d = hperm.shape
    e = wu_t.shape[0]
    cb = GROUPS_PER_STEP * SSM_GROUP
    rows = CHUNK * SSM_GROUP
    n_chunks = seq // CHUNK
    gps = GROUPS_PER_STEP
    nblk = e // cb
    assert (gps // 2) % SSM_UNROLL == 0 and CHUNK % (gps // 2 // SSM_UNROLL) == 0
    proj = lambda j: jnp.minimum(j, nblk - 1)
    scan = lambda j: jnp.maximum(j - 1, 0)
    return pl.pallas_call(
        _ssm_kernel,
        grid=(bsz, nblk + 1),
        in_specs=[
            pl.BlockSpec((None, seq, d), lambda b, j: (b, 0, 0)),
            pl.BlockSpec((cb, d), lambda b, j: (proj(j), 0)),
            pl.BlockSpec((gps, rows, rows), lambda b, j: (scan(j), 0, 0)),
            pl.BlockSpec((gps, 2, rows, 2 * SSM_STATE), lambda b, j: (scan(j), 0, 0, 0)),
            pl.BlockSpec((gps, 2, rows, 2 * SSM_STATE), lambda b, j: (scan(j), 0, 0, 0)),
            pl.BlockSpec((gps // 2, 8, 2 * SSM_STATE), lambda b, j: (scan(j), 0, 0)),
        ],
        out_specs=pl.BlockSpec((None, cb, seq), lambda b, j: (b, scan(j), 0)),
        out_shape=jax.ShapeDtypeStruct((bsz, e, seq), BF16),
        scratch_shapes=[pltpu.VMEM((2, gps, rows, n_chunks), BF16)],
        compiler_params=_params(("parallel", "arbitrary")),
        name="ssm",
    )(hperm, wu_t, toep, state_in, state_out, decay)


def _glu_kernel(y_ref, h_ref, wg_ref, bg_ref, wz_ref, o_ref):
    e = y_ref.shape[0]
    rb = 256
    ya = y_ref[...]
    hb = h_ref[...]
    for r in range(e // rb):
        rows = slice(r * rb, (r + 1) * rb)
        gl = _dot(wg_ref[rows, :], ya) + bg_ref[rows, :]
        z = _dot_nt(wz_ref[rows, :], hb)
        yr = y_ref[rows, :].astype(F32)
        gated = yr * _sigmoid(gl) * _silu(z)
        o_ref[:, rows] = gated.T.astype(BF16)


def _glu(y_t, hperm, wglu_t, b_glu, wz_t):
    bsz, e, seq = y_t.shape
    d = hperm.shape[-1]
    tn = GLU_BLOCK
    return pl.pallas_call(
        _glu_kernel,
        grid=(bsz, seq // tn),
        in_specs=[
            pl.BlockSpec((None, e, tn), lambda b, j: (b, 0, j)),
            pl.BlockSpec((None, tn, d), lambda b, j: (b, j, 0)),
            pl.BlockSpec((e, e), lambda b, j: (0, 0)),
            pl.BlockSpec((e, 1), lambda b, j: (0, 0)),
            pl.BlockSpec((e, d), lambda b, j: (0, 0)),
        ],
        out_specs=pl.BlockSpec((None, tn, e), lambda b, j: (b, j, 0)),
        out_shape=jax.ShapeDtypeStruct((bsz, seq, e), BF16),
        compiler_params=_params(("parallel", "parallel")),
        name="glu",
    )(y_t, hperm, wglu_t, b_glu.reshape(e, 1), wz_t)


def _mid_kernel(gp_ref, x_ref, mod0_ref, mod1_ref, p_ref, wo_ref, gpost_ref, gkv_ref, gpre_ref,
                wk_ref, wv_ref, wq_ref, wz_ref, gsub_ref, h_ref, k_ref, vt_ref, qt_ref, gz_ref):
    d = x_ref.shape[-1]
    tb = p_ref.shape[0]
    subs = range(x_ref.shape[0] // tb)
    cps = tb // CHUNK
    gate0 = mod0_ref[:, 2 * d:3 * d]
    shift1 = mod1_ref[:, 0:d]
    scale1 = mod1_ref[:, d:2 * d]
    ones_row = (lax.broadcasted_iota(jnp.int32, (V_ROWS - V_DIM, tb), 0) == 0).astype(BF16)

    ys = []
    for s in subs:
        gp = gp_ref[:, s * cps:(s + 1) * cps, :].reshape(tb, -1)
        ys.append(_dot(_dot(p_ref[...], gp).astype(BF16), wo_ref[...]))
    ins = []
    for s, y in zip(subs, ys):
        rows = slice(s * tb, (s + 1) * tb)
        h = x_ref[rows, :] + gate0 * _rms_rows(y, gpost_ref[...])
        h_ref[rows, :] = h
        kv_in = _rms_rows(h, gkv_ref[...]).astype(BF16)
        h_in = (_rms_rows(h, gpre_ref[...]) * (1.0 + scale1) + shift1).astype(BF16)
        ins.append((kv_in, h_in))
    for s, (kv_in, h_in) in zip(subs, ins):
        k_ref[s * tb:(s + 1) * tb, :] = _dot(kv_in, wk_ref[...]).astype(BF16)
        vt = _dot_nt(wv_ref[...], kv_in).astype(BF16)
        for hd in range(vt.shape[0] // V_DIM):
            vt_ref[s, hd * V_ROWS:hd * V_ROWS + V_DIM, :] = vt[hd * V_DIM:(hd + 1) * V_DIM, :]
            vt_ref[s, hd * V_ROWS + V_DIM:(hd + 1) * V_ROWS, :] = ones_row
        qt_ref[s] = (_dot_nt(wq_ref[...], h_in) * Q_SCALE).astype(BF16)
        gz_ref[s] = (_silu(_dot_nt(wz_ref[...], h_in)) * gsub_ref[...]).astype(BF16)


def _mid(gated_perm, x, mod0, mod1, perm, wo, g_post0, g_kv, g_pre1, wk, wv_t, wq_t, wz_t, gsub):
    bsz, seq, d = x.shape
    tb = TOKEN_BLOCK
    nb = seq // tb
    sub = MID_SUB
    e = gated_perm.shape[-1]
    qk = wk.shape[1]
    av = wv_t.shape[0]
    row = lambda b, j: (b, j, 0)
    const2 = lambda b, j: (0, 0)
    t_spec = lambda n: pl.BlockSpec((None, sub, n, tb), lambda b, j: (b, j, 0, 0))
    return pl.pallas_call(
        _mid_kernel,
        grid=(bsz, nb // sub),
        in_specs=[
            pl.BlockSpec((None, CHUNK, sub * tb // CHUNK, e), lambda b, j: (b, 0, j, 0)),
            pl.BlockSpec((None, sub * tb, d), row),
            pl.BlockSpec((None, 1, 3 * d), lambda b, j: (b, 0, 0)),
            pl.BlockSpec((None, 1, 3 * d), lambda b, j: (b, 0, 0)),
            pl.BlockSpec((tb, tb), const2),
            pl.BlockSpec((e, d), const2),
            pl.BlockSpec((1, d), const2),
            pl.BlockSpec((1, d), const2),
            pl.BlockSpec((1, d), const2),
            pl.BlockSpec((d, qk), const2),
            pl.BlockSpec((av, d), const2),
            pl.BlockSpec((qk, d), const2),
            pl.BlockSpec((av, d), const2),
            pl.BlockSpec((av, 1), const2),
        ],
        out_specs=[
            pl.BlockSpec((None, sub * tb, d), row),
            pl.BlockSpec((None, sub * tb, qk), row),
            t_spec(av // V_DIM * V_ROWS),
            t_spec(qk),
            t_spec(av),
        ],
        out_shape=[
            jax.ShapeDtypeStruct((bsz, seq, d), F32),
            jax.ShapeDtypeStruct((bsz, seq, qk), BF16),
            jax.ShapeDtypeStruct((bsz, nb, av // V_DIM * V_ROWS, tb), BF16),
            jax.ShapeDtypeStruct((bsz, nb, qk, tb), BF16),
            jax.ShapeDtypeStruct((bsz, nb, av, tb), BF16),
        ],
        compiler_params=_params(("parallel", "parallel")),
        name="mid",
    )(gated_perm.reshape(bsz, CHUNK, seq // CHUNK, e), x, mod0.reshape(bsz, 1, 3 * d),
      mod1.reshape(bsz, 1, 3 * d), perm, wo, g_post0.reshape(1, d), g_kv.reshape(1, d),
      g_pre1.reshape(1, d), wk, wv_t, wq_t, wz_t, gsub.reshape(av, 1))


def _attention_kernel(lam_ref, qt_ref, qn_ref, k_ref, vt_ref, gz_ref, bias_ref, h_ref, mod1_ref, wo_ref,
                      gpost_ref, o_ref, qp_ref, m_ref, acc_ref, s_ref, og_ref):
    tq = qt_ref.shape[-1]
    hw = 2 * HEAD_DIM
    heads = qt_ref.shape[0] // hw
    i = pl.program_id(1)

    zero = jnp.zeros((HEAD_DIM, tq), qt_ref.dtype)

    def pad_queries(q_ref, n_heads):
        for h in range(n_heads):
            qt = q_ref[h * hw:(h + 1) * hw, :]
            qp_ref[2 * h] = jnp.concatenate([qt[:HEAD_DIM], zero], axis=0)
            qp_ref[2 * h + 1] = jnp.concatenate([zero, qt[HEAD_DIM:]], axis=0)

    pad_queries(qt_ref, heads)
    n_strips = 2 * heads

    def scores(j, n):
        h = n // 2
        row0 = pl.multiple_of(j * tq, tq)
        return _dot(k_ref[pl.ds(row0, tq), h * hw:(h + 1) * hw], qp_ref[n])

    def step(j, last=False):
        for n in range(n_strips):
            s = s_ref[n % ATT_AHEAD]
            if n + ATT_AHEAD < n_strips:
                s_ref[n % ATT_AHEAD] = scores(j, n + ATT_AHEAD)
            elif not last:
                s_ref[n % ATT_AHEAD] = scores(j + 1, n + ATT_AHEAD - n_strips)
            m_old = m_ref[n]
            if last:
                hq = tq // 2
                s00 = s[:hq, :hq] + bias_ref[...]
                s01 = s[:hq, hq:]
                s11 = s[hq:, hq:] + bias_ref[...]
                m_blk = jnp.concatenate(
                    [jnp.max(s00, axis=0, keepdims=True),
                     jnp.maximum(jnp.max(s01, axis=0, keepdims=True),
                                 jnp.max(s11, axis=0, keepdims=True))], axis=1)
                m_new = jnp.maximum(m_old, m_blk)
                e = lambda v, m: jnp.exp2((v - m).astype(BF16))
                p = jnp.concatenate(
                    [jnp.concatenate([e(s00, m_new[:, :hq]), e(s01, m_new[:, hq:])], axis=1),
                     jnp.concatenate([jnp.zeros((hq, hq), BF16), e(s11, m_new[:, hq:])], axis=1)],
                    axis=0)
            else:
                m_new = jnp.maximum(m_old, jnp.max(s, axis=0, keepdims=True))
                p = jnp.exp2((s - m_new).astype(BF16))
            alpha = jnp.exp2(m_old - m_new)
            m_ref[n] = m_new
            h = n // 2
            vt = vt_ref[j, h * V_ROWS:(h + 1) * V_ROWS, :]
            acc_ref[n] = alpha * acc_ref[n] + _dot(vt, p)

    @pl.when(i == 0)
    def _():
        for n in range(ATT_AHEAD):
            s_ref[n] = scores(0, n)

    m_ref[...] = jnp.full(m_ref.shape, -jnp.inf, F32)
    acc_ref[...] = jnp.zeros(acc_ref.shape, F32)

    @pl.when(i == 0)
    def _():
        step(0, last=True)

    @pl.when(i > 0)
    def _():
        first = (i - 1) % 2

        @pl.when(first == 1)
        def _():
            step(0)

        def two_steps(t, carry):
            step(first + 2 * t)
            step(first + 2 * t + 1)
            return carry

        lax.fori_loop(0, (i - 1) // 2, two_steps, 0)
        step(i - 1)
        step(i, last=True)

    pad_queries(qn_ref, ATT_AHEAD // 2)
    for n in range(ATT_AHEAD):
        s_ref[n] = scores(0, n)

    for h in range(heads):
        r0 = 1.0 / acc_ref[2 * h, V_DIM:V_DIM + 1, :]
        r1 = lam_ref[0] / acc_ref[2 * h + 1, V_DIM:V_DIM + 1, :]
        o = acc_ref[2 * h, :V_DIM, :] * r0 - acc_ref[2 * h + 1, :V_DIM, :] * r1
        rows = slice(h * V_DIM, (h + 1) * V_DIM)
        inv_rms = lax.rsqrt(jnp.mean(o * o, axis=0, keepdims=True) + EPS)
        og_ref[rows, :] = (o * inv_rms * gz_ref[rows, :].astype(F32)).astype(BF16)
        if h % 2 == 1:
            pr = slice((h - 1) * V_DIM, (h + 1) * V_DIM)
            part = _dot_tn(og_ref[pr, :], wo_ref[pr, :])
            y = part if h == 1 else y + part
    d = h_ref.shape[-1]
    gate1 = mod1_ref[:, 2 * d:3 * d]
    o_ref[...] = h_ref[...] + gate1 * _rms_rows(y, gpost_ref[...])


def _attention(lam, q_t, k, v_t, gz_t, h, mod1, wo, g_post1):
    bsz, nb, width, tb = q_t.shape
    seq = nb * tb
    d = h.shape[-1]
    heads = width // (2 * HEAD_DIM)
    assert heads % 2 == 0 and (2 * heads) % ATT_AHEAD == 0
    pos = jnp.arange(tb // 2)
    bias = jnp.where(pos[:, None] <= pos[None, :], 0.0, -jnp.inf).astype(F32)
    blk = pl.BlockSpec((None, None, width, tb), lambda b, i: (b, i, 0, 0))
    once = pl.Buffered(1)
    return pl.pallas_call(
        _attention_kernel,
        grid=(bsz, nb),
        in_specs=[
            pl.BlockSpec(memory_space=pltpu.SMEM),
            blk,
            pl.BlockSpec((None, None, width, tb), lambda b, i: (b, jnp.minimum(i + 1, nb - 1), 0, 0)),
            pl.BlockSpec((None, seq, width), lambda b, i: (b, 0, 0)),
            pl.BlockSpec((None, nb, heads * V_ROWS, tb), lambda b, i: (b, 0, 0, 0)),
            blk,
            pl.BlockSpec((tb // 2, tb // 2), lambda b, i: (0, 0), pipeline_mode=once),
            pl.BlockSpec((None, tb, d), lambda b, i: (b, i, 0)),
            pl.BlockSpec((None, 1, 3 * d), lambda b, i: (b, 0, 0)),
            pl.BlockSpec((width, d), lambda b, i: (0, 0), pipeline_mode=once),
            pl.BlockSpec((1, d), lambda b, i: (0, 0), pipeline_mode=once),
        ],
        out_specs=pl.BlockSpec((None, tb, d), lambda b, i: (b, i, 0)),
        out_shape=jax.ShapeDtypeStruct((bsz, seq, d), F32),
        scratch_shapes=[
            pltpu.VMEM((2 * heads, 2 * HEAD_DIM, tb), BF16),
            pltpu.VMEM((2 * heads, 1, tb), F32),
            pltpu.VMEM((2 * heads, V_ROWS, tb), F32),
            pltpu.VMEM((ATT_AHEAD, tb, tb), F32),
            pltpu.VMEM((width, tb), BF16),
        ],
        compiler_params=pltpu.CompilerParams(dimension_semantics=("parallel", "arbitrary"),
                                             vmem_limit_bytes=ATT_VMEM_LIMIT),
        name="attention",
    )(lam, q_t, q_t, k, v_t, gz_t, bias, h, mod1.reshape(bsz, 1, 3 * d), wo, g_post1.reshape(1, d))


def kernel(x, c, ada_w, ada_b, g_pre, g_post, a_w_in, a_lam_re, a_lam_im, a_log_dt, a_b_re, a_b_im,
           a_c_re, a_c_im, a_d, a_w_glu, a_b_glu, a_w_out, g_kv, w_k, w_v, b_w_in, b_lq1, b_lk1,
           b_lq2, b_lk2, b_g_sub, b_w_out):
    bsz, seq, d = x.shape
    e = a_w_glu.shape[1]
    qk = w_k.shape[1]
    assert seq % (CHUNK * 128) == 0 and seq % GLU_BLOCK == 0 and d % 128 == 0
    assert e % (GROUPS_PER_STEP * SSM_GROUP) == 0

    mod = _modulation(c, ada_w, ada_b)
    perm = _chunk_permutation(TOKEN_BLOCK)

    hperm, toep, state_in, state_out, decay = _prenorm_and_operators(
        x, mod[0], g_pre[0], perm,
        a_lam_re[0], a_lam_im[0], a_log_dt[0], a_b_re[0], a_b_im[0], a_c_re[0], a_c_im[0], a_d[0])
    hperm = hperm.reshape(bsz, seq, d)
    w_in_t = a_w_in[0].T.astype(BF16)
    y_t = _ssm(hperm, w_in_t[:e], toep, state_in, state_out, decay)
    gated = _glu(y_t, hperm, a_w_glu[0].T.astype(BF16), a_b_glu[0], w_in_t[e:])

    layer = DEPTH // 2
    lambda_init = 0.8 - 0.6 * math.exp(-0.3 * layer)
    w_b_t = b_w_in[0].T.astype(BF16)
    gsub = jnp.tile(b_g_sub[0] * (1.0 - lambda_init), w_v.shape[1] // V_DIM)
    h, k, v_t, q_t, gz_t = _mid(
        gated, x, mod[0], mod[1], perm, a_w_out[0].astype(BF16), g_post[0], g_kv, g_pre[1],
        w_k.astype(BF16), w_v.T.astype(BF16), w_b_t[:qk], w_b_t[qk:], gsub)

    lam = (jnp.exp(jnp.sum(b_lq1[0] * b_lk1[0])) - jnp.exp(jnp.sum(b_lq2[0] * b_lk2[0]))
           + lambda_init).reshape(1).astype(F32)
    return _attention(lam, q_t, k, v_t, gz_t, h, mod[1], b_w_out[0].astype(BF16), g_post[1])
```

```python
import functools
import math

import jax
import jax.numpy as jnp
from jax import lax
from jax.experimental import pallas as pl
from jax.experimental.pallas import tpu as pltpu

F32 = jnp.float32
BF16 = jnp.bfloat16

EPS = 1e-6
DEPTH = 2
SSM_GROUP = 16
SSM_STATE = 64
CHUNK = 16
GROUPS_PER_STEP = 16
SSM_UNROLL = 8
HEAD_DIM = 64
V_DIM = 2 * HEAD_DIM
V_ROWS = V_DIM + 16
TOKEN_BLOCK = 256
PRENORM_BLOCK = 1024
GLU_BLOCK = 1024
MID_SUB = 4
ATT_AHEAD = 8
Q_SCALE = HEAD_DIM ** -0.5 * math.log2(math.e)
VMEM_LIMIT = 48 * 1024 * 1024
ATT_VMEM_LIMIT = 56 * 1024 * 1024


def _params(semantics):
    return pltpu.CompilerParams(dimension_semantics=semantics, vmem_limit_bytes=VMEM_LIMIT)


def _sigmoid(v):
    return 1.0 / (1.0 + jnp.exp(-v))


def _silu(v):
    return v * _sigmoid(v)


def _gelu_tanh(v):
    k = -2.0 * math.sqrt(2.0 / math.pi) * math.log2(math.e)
    return v / (1.0 + jnp.exp2(v * ((k * 0.044715) * (v * v) + k)))


def _rms_rows(v, g):
    return v * lax.rsqrt(jnp.mean(v * v, axis=-1, keepdims=True) + EPS) * g


def _dot(a, b):
    return jnp.dot(a, b, preferred_element_type=F32)


def _dot_nt(a, b):
    return lax.dot_general(a, b, (((1,), (1,)), ((), ())), preferred_element_type=F32)


def _dot_tn(a, b):
    return lax.dot_general(a, b, (((0,), (0,)), ((), ())), preferred_element_type=F32)


def _modulation_kernel(c_ref, w_ref, b_ref, o_ref):
    def split(v):
        hi = v.astype(BF16)
        return hi, (v - hi.astype(F32)).astype(BF16)

    (sh, sl), (wh, wl) = split(_silu(c_ref[...])), split(w_ref[...])
    o_ref[...] = _dot(sh, wh) + _dot(sh, wl) + _dot(sl, wh) + b_ref[...]


def _modulation(c, ada_w, ada_b):
    bsz, d = c.shape
    depth, _, n = ada_w.shape
    tn = 1024
    return pl.pallas_call(
        _modulation_kernel,
        grid=(depth, n // tn),
        in_specs=[
            pl.BlockSpec((bsz, d), lambda l, j: (0, 0)),
            pl.BlockSpec((None, d, tn), lambda l, j: (l, 0, j)),
            pl.BlockSpec((None, 1, tn), lambda l, j: (l, 0, j)),
        ],
        out_specs=pl.BlockSpec((None, bsz, tn), lambda l, j: (l, 0, j)),
        out_shape=jax.ShapeDtypeStruct((depth, bsz, n), F32),
        compiler_params=_params(("parallel", "parallel")),
        name="modulation",
    )(c, ada_w, ada_b.reshape(depth, 1, n))


def _chunk_permutation(n):
    r = jnp.arange(n)
    src = (r % (n // CHUNK)) * CHUNK + r // (n // CHUNK)
    return (src[:, None] == r[None, :]).astype(BF16)


def _prenorm_kernel(x_ref, mod_ref, g_ref, p_ref, o_ref):
    d = x_ref.shape[-1]
    sub = p_ref.shape[0]
    shift = mod_ref[:, 0:d]
    scale = mod_ref[:, d:2 * d]
    for r in range(x_ref.shape[0] // sub):
        x = x_ref[r * sub:(r + 1) * sub, :]
        h = _rms_rows(x, g_ref[...]) * (1.0 + scale) + shift
        hp = _dot(p_ref[...], h.astype(BF16)).astype(BF16)
        o_ref[:, r * (sub // CHUNK):(r + 1) * (sub // CHUNK), :] = hp.reshape(CHUNK, sub // CHUNK, d)


def _operators_kernel(lam_re_ref, lam_im_ref, log_dt_ref, bt_re_ref, bt_im_ref, c_re_ref, c_im_ref,
                      d_ref, toep_ref, sin_ref, sout_ref, decay_ref):
    rows = CHUNK * SSM_GROUP
    lanes = 2 * SSM_STATE
    lam_re, lam_im, dt = lam_re_ref[...], lam_im_ref[...], jnp.exp(log_dt_ref[...])
    ar, ai = lam_re * dt, lam_im * dt

    def apow(k):
        mag = jnp.exp(k * ar)
        return mag * jnp.cos(k * ai), mag * jnp.sin(k * ai)

    pos = lax.broadcasted_iota(jnp.int32, (CHUNK, 1), 0).astype(F32)
    a1r, a1i = apow(jnp.ones((1, 1), F32))
    den = lam_re * lam_re + lam_im * lam_im
    fr = ((a1r - 1.0) * lam_re + a1i * lam_im) / den
    fi = (a1i * lam_re - (a1r - 1.0) * lam_im) / den
    bbr = fr * bt_re_ref[...] - fi * bt_im_ref[...]
    bbi = fr * bt_im_ref[...] + fi * bt_re_ref[...]

    r_idx = lax.broadcasted_iota(jnp.int32, (rows, CHUNK), 0)
    k_idx = lax.broadcasted_iota(jnp.int32, (rows, CHUNK), 1)
    rep = (r_idx // SSM_GROUP == k_idx).astype(BF16)
    tile = (r_idx % SSM_GROUP == k_idx).astype(BF16)
    lane_tile = (lax.broadcasted_iota(jnp.int32, (SSM_GROUP, rows), 1) % SSM_GROUP
                 == lax.broadcasted_iota(jnp.int32, (SSM_GROUP, rows), 0)).astype(BF16)

    def split(v):
        hi = v.astype(BF16)
        return hi, (v - hi.astype(F32)).astype(BF16)

    def expand(sel, v):
        hi, lo = split(v)
        return _dot(sel, hi) + _dot(sel, lo)

    def dot3(x, y):
        (xh, xl), (yh, yl) = split(x), split(y)
        return _dot_nt(xh, yh) + _dot_nt(xh, yl) + _dot_nt(xl, yh)

    def times(xr, xi, yr, yi):
        return xr * yr - xi * yi, xr * yi + xi * yr

    left = lax.broadcasted_iota(jnp.int32, (rows, lanes), 1) < SSM_STATE
    halves = lambda v: (jnp.where(left, v, 0.0), jnp.where(left, 0.0, v))

    qr, qi = times(*apow(CHUNK - 1.0 - pos), fr, fi)
    sr, si = times(expand(rep, qr), expand(rep, qi),
                   expand(tile, bt_re_ref[...]), expand(tile, bt_im_ref[...]))
    cr, ci = expand(tile, c_re_ref[...]), expand(tile, c_im_ref[...])
    wr, wi = apow(pos + 1.0)
    our, oui = times(cr, ci, expand(rep, wr), expand(rep, wi))
    for h, (s_r, s_i, o_r, o_i) in enumerate(zip(halves(sr), halves(si), halves(our), halves(oui))):
        sin_ref[h, 0] = s_r.astype(BF16)
        sin_ref[h, 1] = s_i.astype(BF16)
        sout_ref[h, 0] = o_r.astype(BF16)
        sout_ref[h, 1] = (-o_i).astype(BF16)

    pr, pi = apow(pos)
    lr, li = times(cr, ci, expand(rep, pr), expand(rep, pi))
    lane_blk = lax.broadcasted_iota(jnp.int32, (rows, rows), 1) // SSM_GROUP
    diag = (lax.broadcasted_iota(jnp.int32, (rows, rows), 0)
            == lax.broadcasted_iota(jnp.int32, (rows, rows), 1))
    for h, (l_r, l_i) in enumerate(zip(halves(lr), halves(li))):
        kern = dot3(l_r, bbr) - dot3(l_i, bbi)
        k_hi, k_lo = split(kern)
        wide = _dot(k_hi, lane_tile) + _dot(k_lo, lane_tile)
        toep = jnp.where(diag, d_ref[h], 0.0)
        for p in range(CHUNK):
            n = p * SSM_GROUP
            delayed = wide if p == 0 else jnp.concatenate(
                [jnp.zeros((n, rows), F32), wide[:rows - n]], axis=0)
            toep = toep + jnp.where(lane_blk == p, delayed, 0.0)
        toep_ref[h] = toep.astype(BF16)

    dr, di = apow(jnp.full((1, 1), float(CHUNK), F32))
    sub = lax.broadcasted_iota(jnp.int32, (8, lanes), 0)
    decay_ref[...] = jnp.where(sub == 0, dr, jnp.where(sub == 1, di, 0.0))


def _operators_block_kernel(*refs):
    for k in range(refs[0].shape[0]):
        two = pl.ds(2 * k, 2)
        _operators_kernel(*[r.at[k] for r in refs[:7]], refs[7].at[two], refs[8].at[two],
                          refs[9].at[two], refs[10].at[two], refs[11].at[k])


def _prenorm_operators_kernel(*refs):
    _prenorm_kernel(*refs[:4], refs[12])
    _operators_block_kernel(*refs[4:12], *refs[13:])


def _prenorm_and_operators(x, mod0, g_pre0, perm, lam_re, lam_im, log_dt, b_re, b_im, c_re, c_im,
                           d_skip):
    bsz, seq, d = x.shape
    tb = PRENORM_BLOCK
    nblk = seq // tb
    g, p = lam_re.shape
    cpg = SSM_GROUP
    rows = CHUNK * cpg
    pairs = g // 2
    lanes = 2 * p
    row_pair = lambda v: v.reshape(pairs, 1, lanes)
    mat_pair = lambda m: m.reshape(pairs, 2, cpg, p).transpose(0, 2, 1, 3).reshape(pairs, cpg, lanes)
    op_args = (row_pair(lam_re), row_pair(lam_im),
               row_pair(jnp.broadcast_to(log_dt[:, None], (g, p))),
               mat_pair(b_re.transpose(0, 2, 1)), mat_pair(b_im.transpose(0, 2, 1)),
               mat_pair(c_re), mat_pair(c_im),
               jnp.tile(d_skip.reshape(g, 1, cpg), (1, 1, CHUNK)))
    op_shapes = [
        jax.ShapeDtypeStruct((g, rows, rows), BF16),
        jax.ShapeDtypeStruct((g, 2, rows, lanes), BF16),
        jax.ShapeDtypeStruct((g, 2, rows, lanes), BF16),
        jax.ShapeDtypeStruct((pairs, 8, lanes), F32),
    ]

    def op_specs(per, at):
        vec = pl.BlockSpec((per, 1, lanes), lambda *i: (at(*i), 0, 0))
        mat = pl.BlockSpec((per, cpg, lanes), lambda *i: (at(*i), 0, 0))
        ins = [vec, vec, vec, mat, mat, mat, mat,
               pl.BlockSpec((2 * per, 1, rows), lambda *i: (at(*i), 0, 0))]
        outs = [
            pl.BlockSpec((2 * per, rows, rows), lambda *i: (at(*i), 0, 0)),
            pl.BlockSpec((2 * per, 2, rows, lanes), lambda *i: (at(*i), 0, 0, 0)),
            pl.BlockSpec((2 * per, 2, rows, lanes), lambda *i: (at(*i), 0, 0, 0)),
            pl.BlockSpec((per, 8, lanes), lambda *i: (at(*i), 0, 0)),
        ]
        return ins, outs

    pre_args = (x, mod0.reshape(bsz, 1, 3 * d), g_pre0.reshape(1, d), perm)
    pre_specs = [
        pl.BlockSpec((None, tb, d), lambda b, j: (b, j, 0)),
        pl.BlockSpec((None, 1, 3 * d), lambda b, j: (b, 0, 0)),
        pl.BlockSpec((1, d), lambda b, j: (0, 0)),
        pl.BlockSpec(perm.shape, lambda b, j: (0, 0)),
    ]
    pre_out = pl.BlockSpec((None, CHUNK, tb // CHUNK, d), lambda b, j: (b, 0, j, 0))
    pre_shape = jax.ShapeDtypeStruct((bsz, CHUNK, seq // CHUNK, d), BF16)

    steps = bsz * nblk
    if pairs % steps == 0:
        ins, outs = op_specs(pairs // steps, lambda b, j: b * nblk + j)
        return pl.pallas_call(
            _prenorm_operators_kernel,
            grid=(bsz, nblk),
            in_specs=pre_specs + ins,
            out_specs=[pre_out] + outs,
            out_shape=[pre_shape] + op_shapes,
            compiler_params=_params(("parallel", "parallel")),
            name="prenorm_operators",
        )(*pre_args, *op_args)
    hperm = pl.pallas_call(
        _prenorm_kernel, grid=(bsz, nblk), in_specs=pre_specs, out_specs=pre_out,
        out_shape=pre_shape, compiler_params=_params(("parallel", "parallel")), name="prenorm",
    )(*pre_args)
    ins, outs = op_specs(1, lambda q: q)
    return (hperm, *pl.pallas_call(
        _operators_block_kernel, grid=(pairs,), in_specs=ins, out_specs=outs, out_shape=op_shapes,
        compiler_params=_params(("parallel",)), name="operators",
    )(*op_args))


def _ssm_kernel(h_ref, wu_ref, toep_ref, sin_ref, sout_ref, decay_ref, y_ref, xs_ref):
    n_chunks = y_ref.shape[-1] // CHUNK
    n_state = SSM_STATE
    j = pl.program_id(1)
    last = pl.num_programs(1) - 1
    fill, drain = j % 2, (j + 1) % 2

    def project(p):
        r0 = pl.multiple_of(p * n_chunks, n_chunks)
        u = _dot_nt(wu_ref[...], h_ref[pl.ds(r0, n_chunks), :]).astype(BF16)
        c0 = pl.multiple_of(p * SSM_GROUP, SSM_GROUP)
        for g in range(GROUPS_PER_STEP):
            xs_ref[fill, g, pl.ds(c0, SSM_GROUP), :] = u[g * SSM_GROUP:(g + 1) * SSM_GROUP, :]

    row = lax.broadcasted_iota(jnp.int32, (n_chunks, 2 * n_state), 0)

    def shift_rows(v, s):
        if s % 8 == 0:
            return jnp.concatenate([jnp.zeros((s, v.shape[1]), v.dtype), v[:-s]], axis=0)
        return jnp.where(row >= s, pltpu.roll(v, s, 0), 0.0)

    def gains(q):
        g0, g1 = 2 * q, 2 * q + 1
        x0, x1 = xs_ref[drain, g0], xs_ref[drain, g1]
        inc_r = _dot_tn(x0, sin_ref[g0, 0]) + _dot_tn(x1, sin_ref[g1, 0])
        inc_i = _dot_tn(x0, sin_ref[g0, 1]) + _dot_tn(x1, sin_ref[g1, 1])
        return inc_r, inc_i

    def finish(q, inc_r, inc_i):
        g0, g1 = 2 * q, 2 * q + 1
        er, ei = shift_rows(inc_r, 1), shift_rows(inc_i, 1)
        ar, ai = decay_ref[q, 0:1, :], decay_ref[q, 1:2, :]
        s = 1
        while s < n_chunks:
            if s % 8 == 0:
                dr = ar * er[:-s] - ai * ei[:-s]
                di = ar * ei[:-s] + ai * er[:-s]
                er = jnp.concatenate([er[:s], er[s:] + dr], axis=0)
                ei = jnp.concatenate([ei[:s], ei[s:] + di], axis=0)
            else:
                sr, si = shift_rows(er, s), shift_rows(ei, s)
                er, ei = er + (ar * sr - ai * si), ei + (ar * si + ai * sr)
            ar, ai = ar * ar - ai * ai, 2.0 * (ar * ai)
            s *= 2
        sr, si = er.astype(BF16), ei.astype(BF16)
        for g in (g0, g1):
            y = (_dot(toep_ref[g], xs_ref[drain, g]) + _dot_nt(sout_ref[g, 0], sr)
                 + _dot_nt(sout_ref[g, 1], si))
            act = _gelu_tanh(y).astype(BF16)
            row0 = pl.multiple_of(g * SSM_GROUP, SSM_GROUP)
            for p in range(CHUNK):
                y_ref[pl.ds(row0, SSM_GROUP), p * n_chunks:(p + 1) * n_chunks] = (
                    act[p * SSM_GROUP:(p + 1) * SSM_GROUP, :])

    n_iter = GROUPS_PER_STEP // 2 // SSM_UNROLL
    per_iter = CHUNK // n_iter

    def body(it, carry, with_scan, with_projection):
        qs = [it * SSM_UNROLL + u for u in range(SSM_UNROLL)]
        started = [gains(q) for q in qs] if with_scan else []
        if with_projection:
            for pp in range(per_iter):
                project(it * per_iter + pp)
        for q, inc in zip(qs, started):
            finish(q, *inc)
        return carry

    @pl.when(j == 0)
    def _():
        lax.fori_loop(0, n_iter, functools.partial(body, with_scan=False, with_projection=True), 0)

    @pl.when(jnp.logical_and(j > 0, j < last))
    def _():
        lax.fori_loop(0, n_iter, functools.partial(body, with_scan=True, with_projection=True), 0)

    @pl.when(j == last)
    def _():
        lax.fori_loop(0, n_iter, functools.partial(body, with_scan=True, with_projection=False), 0)


def _ssm(hperm, wu_t, toep, state_in, state_out, decay):
    bsz, seq, d = hperm.shape
    e = wu_t.shape[0]
    cb = GROUPS_PER_STEP * SSM_GROUP
    rows = CHUNK * SSM_GROUP
    n_chunks = seq // CHUNK
    gps = GROUPS_PER_STEP
    nblk = e // cb
    assert (gps // 2) % SSM_UNROLL == 0 and CHUNK % (gps // 2 // SSM_UNROLL) == 0
    proj = lambda j: jnp.minimum(j, nblk - 1)
    scan = lambda j: jnp.maximum(j - 1, 0)
    return pl.pallas_call(
        _ssm_kernel,
        grid=(bsz, nblk + 1),
        in_specs=[
            pl.BlockSpec((None, seq, d), lambda b, j: (b, 0, 0)),
            pl.BlockSpec((cb, d), lambda b, j: (proj(j), 0)),
            pl.BlockSpec((gps, rows, rows), lambda b, j: (scan(j), 0, 0)),
            pl.BlockSpec((gps, 2, rows, 2 * SSM_STATE), lambda b, j: (scan(j), 0, 0, 0)),
            pl.BlockSpec((gps, 2, rows, 2 * SSM_STATE), lambda b, j: (scan(j), 0, 0, 0)),
            pl.BlockSpec((gps // 2, 8, 2 * SSM_STATE), lambda b, j: (scan(j), 0, 0)),
        ],
        out_specs=pl.BlockSpec((None, cb, seq), lambda b, j: (b, scan(j), 0)),
        out_shape=jax.ShapeDtypeStruct((bsz, e, seq), BF16),
        scratch_shapes=[pltpu.VMEM((2, gps, rows, n_chunks), BF16)],
        compiler_params=_params(("parallel", "arbitrary")),
        name="ssm",
    )(hperm, wu_t, toep, state_in, state_out, decay)


def _glu_kernel(y_ref, h_ref, wg_ref, bg_ref, wz_ref, o_ref):
    e = y_ref.shape[0]
    rb = 256
    ya = y_ref[...]
    hb = h_ref[...]
    for r in range(e // rb):
        rows = slice(r * rb, (r + 1) * rb)
        gl = _dot(wg_ref[rows, :], ya) + bg_ref[rows, :]
        z = _dot_nt(wz_ref[rows, :], hb)
        yr = y_ref[rows, :].astype(F32)
        gated = yr * _sigmoid(gl) * _silu(z)
        o_ref[:, rows] = gated.T.astype(BF16)


def _glu(y_t, hperm, wglu_t, b_glu, wz_t):
    bsz, e, seq = y_t.shape
    d = hperm.shape[-1]
    tn = GLU_BLOCK
    return pl.pallas_call(
        _glu_kernel,
        grid=(bsz, seq // tn),
        in_specs=[
            pl.BlockSpec((None, e, tn), lambda b, j: (b, 0, j)),
            pl.BlockSpec((None, tn, d), lambda b, j: (b, j, 0)),
            pl.BlockSpec((e, e), lambda b, j: (0, 0)),
            pl.BlockSpec((e, 1), lambda b, j: (0, 0)),
            pl.BlockSpec((e, d), lambda b, j: (0, 0)),
        ],
        out_specs=pl.BlockSpec((None, tn, e), lambda b, j: (b, j, 0)),
        out_shape=jax.ShapeDtypeStruct((bsz, seq, e), BF16),
        compiler_params=_params(("parallel", "parallel")),
        name="glu",
    )(y_t, hperm, wglu_t, b_glu.reshape(e, 1), wz_t)


def _mid_kernel(gp_ref, x_ref, mod0_ref, mod1_ref, p_ref, wo_ref, gpost_ref, gkv_ref, gpre_ref,
                wk_ref, wv_ref, wq_ref, wz_ref, gsub_ref, h_ref, k_ref, vt_ref, qt_ref, gz_ref):
    d = x_ref.shape[-1]
    tb = p_ref.shape[0]
    subs = range(x_ref.shape[0] // tb)
    cps = tb // CHUNK
    gate0 = mod0_ref[:, 2 * d:3 * d]
    shift1 = mod1_ref[:, 0:d]
    scale1 = mod1_ref[:, d:2 * d]
    ones_row = (lax.broadcasted_iota(jnp.int32, (V_ROWS - V_DIM, tb), 0) == 0).astype(BF16)

    ys = []
    for s in subs:
        gp = gp_ref[:, s * cps:(s + 1) * cps, :].reshape(tb, -1)
        ys.append(_dot(_dot(p_ref[...], gp).astype(BF16), wo_ref[...]))
    ins = []
    for s, y in zip(subs, ys):
        rows = slice(s * tb, (s + 1) * tb)
        h = x_ref[rows, :] + gate0 * _rms_rows(y, gpost_ref[...])
        h_ref[rows, :] = h
        kv_in = _rms_rows(h, gkv_ref[...]).astype(BF16)
        h_in = (_rms_rows(h, gpre_ref[...]) * (1.0 + scale1) + shift1).astype(BF16)
        ins.append((kv_in, h_in))
    for s, (kv_in, h_in) in zip(subs, ins):
        k_ref[s * tb:(s + 1) * tb, :] = _dot(kv_in, wk_ref[...]).astype(BF16)
        vt = _dot_nt(wv_ref[...], kv_in).astype(BF16)
        for hd in range(vt.shape[0] // V_DIM):
            vt_ref[s, hd * V_ROWS:hd * V_ROWS + V_DIM, :] = vt[hd * V_DIM:(hd + 1) * V_DIM, :]
            vt_ref[s, hd * V_ROWS + V_DIM:(hd + 1) * V_ROWS, :] = ones_row
        qt_ref[s] = (_dot_nt(wq_ref[...], h_in) * Q_SCALE).astype(BF16)
        gz_ref[s] = (_silu(_dot_nt(wz_ref[...], h_in)) * gsub_ref[...]).astype(BF16)


def _mid(gated_perm, x, mod0, mod1, perm, wo, g_post0, g_kv, g_pre1, wk, wv_t, wq_t, wz_t, gsub):
    bsz, seq, d = x.shape
    tb = TOKEN_BLOCK
    nb = seq // tb
    sub = MID_SUB
    e = gated_perm.shape[-1]
    qk = wk.shape[1]
    av = wv_t.shape[0]
    row = lambda b, j: (b, j, 0)
    const = lambda shape: pl.BlockSpec(shape, lambda b, j: (0, 0), pipeline_mode=pl.Buffered(1))
    t_spec = lambda n: pl.BlockSpec((None, sub, n, tb), lambda b, j: (b, j, 0, 0))
    return pl.pallas_call(
        _mid_kernel,
        grid=(bsz, nb // sub),
        in_specs=[
            pl.BlockSpec((None, CHUNK, sub * tb // CHUNK, e), lambda b, j: (b, 0, j, 0)),
            pl.BlockSpec((None, sub * tb, d), row),
            pl.BlockSpec((None, 1, 3 * d), lambda b, j: (b, 0, 0)),
            pl.BlockSpec((None, 1, 3 * d), lambda b, j: (b, 0, 0)),
            const((tb, tb)),
            const((e, d)),
            const((1, d)),
            const((1, d)),
            const((1, d)),
            const((d, qk)),
            const((av, d)),
            const((qk, d)),
            const((av, d)),
            const((av, 1)),
        ],
        out_specs=[
            pl.BlockSpec((None, sub * tb, d), row),
            pl.BlockSpec((None, sub * tb, qk), row),
            t_spec(av // V_DIM * V_ROWS),
            t_spec(qk),
            t_spec(av),
        ],
        out_shape=[
            jax.ShapeDtypeStruct((bsz, seq, d), F32),
            jax.ShapeDtypeStruct((bsz, seq, qk), BF16),
            jax.ShapeDtypeStruct((bsz, nb, av // V_DIM * V_ROWS, tb), BF16),
            jax.ShapeDtypeStruct((bsz, nb, qk, tb), BF16),
            jax.ShapeDtypeStruct((bsz, nb, av, tb), BF16),
        ],
        compiler_params=_params(("parallel", "parallel")),
        name="mid",
    )(gated_perm.reshape(bsz, CHUNK, seq // CHUNK, e), x, mod0.reshape(bsz, 1, 3 * d),
      mod1.reshape(bsz, 1, 3 * d), perm, wo, g_post0.reshape(1, d), g_kv.reshape(1, d),
      g_pre1.reshape(1, d), wk, wv_t, wq_t, wz_t, gsub.reshape(av, 1))


def _attention_kernel(lam_ref, qt_ref, qn_ref, k_ref, vt_ref, gz_ref, bias_ref, h_ref, mod1_ref, wo_ref,
                      gpost_ref, o_ref, qp_ref, m_ref, acc_ref, s_ref, og_ref):
    tq = qt_ref.shape[-1]
    hw = 2 * HEAD_DIM
    heads = qt_ref.shape[0] // hw
    i = pl.program_id(1)

    zero = jnp.zeros((HEAD_DIM, tq), qt_ref.dtype)

    def pad_queries(q_ref, n_heads):
        for h in range(n_heads):
            qt = q_ref[h * hw:(h + 1) * hw, :]
            qp_ref[2 * h] = jnp.concatenate([qt[:HEAD_DIM], zero], axis=0)
            qp_ref[2 * h + 1] = jnp.concatenate([zero, qt[HEAD_DIM:]], axis=0)

    pad_queries(qt_ref, heads)
    n_strips = 2 * heads

    def scores(j, n):
        h = n // 2
        row0 = pl.multiple_of(j * tq, tq)
        return _dot(k_ref[pl.ds(row0, tq), h * hw:(h + 1) * hw], qp_ref[n])

    def step(j, last=False):
        for n in range(n_strips):
            s = s_ref[n % ATT_AHEAD]
            if n + ATT_AHEAD < n_strips:
                s_ref[n % ATT_AHEAD] = scores(j, n + ATT_AHEAD)
            elif not last:
                s_ref[n % ATT_AHEAD] = scores(j + 1, n + ATT_AHEAD - n_strips)
            m_old = m_ref[n]
            if last:
                hq = tq // 2
                s00 = s[:hq, :hq] + bias_ref[...]
                s01 = s[:hq, hq:]
                s11 = s[hq:, hq:] + bias_ref[...]
                m_blk = jnp.concatenate(
                    [jnp.max(s00, axis=0, keepdims=True),
                     jnp.maximum(jnp.max(s01, axis=0, keepdims=True),
                                 jnp.max(s11, axis=0, keepdims=True))], axis=1)
                m_new = jnp.maximum(m_old, m_blk)
                e = lambda v, m: jnp.exp2((v - m).astype(BF16))
                p = jnp.concatenate(
                    [jnp.concatenate([e(s00, m_new[:, :hq]), e(s01, m_new[:, hq:])], axis=1),
                     jnp.concatenate([jnp.zeros((hq, hq), BF16), e(s11, m_new[:, hq:])], axis=1)],
                    axis=0)
            else:
                m_new = jnp.maximum(m_old, jnp.max(s, axis=0, keepdims=True))
                p = jnp.exp2((s - m_new).astype(BF16))
            alpha = jnp.exp2(m_old - m_new)
            m_ref[n] = m_new
            h = n // 2
            vt = vt_ref[j, h * V_ROWS:(h + 1) * V_ROWS, :]
            acc_ref[n] = alpha * acc_ref[n] + _dot(vt, p)

    @pl.when(i == 0)
    def _():
        for n in range(ATT_AHEAD):
            s_ref[n] = scores(0, n)

    m_ref[...] = jnp.full(m_ref.shape, -jnp.inf, F32)
    acc_ref[...] = jnp.zeros(acc_ref.shape, F32)

    @pl.when(i == 0)
    def _():
        step(0, last=True)

    @pl.when(i > 0)
    def _():
        first = (i - 1) % 2

        @pl.when(first == 1)
        def _():
            step(0)

        def two_steps(t, carry):
            step(first + 2 * t)
            step(first + 2 * t + 1)
            return carry

        lax.fori_loop(0, (i - 1) // 2, two_steps, 0)
        step(i - 1)
        step(i, last=True)

    pad_queries(qn_ref, ATT_AHEAD // 2)
    for n in range(ATT_AHEAD):
        s_ref[n] = scores(0, n)

    for h in range(heads):
        r0 = 1.0 / acc_ref[2 * h, V_DIM:V_DIM + 1, :]
        r1 = lam_ref[0] / acc_ref[2 * h + 1, V_DIM:V_DIM + 1, :]
        o = acc_ref[2 * h, :V_DIM, :] * r0 - acc_ref[2 * h + 1, :V_DIM, :] * r1
        rows = slice(h * V_DIM, (h + 1) * V_DIM)
        inv_rms = lax.rsqrt(jnp.mean(o * o, axis=0, keepdims=True) + EPS)
        og_ref[rows, :] = (o * inv_rms * gz_ref[rows, :].astype(F32)).astype(BF16)
        if h % 2 == 1:
            pr = slice((h - 1) * V_DIM, (h + 1) * V_DIM)
            part = _dot_tn(og_ref[pr, :], wo_ref[pr, :])
            y = part if h == 1 else y + part
    d = h_ref.shape[-1]
    gate1 = mod1_ref[:, 2 * d:3 * d]
    o_ref[...] = h_ref[...] + gate1 * _rms_rows(y, gpost_ref[...])


def _attention(lam, q_t, k, v_t, gz_t, h, mod1, wo, g_post1):
    bsz, nb, width, tb = q_t.shape
    seq = nb * tb
    d = h.shape[-1]
    heads = width // (2 * HEAD_DIM)
    assert heads % 2 == 0 and (2 * heads) % ATT_AHEAD == 0
    pos = jnp.arange(tb // 2)
    bias = jnp.where(pos[:, None] <= pos[None, :], 0.0, -jnp.inf).astype(F32)
    blk = pl.BlockSpec((None, None, width, tb), lambda b, i: (b, i, 0, 0))
    once = pl.Buffered(1)
    return pl.pallas_call(
        _attention_kernel,
        grid=(bsz, nb),
        in_specs=[
            pl.BlockSpec(memory_space=pltpu.SMEM),
            blk,
            pl.BlockSpec((None, None, width, tb), lambda b, i: (b, jnp.minimum(i + 1, nb - 1), 0, 0)),
            pl.BlockSpec((None, seq, width), lambda b, i: (b, 0, 0)),
            pl.BlockSpec((None, nb, heads * V_ROWS, tb), lambda b, i: (b, 0, 0, 0)),
            blk,
            pl.BlockSpec((tb // 2, tb // 2), lambda b, i: (0, 0), pipeline_mode=once),
            pl.BlockSpec((None, tb, d), lambda b, i: (b, i, 0)),
            pl.BlockSpec((None, 1, 3 * d), lambda b, i: (b, 0, 0)),
            pl.BlockSpec((width, d), lambda b, i: (0, 0), pipeline_mode=once),
            pl.BlockSpec((1, d), lambda b, i: (0, 0), pipeline_mode=once),
        ],
        out_specs=pl.BlockSpec((None, tb, d), lambda b, i: (b, i, 0)),
        out_shape=jax.ShapeDtypeStruct((bsz, seq, d), F32),
        scratch_shapes=[
            pltpu.VMEM((2 * heads, 2 * HEAD_DIM, tb), BF16),
            pltpu.VMEM((2 * heads, 1, tb), F32),
            pltpu.VMEM((2 * heads, V_ROWS, tb), F32),
            pltpu.VMEM((ATT_AHEAD, tb, tb), F32),
            pltpu.VMEM((width, tb), BF16),
        ],
        compiler_params=pltpu.CompilerParams(dimension_semantics=("parallel", "arbitrary"),
                                             vmem_limit_bytes=ATT_VMEM_LIMIT),
        name="attention",
    )(lam, q_t, q_t, k, v_t, gz_t, bias, h, mod1.reshape(bsz, 1, 3 * d), wo, g_post1.reshape(1, d))


def kernel(x, c, ada_w, ada_b, g_pre, g_post, a_w_in, a_lam_re, a_lam_im, a_log_dt, a_b_re, a_b_im,
           a_c_re, a_c_im, a_d, a_w_glu, a_b_glu, a_w_out, g_kv, w_k, w_v, b_w_in, b_lq1, b_lk1,
           b_lq2, b_lk2, b_g_sub, b_w_out):
    bsz, seq, d = x.shape
    e = a_w_glu.shape[1]
    qk = w_k.shape[1]
    assert seq % (CHUNK * 128) == 0 and seq % GLU_BLOCK == 0 and d % 128 == 0
    assert e % (GROUPS_PER_STEP * SSM_GROUP) == 0

    mod = _modulation(c, ada_w, ada_b)
    perm = _chunk_permutation(TOKEN_BLOCK)

    hperm, toep, state_in, state_out, decay = _prenorm_and_operators(
        x, mod[0], g_pre[0], perm,
        a_lam_re[0], a_lam_im[0], a_log_dt[0], a_b_re[0], a_b_im[0], a_c_re[0], a_c_im[0], a_d[0])
    hperm = hperm.reshape(bsz, seq, d)
    w_in_t = a_w_in[0].T.astype(BF16)
    y_t = _ssm(hperm, w_in_t[:e], toep, state_in, state_out, decay)
    gated = _glu(y_t, hperm, a_w_glu[0].T.astype(BF16), a_b_glu[0], w_in_t[e:])

    layer = DEPTH // 2
    lambda_init = 0.8 - 0.6 * math.exp(-0.3 * layer)
    w_b_t = b_w_in[0].T.astype(BF16)
    gsub = jnp.tile(b_g_sub[0] * (1.0 - lambda_init), w_v.shape[1] // V_DIM)
    h, k, v_t, q_t, gz_t = _mid(
        gated, x, mod[0], mod[1], perm, a_w_out[0].astype(BF16), g_post[0], g_kv, g_pre[1],
        w_k.astype(BF16), w_v.T.astype(BF16), w_b_t[:qk], w_b_t[qk:], gsub)

    lam = (jnp.exp(jnp.sum(b_lq1[0] * b_lk1[0])) - jnp.exp(jnp.sum(b_lq2[0] * b_lk2[0]))
           + lambda_init).reshape(1).astype(F32)
    return _attention(lam, q_t, k, v_t, gz_t, h, mod[1], b_w_out[0].astype(BF16), g_post[1])
```

```python
import functools
import math

import jax
import jax.numpy as jnp
from jax import lax
from jax.experimental import pallas as pl
from jax.experimental.pallas import tpu as pltpu

F32 = jnp.float32
BF16 = jnp.bfloat16

EPS = 1e-6
DEPTH = 2
SSM_GROUP = 16
SSM_STATE = 64
CHUNK = 16
GROUPS_PER_STEP = 16
SSM_UNROLL = 8
HEAD_DIM = 64
V_DIM = 2 * HEAD_DIM
V_ROWS = V_DIM + 16
TOKEN_BLOCK = 256
PRENORM_BLOCK = 1024
GLU_BLOCK = 1024
MID_SUB = 4
ATT_AHEAD = 8
ATT_BLOCKS = 3
Q_SCALE = HEAD_DIM ** -0.5 * math.log2(math.e)
VMEM_LIMIT = 48 * 1024 * 1024
ATT_VMEM_LIMIT = 56 * 1024 * 1024


def _params(semantics):
    return pltpu.CompilerParams(dimension_semantics=semantics, vmem_limit_bytes=VMEM_LIMIT)


def _sigmoid(v):
    return 1.0 / (1.0 + jnp.exp(-v))


def _silu(v):
    return v * _sigmoid(v)


def _gelu_tanh(v):
    k = -2.0 * math.sqrt(2.0 / math.pi) * math.log2(math.e)
    return v / (1.0 + jnp.exp2(v * ((k * 0.044715) * (v * v) + k)))


def _rms_rows(v, g):
    return v * lax.rsqrt(jnp.mean(v * v, axis=-1, keepdims=True) + EPS) * g


def _dot(a, b):
    return jnp.dot(a, b, preferred_element_type=F32)


def _dot_nt(a, b):
    return lax.dot_general(a, b, (((1,), (1,)), ((), ())), preferred_element_type=F32)


def _dot_tn(a, b):
    return lax.dot_general(a, b, (((0,), (0,)), ((), ())), preferred_element_type=F32)


def _modulation_kernel(c_ref, w_ref, b_ref, o_ref):
    def split(v):
        hi = v.astype(BF16)
        return hi, (v - hi.astype(F32)).astype(BF16)

    (sh, sl), (wh, wl) = split(_silu(c_ref[...])), split(w_ref[...])
    o_ref[...] = _dot(sh, wh) + _dot(sh, wl) + _dot(sl, wh) + b_ref[...]


def _modulation(c, ada_w, ada_b):
    bsz, d = c.shape
    depth, _, n = ada_w.shape
    tn = 1024
    return pl.pallas_call(
        _modulation_kernel,
        grid=(depth, n // tn),
        in_specs=[
            pl.BlockSpec((bsz, d), lambda l, j: (0, 0)),
            pl.BlockSpec((None, d, tn), lambda l, j: (l, 0, j)),
            pl.BlockSpec((None, 1, tn), lambda l, j: (l, 0, j)),
        ],
        out_specs=pl.BlockSpec((None, bsz, tn), lambda l, j: (l, 0, j)),
        out_shape=jax.ShapeDtypeStruct((depth, bsz, n), F32),
        compiler_params=_params(("parallel", "parallel")),
        name="modulation",
    )(c, ada_w, ada_b.reshape(depth, 1, n))


def _chunk_permutation(n):
    r = jnp.arange(n)
    src = (r % (n // CHUNK)) * CHUNK + r // (n // CHUNK)
    return (src[:, None] == r[None, :]).astype(BF16)


def _prenorm_kernel(x_ref, mod_ref, g_ref, p_ref, o_ref):
    d = x_ref.shape[-1]
    sub = p_ref.shape[0]
    shift = mod_ref[:, 0:d]
    scale = mod_ref[:, d:2 * d]
    for r in range(x_ref.shape[0] // sub):
        x = x_ref[r * sub:(r + 1) * sub, :]
        h = _rms_rows(x, g_ref[...]) * (1.0 + scale) + shift
        hp = _dot(p_ref[...], h.astype(BF16)).astype(BF16)
        o_ref[:, r * (sub // CHUNK):(r + 1) * (sub // CHUNK), :] = hp.reshape(CHUNK, sub // CHUNK, d)


def _operators_kernel(lam_re_ref, lam_im_ref, log_dt_ref, bt_re_ref, bt_im_ref, c_re_ref, c_im_ref,
                      d_ref, toep_ref, sin_ref, sout_ref, decay_ref):
    rows = CHUNK * SSM_GROUP
    lanes = 2 * SSM_STATE
    lam_re, lam_im, dt = lam_re_ref[...], lam_im_ref[...], jnp.exp(log_dt_ref[...])
    ar, ai = lam_re * dt, lam_im * dt

    def apow(k):
        mag = jnp.exp(k * ar)
        return mag * jnp.cos(k * ai), mag * jnp.sin(k * ai)

    pos = lax.broadcasted_iota(jnp.int32, (CHUNK, 1), 0).astype(F32)
    a1r, a1i = apow(jnp.ones((1, 1), F32))
    den = lam_re * lam_re + lam_im * lam_im
    fr = ((a1r - 1.0) * lam_re + a1i * lam_im) / den
    fi = (a1i * lam_re - (a1r - 1.0) * lam_im) / den
    bbr = fr * bt_re_ref[...] - fi * bt_im_ref[...]
    bbi = fr * bt_im_ref[...] + fi * bt_re_ref[...]

    r_idx = lax.broadcasted_iota(jnp.int32, (rows, CHUNK), 0)
    k_idx = lax.broadcasted_iota(jnp.int32, (rows, CHUNK), 1)
    rep = (r_idx // SSM_GROUP == k_idx).astype(BF16)
    tile = (r_idx % SSM_GROUP == k_idx).astype(BF16)
    lane_tile = (lax.broadcasted_iota(jnp.int32, (SSM_GROUP, rows), 1) % SSM_GROUP
                 == lax.broadcasted_iota(jnp.int32, (SSM_GROUP, rows), 0)).astype(BF16)

    def split(v):
        hi = v.astype(BF16)
        return hi, (v - hi.astype(F32)).astype(BF16)

    def expand(sel, v):
        hi, lo = split(v)
        return _dot(sel, hi) + _dot(sel, lo)

    def dot3(x, y):
        (xh, xl), (yh, yl) = split(x), split(y)
        return _dot_nt(xh, yh) + _dot_nt(xh, yl) + _dot_nt(xl, yh)

    def times(xr, xi, yr, yi):
        return xr * yr - xi * yi, xr * yi + xi * yr

    left = lax.broadcasted_iota(jnp.int32, (rows, lanes), 1) < SSM_STATE
    halves = lambda v: (jnp.where(left, v, 0.0), jnp.where(left, 0.0, v))

    qr, qi = times(*apow(CHUNK - 1.0 - pos), fr, fi)
    sr, si = times(expand(rep, qr), expand(rep, qi),
                   expand(tile, bt_re_ref[...]), expand(tile, bt_im_ref[...]))
    cr, ci = expand(tile, c_re_ref[...]), expand(tile, c_im_ref[...])
    wr, wi = apow(pos + 1.0)
    our, oui = times(cr, ci, expand(rep, wr), expand(rep, wi))
    for h, (s_r, s_i, o_r, o_i) in enumerate(zip(halves(sr), halves(si), halves(our), halves(oui))):
        sin_ref[h, 0] = s_r.astype(BF16)
        sin_ref[h, 1] = s_i.astype(BF16)
        sout_ref[h, 0] = o_r.astype(BF16)
        sout_ref[h, 1] = (-o_i).astype(BF16)

    pr, pi = apow(pos)
    lr, li = times(cr, ci, expand(rep, pr), expand(rep, pi))
    lane_blk = lax.broadcasted_iota(jnp.int32, (rows, rows), 1) // SSM_GROUP
    diag = (lax.broadcasted_iota(jnp.int32, (rows, rows), 0)
            == lax.broadcasted_iota(jnp.int32, (rows, rows), 1))
    for h, (l_r, l_i) in enumerate(zip(halves(lr), halves(li))):
        kern = dot3(l_r, bbr) - dot3(l_i, bbi)
        k_hi, k_lo = split(kern)
        wide = _dot(k_hi, lane_tile) + _dot(k_lo, lane_tile)
        toep = jnp.where(diag, d_ref[h], 0.0)
        for p in range(CHUNK):
            n = p * SSM_GROUP
            delayed = wide if p == 0 else jnp.concatenate(
                [jnp.zeros((n, rows), F32), wide[:rows - n]], axis=0)
            toep = toep + jnp.where(lane_blk == p, delayed, 0.0)
        toep_ref[h] = toep.astype(BF16)

    dr, di = apow(jnp.full((1, 1), float(CHUNK), F32))
    sub = lax.broadcasted_iota(jnp.int32, (8, lanes), 0)
    decay_ref[...] = jnp.where(sub == 0, dr, jnp.where(sub == 1, di, 0.0))


def _operators_block_kernel(*refs):
    for k in range(refs[0].shape[0]):
        two = pl.ds(2 * k, 2)
        _operators_kernel(*[r.at[k] for r in refs[:7]], refs[7].at[two], refs[8].at[two],
                          refs[9].at[two], refs[10].at[two], refs[11].at[k])


def _prenorm_operators_kernel(*refs):
    _prenorm_kernel(*refs[:4], refs[12])
    _operators_block_kernel(*refs[4:12], *refs[13:])


def _prenorm_and_operators(x, mod0, g_pre0, perm, lam_re, lam_im, log_dt, b_re, b_im, c_re, c_im,
                           d_skip):
    bsz, seq, d = x.shape
    tb = PRENORM_BLOCK
    nblk = seq // tb
    g, p = lam_re.shape
    cpg = SSM_GROUP
    rows = CHUNK * cpg
    pairs = g // 2
    lanes = 2 * p
    row_pair = lambda v: v.reshape(pairs, 1, lanes)
    mat_pair = lambda m: m.reshape(pairs, 2, cpg, p).transpose(0, 2, 1, 3).reshape(pairs, cpg, lanes)
    op_args = (row_pair(lam_re), row_pair(lam_im),
               row_pair(jnp.broadcast_to(log_dt[:, None], (g, p))),
               mat_pair(b_re.transpose(0, 2, 1)), mat_pair(b_im.transpose(0, 2, 1)),
               mat_pair(c_re), mat_pair(c_im),
               jnp.tile(d_skip.reshape(g, 1, cpg), (1, 1, CHUNK)))
    op_shapes = [
        jax.ShapeDtypeStruct((g, rows, rows), BF16),
        jax.ShapeDtypeStruct((g, 2, rows, lanes), BF16),
        jax.ShapeDtypeStruct((g, 2, rows, lanes), BF16),
        jax.ShapeDtypeStruct((pairs, 8, lanes), F32),
    ]

    def op_specs(per, at):
        vec = pl.BlockSpec((per, 1, lanes), lambda *i: (at(*i), 0, 0))
        mat = pl.BlockSpec((per, cpg, lanes), lambda *i: (at(*i), 0, 0))
        ins = [vec, vec, vec, mat, mat, mat, mat,
               pl.BlockSpec((2 * per, 1, rows), lambda *i: (at(*i), 0, 0))]
        outs = [
            pl.BlockSpec((2 * per, rows, rows), lambda *i: (at(*i), 0, 0)),
            pl.BlockSpec((2 * per, 2, rows, lanes), lambda *i: (at(*i), 0, 0, 0)),
            pl.BlockSpec((2 * per, 2, rows, lanes), lambda *i: (at(*i), 0, 0, 0)),
            pl.BlockSpec((per, 8, lanes), lambda *i: (at(*i), 0, 0)),
        ]
        return ins, outs

    pre_args = (x, mod0.reshape(bsz, 1, 3 * d), g_pre0.reshape(1, d), perm)
    pre_specs = [
        pl.BlockSpec((None, tb, d), lambda b, j: (b, j, 0)),
        pl.BlockSpec((None, 1, 3 * d), lambda b, j: (b, 0, 0)),
        pl.BlockSpec((1, d), lambda b, j: (0, 0)),
        pl.BlockSpec(perm.shape, lambda b, j: (0, 0)),
    ]
    pre_out = pl.BlockSpec((None, CHUNK, tb // CHUNK, d), lambda b, j: (b, 0, j, 0))
    pre_shape = jax.ShapeDtypeStruct((bsz, CHUNK, seq // CHUNK, d), BF16)

    steps = bsz * nblk
    if pairs % steps == 0:
        ins, outs = op_specs(pairs // steps, lambda b, j: b * nblk + j)
        return pl.pallas_call(
            _prenorm_operators_kernel,
            grid=(bsz, nblk),
            in_specs=pre_specs + ins,
            out_specs=[pre_out] + outs,
            out_shape=[pre_shape] + op_shapes,
            compiler_params=_params(("parallel", "parallel")),
            name="prenorm_operators",
        )(*pre_args, *op_args)
    hperm = pl.pallas_call(
        _prenorm_kernel, grid=(bsz, nblk), in_specs=pre_specs, out_specs=pre_out,
        out_shape=pre_shape, compiler_params=_params(("parallel", "parallel")), name="prenorm",
    )(*pre_args)
    ins, outs = op_specs(1, lambda q: q)
    return (hperm, *pl.pallas_call(
        _operators_block_kernel, grid=(pairs,), in_specs=ins, out_specs=outs, out_shape=op_shapes,
        compiler_params=_params(("parallel",)), name="operators",
    )(*op_args))


def _ssm_kernel(h_ref, wu_ref, toep_ref, sin_ref, sout_ref, decay_ref, y_ref, xs_ref):
    n_chunks = y_ref.shape[-1] // CHUNK
    n_state = SSM_STATE
    j = pl.program_id(1)
    last = pl.num_programs(1) - 1
    fill, drain = j % 2, (j + 1) % 2

    def project(p):
        r0 = pl.multiple_of(p * n_chunks, n_chunks)
        u = _dot_nt(wu_ref[...], h_ref[pl.ds(r0, n_chunks), :]).astype(BF16)
        c0 = pl.multiple_of(p * SSM_GROUP, SSM_GROUP)
        for g in range(GROUPS_PER_STEP):
            xs_ref[fill, g, pl.ds(c0, SSM_GROUP), :] = u[g * SSM_GROUP:(g + 1) * SSM_GROUP, :]

    row = lax.broadcasted_iota(jnp.int32, (n_chunks, 2 * n_state), 0)

    def shift_rows(v, s):
        if s % 8 == 0:
            return jnp.concatenate([jnp.zeros((s, v.shape[1]), v.dtype), v[:-s]], axis=0)
        return jnp.where(row >= s, pltpu.roll(v, s, 0), 0.0)

    def gains(q):
        g0, g1 = 2 * q, 2 * q + 1
        x0, x1 = xs_ref[drain, g0], xs_ref[drain, g1]
        inc_r = _dot_tn(x0, sin_ref[g0, 0]) + _dot_tn(x1, sin_ref[g1, 0])
        inc_i = _dot_tn(x0, sin_ref[g0, 1]) + _dot_tn(x1, sin_ref[g1, 1])
        return inc_r, inc_i

    def finish(q, inc_r, inc_i):
        g0, g1 = 2 * q, 2 * q + 1
        er, ei = shift_rows(inc_r, 1), shift_rows(inc_i, 1)
        ar, ai = decay_ref[q, 0:1, :], decay_ref[q, 1:2, :]
        s = 1
        while s < n_chunks:
            if s % 8 == 0:
                dr = ar * er[:-s] - ai * ei[:-s]
                di = ar * ei[:-s] + ai * er[:-s]
                er = jnp.concatenate([er[:s], er[s:] + dr], axis=0)
                ei = jnp.concatenate([ei[:s], ei[s:] + di], axis=0)
            else:
                sr, si = shift_rows(er, s), shift_rows(ei, s)
                er, ei = er + (ar * sr - ai * si), ei + (ar * si + ai * sr)
            ar, ai = ar * ar - ai * ai, 2.0 * (ar * ai)
            s *= 2
        sr, si = er.astype(BF16), ei.astype(BF16)
        for g in (g0, g1):
            y = (_dot(toep_ref[g], xs_ref[drain, g]) + _dot_nt(sout_ref[g, 0], sr)
                 + _dot_nt(sout_ref[g, 1], si))
            act = _gelu_tanh(y).astype(BF16)
            row0 = pl.multiple_of(g * SSM_GROUP, SSM_GROUP)
            for p in range(CHUNK):
                y_ref[pl.ds(row0, SSM_GROUP), p * n_chunks:(p + 1) * n_chunks] = (
                    act[p * SSM_GROUP:(p + 1) * SSM_GROUP, :])

    n_iter = GROUPS_PER_STEP // 2 // SSM_UNROLL
    per_iter = CHUNK // n_iter

    def body(it, carry, with_scan, with_projection):
        qs = [it * SSM_UNROLL + u for u in range(SSM_UNROLL)]
        started = [gains(q) for q in qs] if with_scan else []
        if with_projection:
            for pp in range(per_iter):
                project(it * per_iter + pp)
        for q, inc in zip(qs, started):
            finish(q, *inc)
        return carry

    @pl.when(j == 0)
    def _():
        lax.fori_loop(0, n_iter, functools.partial(body, with_scan=False, with_projection=True), 0)

    @pl.when(jnp.logical_and(j > 0, j < last))
    def _():
        lax.fori_loop(0, n_iter, functools.partial(body, with_scan=True, with_projection=True), 0)

    @pl.when(j == last)
    def _():
        lax.fori_loop(0, n_iter, functools.partial(body, with_scan=True, with_projection=False), 0)


def _ssm(hperm, wu_t, toep, state_in, state_out, decay):
    bsz, seq, d = hperm.shape
    e = wu_t.shape[0]
    cb = GROUPS_PER_STEP * SSM_GROUP
    rows = CHUNK * SSM_GROUP
    n_chunks = seq // CHUNK
    gps = GROUPS_PER_STEP
    nblk = e // cb
    assert (gps // 2) % SSM_UNROLL == 0 and CHUNK % (gps // 2 // SSM_UNROLL) == 0
    proj = lambda j: jnp.minimum(j, nblk - 1)
    scan = lambda j: jnp.maximum(j - 1, 0)
    return pl.pallas_call(
        _ssm_kernel,
        grid=(bsz, nblk + 1),
        in_specs=[
            pl.BlockSpec((None, seq, d), lambda b, j: (b, 0, 0)),
            pl.BlockSpec((cb, d), lambda b, j: (proj(j), 0)),
            pl.BlockSpec((gps, rows, rows), lambda b, j: (scan(j), 0, 0)),
            pl.BlockSpec((gps, 2, rows, 2 * SSM_STATE), lambda b, j: (scan(j), 0, 0, 0)),
            pl.BlockSpec((gps, 2, rows, 2 * SSM_STATE), lambda b, j: (scan(j), 0, 0, 0)),
            pl.BlockSpec((gps // 2, 8, 2 * SSM_STATE), lambda b, j: (scan(j), 0, 0)),
        ],
        out_specs=pl.BlockSpec((None, cb, seq), lambda b, j: (b, scan(j), 0)),
        out_shape=jax.ShapeDtypeStruct((bsz, e, seq), BF16),
        scratch_shapes=[pltpu.VMEM((2, gps, rows, n_chunks), BF16)],
        compiler_params=_params(("parallel", "arbitrary")),
        name="ssm",
    )(hperm, wu_t, toep, state_in, state_out, decay)


def _glu_kernel(y_ref, h_ref, wg_ref, bg_ref, wz_ref, o_ref):
    e = y_ref.shape[0]
    rb = 256
    ya = y_ref[...]
    hb = h_ref[...]
    for r in range(e // rb):
        rows = slice(r * rb, (r + 1) * rb)
        gl = _dot(wg_ref[rows, :], ya) + bg_ref[rows, :]
        z = _dot_nt(wz_ref[rows, :], hb)
        yr = y_ref[rows, :].astype(F32)
        gated = yr * _sigmoid(gl) * _silu(z)
        o_ref[:, rows] = gated.T.astype(BF16)


def _glu(y_t, hperm, wglu_t, b_glu, wz_t):
    bsz, e, seq = y_t.shape
    d = hperm.shape[-1]
    tn = GLU_BLOCK
    return pl.pallas_call(
        _glu_kernel,
        grid=(bsz, seq // tn),
        in_specs=[
            pl.BlockSpec((None, e, tn), lambda b, j: (b, 0, j)),
            pl.BlockSpec((None, tn, d), lambda b, j: (b, j, 0)),
            pl.BlockSpec((e, e), lambda b, j: (0, 0)),
            pl.BlockSpec((e, 1), lambda b, j: (0, 0)),
            pl.BlockSpec((e, d), lambda b, j: (0, 0)),
        ],
        out_specs=pl.BlockSpec((None, tn, e), lambda b, j: (b, j, 0)),
        out_shape=jax.ShapeDtypeStruct((bsz, seq, e), BF16),
        compiler_params=_params(("parallel", "parallel")),
        name="glu",
    )(y_t, hperm, wglu_t, b_glu.reshape(e, 1), wz_t)


def _mid_kernel(gp_ref, x_ref, mod0_ref, mod1_ref, p_ref, wo_ref, gpost_ref, gkv_ref, gpre_ref,
                wk_ref, wv_ref, wq_ref, wz_ref, gsub_ref, h_ref, k_ref, vt_ref, qt_ref, gz_ref):
    d = x_ref.shape[-1]
    tb = p_ref.shape[0]
    subs = range(x_ref.shape[0] // tb)
    cps = tb // CHUNK
    gate0 = mod0_ref[:, 2 * d:3 * d]
    shift1 = mod1_ref[:, 0:d]
    scale1 = mod1_ref[:, d:2 * d]
    ones_row = (lax.broadcasted_iota(jnp.int32, (V_ROWS - V_DIM, tb), 0) == 0).astype(BF16)

    ys = []
    for s in subs:
        gp = gp_ref[:, s * cps:(s + 1) * cps, :].reshape(tb, -1)
        ys.append(_dot(_dot(p_ref[...], gp).astype(BF16), wo_ref[...]))
    ins = []
    for s, y in zip(subs, ys):
        rows = slice(s * tb, (s + 1) * tb)
        h = x_ref[rows, :] + gate0 * _rms_rows(y, gpost_ref[...])
        h_ref[rows, :] = h
        kv_in = _rms_rows(h, gkv_ref[...]).astype(BF16)
        h_in = (_rms_rows(h, gpre_ref[...]) * (1.0 + scale1) + shift1).astype(BF16)
        ins.append((kv_in, h_in))
    for s, (kv_in, h_in) in zip(subs, ins):
        k_ref[s * tb:(s + 1) * tb, :] = _dot(kv_in, wk_ref[...]).astype(BF16)
        vt = _dot_nt(wv_ref[...], kv_in).astype(BF16)
        for hd in range(vt.shape[0] // V_DIM):
            vt_ref[s, hd * V_ROWS:hd * V_ROWS + V_DIM, :] = vt[hd * V_DIM:(hd + 1) * V_DIM, :]
            vt_ref[s, hd * V_ROWS + V_DIM:(hd + 1) * V_ROWS, :] = ones_row
        qt_ref[s] = (_dot_nt(wq_ref[...], h_in) * Q_SCALE).astype(BF16)
        gz_ref[s] = (_silu(_dot_nt(wz_ref[...], h_in)) * gsub_ref[...]).astype(BF16)


def _mid(gated_perm, x, mod0, mod1, perm, wo, g_post0, g_kv, g_pre1, wk, wv_t, wq_t, wz_t, gsub):
    bsz, seq, d = x.shape
    tb = TOKEN_BLOCK
    nb = seq // tb
    sub = MID_SUB
    e = gated_perm.shape[-1]
    qk = wk.shape[1]
    av = wv_t.shape[0]
    row = lambda b, j: (b, j, 0)
    const = lambda shape: pl.BlockSpec(shape, lambda b, j: (0, 0), pipeline_mode=pl.Buffered(1))
    t_spec = lambda n: pl.BlockSpec((None, sub, n, tb), lambda b, j: (b, j, 0, 0))
    return pl.pallas_call(
        _mid_kernel,
        grid=(bsz, nb // sub),
        in_specs=[
            pl.BlockSpec((None, CHUNK, sub * tb // CHUNK, e), lambda b, j: (b, 0, j, 0)),
            pl.BlockSpec((None, sub * tb, d), row),
            pl.BlockSpec((None, 1, 3 * d), lambda b, j: (b, 0, 0)),
            pl.BlockSpec((None, 1, 3 * d), lambda b, j: (b, 0, 0)),
            const((tb, tb)),
            const((e, d)),
            const((1, d)),
            const((1, d)),
            const((1, d)),
            const((d, qk)),
            const((av, d)),
            const((qk, d)),
            const((av, d)),
            const((av, 1)),
        ],
        out_specs=[
            pl.BlockSpec((None, sub * tb, d), row),
            pl.BlockSpec((None, sub * tb, qk), row),
            t_spec(av // V_DIM * V_ROWS),
            t_spec(qk),
            t_spec(av),
        ],
        out_shape=[
            jax.ShapeDtypeStruct((bsz, seq, d), F32),
            jax.ShapeDtypeStruct((bsz, seq, qk), BF16),
            jax.ShapeDtypeStruct((bsz, nb, av // V_DIM * V_ROWS, tb), BF16),
            jax.ShapeDtypeStruct((bsz, nb, qk, tb), BF16),
            jax.ShapeDtypeStruct((bsz, nb, av, tb), BF16),
        ],
        compiler_params=_params(("parallel", "parallel")),
        name="mid",
    )(gated_perm.reshape(bsz, CHUNK, seq // CHUNK, e), x, mod0.reshape(bsz, 1, 3 * d),
      mod1.reshape(bsz, 1, 3 * d), perm, wo, g_post0.reshape(1, d), g_kv.reshape(1, d),
      g_pre1.reshape(1, d), wk, wv_t, wq_t, wz_t, gsub.reshape(av, 1))


def _attention_kernel(lam_ref, qt_ref, qn_ref, k_ref, vt_ref, gz_ref, bias_ref, h_ref, mod1_ref, wo_ref,
                      gpost_ref, o_ref, qp_ref, m_ref, acc_ref, s_ref, og_ref):
    tq = qt_ref.shape[-1]
    hw = 2 * HEAD_DIM
    heads = qt_ref.shape[0] // hw
    i = pl.program_id(1)

    zero = jnp.zeros((HEAD_DIM, tq), qt_ref.dtype)

    def pad_queries(q_ref, n_heads):
        for h in range(n_heads):
            qt = q_ref[h * hw:(h + 1) * hw, :]
            qp_ref[2 * h] = jnp.concatenate([qt[:HEAD_DIM], zero], axis=0)
            qp_ref[2 * h + 1] = jnp.concatenate([zero, qt[HEAD_DIM:]], axis=0)

    pad_queries(qt_ref, heads)
    n_strips = 2 * heads

    def scores(j, n):
        h = n // 2
        row0 = pl.multiple_of(j * tq, tq)
        return _dot(k_ref[pl.ds(row0, tq), h * hw:(h + 1) * hw], qp_ref[n])

    def step(j, last=False):
        for n in range(n_strips):
            s = s_ref[n % ATT_AHEAD]
            if n + ATT_AHEAD < n_strips:
                s_ref[n % ATT_AHEAD] = scores(j, n + ATT_AHEAD)
            elif not last:
                s_ref[n % ATT_AHEAD] = scores(j + 1, n + ATT_AHEAD - n_strips)
            m_old = m_ref[n]
            if last:
                hq = tq // 2
                s00 = s[:hq, :hq] + bias_ref[...]
                s01 = s[:hq, hq:]
                s11 = s[hq:, hq:] + bias_ref[...]
                m_blk = jnp.concatenate(
                    [jnp.max(s00, axis=0, keepdims=True),
                     jnp.maximum(jnp.max(s01, axis=0, keepdims=True),
                                 jnp.max(s11, axis=0, keepdims=True))], axis=1)
                m_new = jnp.maximum(m_old, m_blk)
                e = lambda v, m: jnp.exp2((v - m).astype(BF16))
                p = jnp.concatenate(
                    [jnp.concatenate([e(s00, m_new[:, :hq]), e(s01, m_new[:, hq:])], axis=1),
                     jnp.concatenate([jnp.zeros((hq, hq), BF16), e(s11, m_new[:, hq:])], axis=1)],
                    axis=0)
            else:
                m_new = jnp.maximum(m_old, jnp.max(s, axis=0, keepdims=True))
                p = jnp.exp2((s - m_new).astype(BF16))
            alpha = jnp.exp2(m_old - m_new)
            m_ref[n] = m_new
            h = n // 2
            vt = vt_ref[j, h * V_ROWS:(h + 1) * V_ROWS, :]
            acc_ref[n] = alpha * acc_ref[n] + _dot(vt, p)

    @pl.when(i == 0)
    def _():
        for n in range(ATT_AHEAD):
            s_ref[n] = scores(0, n)

    m_ref[...] = jnp.full(m_ref.shape, -jnp.inf, F32)
    acc_ref[...] = jnp.zeros(acc_ref.shape, F32)

    @pl.when(i == 0)
    def _():
        step(0, last=True)

    @pl.when(i > 0)
    def _():
        first = (i - 1) % ATT_BLOCKS
        for lead in range(1, ATT_BLOCKS):
            @pl.when(first == lead)
            def _():
                for j in range(lead):
                    step(j)

        def grouped_steps(t, carry):
            for u in range(ATT_BLOCKS):
                step(first + ATT_BLOCKS * t + u)
            return carry

        lax.fori_loop(0, (i - 1) // ATT_BLOCKS, grouped_steps, 0)
        step(i - 1)
        step(i, last=True)

    pad_queries(qn_ref, ATT_AHEAD // 2)
    for n in range(ATT_AHEAD):
        s_ref[n] = scores(0, n)

    for h in range(heads):
        r0 = 1.0 / acc_ref[2 * h, V_DIM:V_DIM + 1, :]
        r1 = lam_ref[0] / acc_ref[2 * h + 1, V_DIM:V_DIM + 1, :]
        o = acc_ref[2 * h, :V_DIM, :] * r0 - acc_ref[2 * h + 1, :V_DIM, :] * r1
        rows = slice(h * V_DIM, (h + 1) * V_DIM)
        inv_rms = lax.rsqrt(jnp.mean(o * o, axis=0, keepdims=True) + EPS)
        og_ref[rows, :] = (o * inv_rms * gz_ref[rows, :].astype(F32)).astype(BF16)
        if h % 2 == 1:
            pr = slice((h - 1) * V_DIM, (h + 1) * V_DIM)
            part = _dot_tn(og_ref[pr, :], wo_ref[pr, :])
            y = part if h == 1 else y + part
    d = h_ref.shape[-1]
    gate1 = mod1_ref[:, 2 * d:3 * d]
    o_ref[...] = h_ref[...] + gate1 * _rms_rows(y, gpost_ref[...])


def _attention(lam, q_t, k, v_t, gz_t, h, mod1, wo, g_post1):
    bsz, nb, width, tb = q_t.shape
    seq = nb * tb
    d = h.shape[-1]
    heads = width // (2 * HEAD_DIM)
    assert heads % 2 == 0 and (2 * heads) % ATT_AHEAD == 0
    pos = jnp.arange(tb // 2)
    bias = jnp.where(pos[:, None] <= pos[None, :], 0.0, -jnp.inf).astype(F32)
    blk = pl.BlockSpec((None, None, width, tb), lambda b, i: (b, i, 0, 0))
    once = pl.Buffered(1)
    return pl.pallas_call(
        _attention_kernel,
        grid=(bsz, nb),
        in_specs=[
            pl.BlockSpec(memory_space=pltpu.SMEM),
            blk,
            pl.BlockSpec((None, None, width, tb), lambda b, i: (b, jnp.minimum(i + 1, nb - 1), 0, 0)),
            pl.BlockSpec((None, seq, width), lambda b, i: (b, 0, 0)),
            pl.BlockSpec((None, nb, heads * V_ROWS, tb), lambda b, i: (b, 0, 0, 0)),
            blk,
            pl.BlockSpec((tb // 2, tb // 2), lambda b, i: (0, 0), pipeline_mode=once),
            pl.BlockSpec((None, tb, d), lambda b, i: (b, i, 0)),
            pl.BlockSpec((None, 1, 3 * d), lambda b, i: (b, 0, 0)),
            pl.BlockSpec((width, d), lambda b, i: (0, 0), pipeline_mode=once),
            pl.BlockSpec((1, d), lambda b, i: (0, 0), pipeline_mode=once),
        ],
        out_specs=pl.BlockSpec((None, tb, d), lambda b, i: (b, i, 0)),
        out_shape=jax.ShapeDtypeStruct((bsz, seq, d), F32),
        scratch_shapes=[
            pltpu.VMEM((2 * heads, 2 * HEAD_DIM, tb), BF16),
            pltpu.VMEM((2 * heads, 1, tb), F32),
            pltpu.VMEM((2 * heads, V_ROWS, tb), F32),
            pltpu.VMEM((ATT_AHEAD, tb, tb), F32),
            pltpu.VMEM((width, tb), BF16),
        ],
        compiler_params=pltpu.CompilerParams(dimension_semantics=("parallel", "arbitrary"),
                                             vmem_limit_bytes=ATT_VMEM_LIMIT),
        name="attention",
    )(lam, q_t, q_t, k, v_t, gz_t, bias, h, mod1.reshape(bsz, 1, 3 * d), wo, g_post1.reshape(1, d))


def kernel(x, c, ada_w, ada_b, g_pre, g_post, a_w_in, a_lam_re, a_lam_im, a_log_dt, a_b_re, a_b_im,
           a_c_re, a_c_im, a_d, a_w_glu, a_b_glu, a_w_out, g_kv, w_k, w_v, b_w_in, b_lq1, b_lk1,
           b_lq2, b_lk2, b_g_sub, b_w_out):
    bsz, seq, d = x.shape
    e = a_w_glu.shape[1]
    qk = w_k.shape[1]
    assert seq % (CHUNK * 128) == 0 and seq % GLU_BLOCK == 0 and d % 128 == 0
    assert e % (GROUPS_PER_STEP * SSM_GROUP) == 0

    mod = _modulation(c, ada_w, ada_b)
    perm = _chunk_permutation(TOKEN_BLOCK)

    hperm, toep, state_in, state_out, decay = _prenorm_and_operators(
        x, mod[0], g_pre[0], perm,
        a_lam_re[0], a_lam_im[0], a_log_dt[0], a_b_re[0], a_b_im[0], a_c_re[0], a_c_im[0], a_d[0])
    hperm = hperm.reshape(bsz, seq, d)
    w_in_t = a_w_in[0].T.astype(BF16)
    y_t = _ssm(hperm, w_in_t[:e], toep, state_in, state_out, decay)
    gated = _glu(y_t, hperm, a_w_glu[0].T.astype(BF16), a_b_glu[0], w_in_t[e:])

    layer = DEPTH // 2
    lambda_init = 0.8 - 0.6 * math.exp(-0.3 * layer)
    w_b_t = b_w_in[0].T.astype(BF16)
    gsub = jnp.tile(b_g_sub[0] * (1.0 - lambda_init), w_v.shape[1] // V_DIM)
    h, k, v_t, q_t, gz_t = _mid(
        gated, x, mod[0], mod[1], perm, a_w_out[0].astype(BF16), g_post[0], g_kv, g_pre[1],
        w_k.astype(BF16), w_v.T.astype(BF16), w_b_t[:qk], w_b_t[qk:], gsub)

    lam = (jnp.exp(jnp.sum(b_lq1[0] * b_lk1[0])) - jnp.exp(jnp.sum(b_lq2[0] * b_lk2[0]))
           + lambda_init).reshape(1).astype(F32)
    return _attention(lam, q_t, k, v_t, gz_t, h, mod[1], b_w_out[0].astype(BF16), g_post[1])
```

```python
import functools
import math

import jax
import jax.numpy as jnp
from jax import lax
from jax.experimental import pallas as pl
from jax.experimental.pallas import tpu as pltpu

F32 = jnp.float32
BF16 = jnp.bfloat16

EPS = 1e-6
DEPTH = 2
SSM_GROUP = 16
SSM_STATE = 64
CHUNK = 16
GROUPS_PER_STEP = 16
SSM_UNROLL = 8
HEAD_DIM = 64
V_DIM = 2 * HEAD_DIM
V_ROWS = V_DIM + 16
TOKEN_BLOCK = 256
PRENORM_BLOCK = 1024
GLU_BLOCK = 2048
MID_SUB = 4
ATT_AHEAD = 8
ATT_BLOCKS = 3
Q_SCALE = HEAD_DIM ** -0.5 * math.log2(math.e)
VMEM_LIMIT = 48 * 1024 * 1024
ATT_VMEM_LIMIT = 56 * 1024 * 1024


def _params(semantics):
    return pltpu.CompilerParams(dimension_semantics=semantics, vmem_limit_bytes=VMEM_LIMIT)


def _sigmoid(v):
    return 1.0 / (1.0 + jnp.exp(-v))


def _silu(v):
    return v * _sigmoid(v)


def _gelu_tanh(v):
    k = -2.0 * math.sqrt(2.0 / math.pi) * math.log2(math.e)
    return v / (1.0 + jnp.exp2(v * ((k * 0.044715) * (v * v) + k)))


def _rms_rows(v, g):
    return v * lax.rsqrt(jnp.mean(v * v, axis=-1, keepdims=True) + EPS) * g


def _dot(a, b):
    return jnp.dot(a, b, preferred_element_type=F32)


def _dot_nt(a, b):
    return lax.dot_general(a, b, (((1,), (1,)), ((), ())), preferred_element_type=F32)


def _dot_tn(a, b):
    return lax.dot_general(a, b, (((0,), (0,)), ((), ())), preferred_element_type=F32)


def _modulation_kernel(c_ref, w_ref, b_ref, o_ref):
    def split(v):
        hi = v.astype(BF16)
        return hi, (v - hi.astype(F32)).astype(BF16)

    (sh, sl), (wh, wl) = split(_silu(c_ref[...])), split(w_ref[...])
    o_ref[...] = _dot(sh, wh) + _dot(sh, wl) + _dot(sl, wh) + b_ref[...]


def _modulation(c, ada_w, ada_b):
    bsz, d = c.shape
    depth, _, n = ada_w.shape
    tn = 1024
    return pl.pallas_call(
        _modulation_kernel,
        grid=(depth, n // tn),
        in_specs=[
            pl.BlockSpec((bsz, d), lambda l, j: (0, 0)),
            pl.BlockSpec((None, d, tn), lambda l, j: (l, 0, j)),
            pl.BlockSpec((None, 1, tn), lambda l, j: (l, 0, j)),
        ],
        out_specs=pl.BlockSpec((None, bsz, tn), lambda l, j: (l, 0, j)),
        out_shape=jax.ShapeDtypeStruct((depth, bsz, n), F32),
        compiler_params=_params(("parallel", "parallel")),
        name="modulation",
    )(c, ada_w, ada_b.reshape(depth, 1, n))


def _chunk_permutation(n):
    r = jnp.arange(n)
    src = (r % (n // CHUNK)) * CHUNK + r // (n // CHUNK)
    return (src[:, None] == r[None, :]).astype(BF16)


def _prenorm_kernel(x_ref, mod_ref, g_ref, p_ref, o_ref):
    d = x_ref.shape[-1]
    sub = p_ref.shape[0]
    shift = mod_ref[:, 0:d]
    scale = mod_ref[:, d:2 * d]
    for r in range(x_ref.shape[0] // sub):
        x = x_ref[r * sub:(r + 1) * sub, :]
        h = _rms_rows(x, g_ref[...]) * (1.0 + scale) + shift
        hp = _dot(p_ref[...], h.astype(BF16)).astype(BF16)
        o_ref[:, r * (sub // CHUNK):(r + 1) * (sub // CHUNK), :] = hp.reshape(CHUNK, sub // CHUNK, d)


def _operators_kernel(lam_re_ref, lam_im_ref, log_dt_ref, bt_re_ref, bt_im_ref, c_re_ref, c_im_ref,
                      d_ref, toep_ref, sin_ref, sout_ref, decay_ref):
    rows = CHUNK * SSM_GROUP
    lanes = 2 * SSM_STATE
    lam_re, lam_im, dt = lam_re_ref[...], lam_im_ref[...], jnp.exp(log_dt_ref[...])
    ar, ai = lam_re * dt, lam_im * dt

    def apow(k):
        mag = jnp.exp(k * ar)
        return mag * jnp.cos(k * ai), mag * jnp.sin(k * ai)

    pos = lax.broadcasted_iota(jnp.int32, (CHUNK, 1), 0).astype(F32)
    a1r, a1i = apow(jnp.ones((1, 1), F32))
    den = lam_re * lam_re + lam_im * lam_im
    fr = ((a1r - 1.0) * lam_re + a1i * lam_im) / den
    fi = (a1i * lam_re - (a1r - 1.0) * lam_im) / den
    bbr = fr * bt_re_ref[...] - fi * bt_im_ref[...]
    bbi = fr * bt_im_ref[...] + fi * bt_re_ref[...]

    r_idx = lax.broadcasted_iota(jnp.int32, (rows, CHUNK), 0)
    k_idx = lax.broadcasted_iota(jnp.int32, (rows, CHUNK), 1)
    rep = (r_idx // SSM_GROUP == k_idx).astype(BF16)
    tile = (r_idx % SSM_GROUP == k_idx).astype(BF16)
    lane_tile = (lax.broadcasted_iota(jnp.int32, (SSM_GROUP, rows), 1) % SSM_GROUP
                 == lax.broadcasted_iota(jnp.int32, (SSM_GROUP, rows), 0)).astype(BF16)

    def split(v):
        hi = v.astype(BF16)
        return hi, (v - hi.astype(F32)).astype(BF16)

    def expand(sel, v):
        hi, lo = split(v)
        return _dot(sel, hi) + _dot(sel, lo)

    def dot3(x, y):
        (xh, xl), (yh, yl) = split(x), split(y)
        return _dot_nt(xh, yh) + _dot_nt(xh, yl) + _dot_nt(xl, yh)

    def times(xr, xi, yr, yi):
        return xr * yr - xi * yi, xr * yi + xi * yr

    left = lax.broadcasted_iota(jnp.int32, (rows, lanes), 1) < SSM_STATE
    halves = lambda v: (jnp.where(left, v, 0.0), jnp.where(left, 0.0, v))

    qr, qi = times(*apow(CHUNK - 1.0 - pos), fr, fi)
    sr, si = times(expand(rep, qr), expand(rep, qi),
                   expand(tile, bt_re_ref[...]), expand(tile, bt_im_ref[...]))
    cr, ci = expand(tile, c_re_ref[...]), expand(tile, c_im_ref[...])
    wr, wi = apow(pos + 1.0)
    our, oui = times(cr, ci, expand(rep, wr), expand(rep, wi))
    for h, (s_r, s_i, o_r, o_i) in enumerate(zip(halves(sr), halves(si), halves(our), halves(oui))):
        sin_ref[h, 0] = s_r.astype(BF16)
        sin_ref[h, 1] = s_i.astype(BF16)
        sout_ref[h, 0] = o_r.astype(BF16)
        sout_ref[h, 1] = (-o_i).astype(BF16)

    pr, pi = apow(pos)
    lr, li = times(cr, ci, expand(rep, pr), expand(rep, pi))
    lane_blk = lax.broadcasted_iota(jnp.int32, (rows, rows), 1) // SSM_GROUP
    diag = (lax.broadcasted_iota(jnp.int32, (rows, rows), 0)
            == lax.broadcasted_iota(jnp.int32, (rows, rows), 1))
    for h, (l_r, l_i) in enumerate(zip(halves(lr), halves(li))):
        kern = dot3(l_r, bbr) - dot3(l_i, bbi)
        k_hi, k_lo = split(kern)
        wide = _dot(k_hi, lane_tile) + _dot(k_lo, lane_tile)
        toep = jnp.where(diag, d_ref[h], 0.0)
        for p in range(CHUNK):
            n = p * SSM_GROUP
            delayed = wide if p == 0 else jnp.concatenate(
                [jnp.zeros((n, rows), F32), wide[:rows - n]], axis=0)
            toep = toep + jnp.where(lane_blk == p, delayed, 0.0)
        toep_ref[h] = toep.astype(BF16)

    dr, di = apow(jnp.full((1, 1), float(CHUNK), F32))
    sub = lax.broadcasted_iota(jnp.int32, (8, lanes), 0)
    decay_ref[...] = jnp.where(sub == 0, dr, jnp.where(sub == 1, di, 0.0))


def _operators_block_kernel(*refs):
    for k in range(refs[0].shape[0]):
        two = pl.ds(2 * k, 2)
        _operators_kernel(*[r.at[k] for r in refs[:7]], refs[7].at[two], refs[8].at[two],
                          refs[9].at[two], refs[10].at[two], refs[11].at[k])


def _prenorm_operators_kernel(*refs):
    _prenorm_kernel(*refs[:4], refs[12])
    _operators_block_kernel(*refs[4:12], *refs[13:])


def _prenorm_and_operators(x, mod0, g_pre0, perm, lam_re, lam_im, log_dt, b_re, b_im, c_re, c_im,
                           d_skip):
    bsz, seq, d = x.shape
    tb = PRENORM_BLOCK
    nblk = seq // tb
    g, p = lam_re.shape
    cpg = SSM_GROUP
    rows = CHUNK * cpg
    pairs = g // 2
    lanes = 2 * p
    row_pair = lambda v: v.reshape(pairs, 1, lanes)
    mat_pair = lambda m: m.reshape(pairs, 2, cpg, p).transpose(0, 2, 1, 3).reshape(pairs, cpg, lanes)
    op_args = (row_pair(lam_re), row_pair(lam_im),
               row_pair(jnp.broadcast_to(log_dt[:, None], (g, p))),
               mat_pair(b_re.transpose(0, 2, 1)), mat_pair(b_im.transpose(0, 2, 1)),
               mat_pair(c_re), mat_pair(c_im),
               jnp.tile(d_skip.reshape(g, 1, cpg), (1, 1, CHUNK)))
    op_shapes = [
        jax.ShapeDtypeStruct((g, rows, rows), BF16),
        jax.ShapeDtypeStruct((g, 2, rows, lanes), BF16),
        jax.ShapeDtypeStruct((g, 2, rows, lanes), BF16),
        jax.ShapeDtypeStruct((pairs, 8, lanes), F32),
    ]

    def op_specs(per, at):
        vec = pl.BlockSpec((per, 1, lanes), lambda *i: (at(*i), 0, 0))
        mat = pl.BlockSpec((per, cpg, lanes), lambda *i: (at(*i), 0, 0))
        ins = [vec, vec, vec, mat, mat, mat, mat,
               pl.BlockSpec((2 * per, 1, rows), lambda *i: (at(*i), 0, 0))]
        outs = [
            pl.BlockSpec((2 * per, rows, rows), lambda *i: (at(*i), 0, 0)),
            pl.BlockSpec((2 * per, 2, rows, lanes), lambda *i: (at(*i), 0, 0, 0)),
            pl.BlockSpec((2 * per, 2, rows, lanes), lambda *i: (at(*i), 0, 0, 0)),
            pl.BlockSpec((per, 8, lanes), lambda *i: (at(*i), 0, 0)),
        ]
        return ins, outs

    pre_args = (x, mod0.reshape(bsz, 1, 3 * d), g_pre0.reshape(1, d), perm)
    pre_specs = [
        pl.BlockSpec((None, tb, d), lambda b, j: (b, j, 0)),
        pl.BlockSpec((None, 1, 3 * d), lambda b, j: (b, 0, 0)),
        pl.BlockSpec((1, d), lambda b, j: (0, 0)),
        pl.BlockSpec(perm.shape, lambda b, j: (0, 0)),
    ]
    pre_out = pl.BlockSpec((None, CHUNK, tb // CHUNK, d), lambda b, j: (b, 0, j, 0))
    pre_shape = jax.ShapeDtypeStruct((bsz, CHUNK, seq // CHUNK, d), BF16)

    steps = bsz * nblk
    if pairs % steps == 0:
        ins, outs = op_specs(pairs // steps, lambda b, j: b * nblk + j)
        return pl.pallas_call(
            _prenorm_operators_kernel,
            grid=(bsz, nblk),
            in_specs=pre_specs + ins,
            out_specs=[pre_out] + outs,
            out_shape=[pre_shape] + op_shapes,
            compiler_params=_params(("parallel", "parallel")),
            name="prenorm_operators",
        )(*pre_args, *op_args)
    hperm = pl.pallas_call(
        _prenorm_kernel, grid=(bsz, nblk), in_specs=pre_specs, out_specs=pre_out,
        out_shape=pre_shape, compiler_params=_params(("parallel", "parallel")), name="prenorm",
    )(*pre_args)
    ins, outs = op_specs(1, lambda q: q)
    return (hperm, *pl.pallas_call(
        _operators_block_kernel, grid=(pairs,), in_specs=ins, out_specs=outs, out_shape=op_shapes,
        compiler_params=_params(("parallel",)), name="operators",
    )(*op_args))


def _ssm_kernel(h_ref, wu_ref, toep_ref, sin_ref, sout_ref, decay_ref, y_ref, xs_ref):
    n_chunks = y_ref.shape[-1] // CHUNK
    n_state = SSM_STATE
    j = pl.program_id(1)
    last = pl.num_programs(1) - 1
    fill, drain = j % 2, (j + 1) % 2

    def project(p):
        r0 = pl.multiple_of(p * n_chunks, n_chunks)
        u = _dot_nt(wu_ref[...], h_ref[pl.ds(r0, n_chunks), :]).astype(BF16)
        c0 = pl.multiple_of(p * SSM_GROUP, SSM_GROUP)
        for g in range(GROUPS_PER_STEP):
            xs_ref[fill, g, pl.ds(c0, SSM_GROUP), :] = u[g * SSM_GROUP:(g + 1) * SSM_GROUP, :]

    row = lax.broadcasted_iota(jnp.int32, (n_chunks, 2 * n_state), 0)

    def shift_rows(v, s):
        if s % 8 == 0:
            return jnp.concatenate([jnp.zeros((s, v.shape[1]), v.dtype), v[:-s]], axis=0)
        return jnp.where(row >= s, pltpu.roll(v, s, 0), 0.0)

    def gains(q):
        g0, g1 = 2 * q, 2 * q + 1
        x0, x1 = xs_ref[drain, g0], xs_ref[drain, g1]
        inc_r = _dot_tn(x0, sin_ref[g0, 0]) + _dot_tn(x1, sin_ref[g1, 0])
        inc_i = _dot_tn(x0, sin_ref[g0, 1]) + _dot_tn(x1, sin_ref[g1, 1])
        return inc_r, inc_i

    def finish(q, inc_r, inc_i):
        g0, g1 = 2 * q, 2 * q + 1
        er, ei = shift_rows(inc_r, 1), shift_rows(inc_i, 1)
        ar, ai = decay_ref[q, 0:1, :], decay_ref[q, 1:2, :]
        s = 1
        while s < n_chunks:
            if s % 8 == 0:
                dr = ar * er[:-s] - ai * ei[:-s]
                di = ar * ei[:-s] + ai * er[:-s]
                er = jnp.concatenate([er[:s], er[s:] + dr], axis=0)
                ei = jnp.concatenate([ei[:s], ei[s:] + di], axis=0)
            else:
                sr, si = shift_rows(er, s), shift_rows(ei, s)
                er, ei = er + (ar * sr - ai * si), ei + (ar * si + ai * sr)
            ar, ai = ar * ar - ai * ai, 2.0 * (ar * ai)
            s *= 2
        sr, si = er.astype(BF16), ei.astype(BF16)
        for g in (g0, g1):
            y = (_dot(toep_ref[g], xs_ref[drain, g]) + _dot_nt(sout_ref[g, 0], sr)
                 + _dot_nt(sout_ref[g, 1], si))
            act = _gelu_tanh(y).astype(BF16)
            row0 = pl.multiple_of(g * SSM_GROUP, SSM_GROUP)
            for p in range(CHUNK):
                y_ref[pl.ds(row0, SSM_GROUP), p * n_chunks:(p + 1) * n_chunks] = (
                    act[p * SSM_GROUP:(p + 1) * SSM_GROUP, :])

    n_iter = GROUPS_PER_STEP // 2 // SSM_UNROLL
    per_iter = CHUNK // n_iter

    def body(it, carry, with_scan, with_projection):
        qs = [it * SSM_UNROLL + u for u in range(SSM_UNROLL)]
        started = [gains(q) for q in qs] if with_scan else []
        if with_projection:
            for pp in range(per_iter):
                project(it * per_iter + pp)
        for q, inc in zip(qs, started):
            finish(q, *inc)
        return carry

    @pl.when(j == 0)
    def _():
        lax.fori_loop(0, n_iter, functools.partial(body, with_scan=False, with_projection=True), 0)

    @pl.when(jnp.logical_and(j > 0, j < last))
    def _():
        lax.fori_loop(0, n_iter, functools.partial(body, with_scan=True, with_projection=True), 0)

    @pl.when(j == last)
    def _():
        lax.fori_loop(0, n_iter, functools.partial(body, with_scan=True, with_projection=False), 0)


def _ssm(hperm, wu_t, toep, state_in, state_out, decay):
    bsz, seq, d = hperm.shape
    e = wu_t.shape[0]
    cb = GROUPS_PER_STEP * SSM_GROUP
    rows = CHUNK * SSM_GROUP
    n_chunks = seq // CHUNK
    gps = GROUPS_PER_STEP
    nblk = e // cb
    assert (gps // 2) % SSM_UNROLL == 0 and CHUNK % (gps // 2 // SSM_UNROLL) == 0
    proj = lambda j: jnp.minimum(j, nblk - 1)
    scan = lambda j: jnp.maximum(j - 1, 0)
    return pl.pallas_call(
        _ssm_kernel,
        grid=(bsz, nblk + 1),
        in_specs=[
            pl.BlockSpec((None, seq, d), lambda b, j: (b, 0, 0)),
            pl.BlockSpec((cb, d), lambda b, j: (proj(j), 0)),
            pl.BlockSpec((gps, rows, rows), lambda b, j: (scan(j), 0, 0)),
            pl.BlockSpec((gps, 2, rows, 2 * SSM_STATE), lambda b, j: (scan(j), 0, 0, 0)),
            pl.BlockSpec((gps, 2, rows, 2 * SSM_STATE), lambda b, j: (scan(j), 0, 0, 0)),
            pl.BlockSpec((gps // 2, 8, 2 * SSM_STATE), lambda b, j: (scan(j), 0, 0)),
        ],
        out_specs=pl.BlockSpec((None, cb, seq), lambda b, j: (b, scan(j), 0)),
        out_shape=jax.ShapeDtypeStruct((bsz, e, seq), BF16),
        scratch_shapes=[pltpu.VMEM((2, gps, rows, n_chunks), BF16)],
        compiler_params=_params(("parallel", "arbitrary")),
        name="ssm",
    )(hperm, wu_t, toep, state_in, state_out, decay)


def _glu_kernel(y_ref, h_ref, wg_ref, bg_ref, wz_ref, o_ref):
    e = y_ref.shape[0]
    rb = 256
    ya = y_ref[...]
    hb = h_ref[...]
    for r in range(e // rb):
        rows = slice(r * rb, (r + 1) * rb)
        gl = _dot(wg_ref[rows, :], ya) + bg_ref[rows, :]
        z = _dot_nt(wz_ref[rows, :], hb)
        yr = y_ref[rows, :].astype(F32)
        gated = yr * _sigmoid(gl) * _silu(z)
        o_ref[:, rows] = gated.T.astype(BF16)


def _glu(y_t, hperm, wglu_t, b_glu, wz_t):
    bsz, e, seq = y_t.shape
    d = hperm.shape[-1]
    tn = GLU_BLOCK
    return pl.pallas_call(
        _glu_kernel,
        grid=(bsz, seq // tn),
        in_specs=[
            pl.BlockSpec((None, e, tn), lambda b, j: (b, 0, j)),
            pl.BlockSpec((None, tn, d), lambda b, j: (b, j, 0)),
            pl.BlockSpec((e, e), lambda b, j: (0, 0)),
            pl.BlockSpec((e, 1), lambda b, j: (0, 0)),
            pl.BlockSpec((e, d), lambda b, j: (0, 0)),
        ],
        out_specs=pl.BlockSpec((None, tn, e), lambda b, j: (b, j, 0)),
        out_shape=jax.ShapeDtypeStruct((bsz, seq, e), BF16),
        compiler_params=_params(("parallel", "parallel")),
        name="glu",
    )(y_t, hperm, wglu_t, b_glu.reshape(e, 1), wz_t)


def _mid_kernel(gp_ref, x_ref, mod0_ref, mod1_ref, p_ref, wo_ref, gpost_ref, gkv_ref, gpre_ref,
                wk_ref, wv_ref, wq_ref, wz_ref, gsub_ref, h_ref, k_ref, vt_ref, qt_ref, gz_ref):
    d = x_ref.shape[-1]
    tb = p_ref.shape[0]
    subs = range(x_ref.shape[0] // tb)
    cps = tb // CHUNK
    gate0 = mod0_ref[:, 2 * d:3 * d]
    shift1 = mod1_ref[:, 0:d]
    scale1 = mod1_ref[:, d:2 * d]
    ones_row = (lax.broadcasted_iota(jnp.int32, (V_ROWS - V_DIM, tb), 0) == 0).astype(BF16)

    ys = []
    for s in subs:
        gp = gp_ref[:, s * cps:(s + 1) * cps, :].reshape(tb, -1)
        ys.append(_dot(_dot(p_ref[...], gp).astype(BF16), wo_ref[...]))
    ins = []
    for s, y in zip(subs, ys):
        rows = slice(s * tb, (s + 1) * tb)
        h = x_ref[rows, :] + gate0 * _rms_rows(y, gpost_ref[...])
        h_ref[rows, :] = h
        kv_in = _rms_rows(h, gkv_ref[...]).astype(BF16)
        h_in = (_rms_rows(h, gpre_ref[...]) * (1.0 + scale1) + shift1).astype(BF16)
        ins.append((kv_in, h_in))
    for s, (kv_in, h_in) in zip(subs, ins):
        k_ref[s * tb:(s + 1) * tb, :] = _dot(kv_in, wk_ref[...]).astype(BF16)
        vt = _dot_nt(wv_ref[...], kv_in).astype(BF16)
        for hd in range(vt.shape[0] // V_DIM):
            vt_ref[s, hd * V_ROWS:hd * V_ROWS + V_DIM, :] = vt[hd * V_DIM:(hd + 1) * V_DIM, :]
            vt_ref[s, hd * V_ROWS + V_DIM:(hd + 1) * V_ROWS, :] = ones_row
        qt_ref[s] = (_dot_nt(wq_ref[...], h_in) * Q_SCALE).astype(BF16)
        gz_ref[s] = (_silu(_dot_nt(wz_ref[...], h_in)) * gsub_ref[...]).astype(BF16)


def _mid(gated_perm, x, mod0, mod1, perm, wo, g_post0, g_kv, g_pre1, wk, wv_t, wq_t, wz_t, gsub):
    bsz, seq, d = x.shape
    tb = TOKEN_BLOCK
    nb = seq // tb
    sub = MID_SUB
    e = gated_perm.shape[-1]
    qk = wk.shape[1]
    av = wv_t.shape[0]
    row = lambda b, j: (b, j, 0)
    const = lambda shape: pl.BlockSpec(shape, lambda b, j: (0, 0), pipeline_mode=pl.Buffered(1))
    t_spec = lambda n: pl.BlockSpec((None, sub, n, tb), lambda b, j: (b, j, 0, 0))
    return pl.pallas_call(
        _mid_kernel,
        grid=(bsz, nb // sub),
        in_specs=[
            pl.BlockSpec((None, CHUNK, sub * tb // CHUNK, e), lambda b, j: (b, 0, j, 0)),
            pl.BlockSpec((None, sub * tb, d), row),
            pl.BlockSpec((None, 1, 3 * d), lambda b, j: (b, 0, 0)),
            pl.BlockSpec((None, 1, 3 * d), lambda b, j: (b, 0, 0)),
            const((tb, tb)),
            const((e, d)),
            const((1, d)),
            const((1, d)),
            const((1, d)),
            const((d, qk)),
            const((av, d)),
            const((qk, d)),
            const((av, d)),
            const((av, 1)),
        ],
        out_specs=[
            pl.BlockSpec((None, sub * tb, d), row),
            pl.BlockSpec((None, sub * tb, qk), row),
            t_spec(av // V_DIM * V_ROWS),
            t_spec(qk),
            t_spec(av),
        ],
        out_shape=[
            jax.ShapeDtypeStruct((bsz, seq, d), F32),
            jax.ShapeDtypeStruct((bsz, seq, qk), BF16),
            jax.ShapeDtypeStruct((bsz, nb, av // V_DIM * V_ROWS, tb), BF16),
            jax.ShapeDtypeStruct((bsz, nb, qk, tb), BF16),
            jax.ShapeDtypeStruct((bsz, nb, av, tb), BF16),
        ],
        compiler_params=_params(("parallel", "parallel")),
        name="mid",
    )(gated_perm.reshape(bsz, CHUNK, seq // CHUNK, e), x, mod0.reshape(bsz, 1, 3 * d),
      mod1.reshape(bsz, 1, 3 * d), perm, wo, g_post0.reshape(1, d), g_kv.reshape(1, d),
      g_pre1.reshape(1, d), wk, wv_t, wq_t, wz_t, gsub.reshape(av, 1))


def _attention_kernel(lam_ref, qt_ref, qn_ref, k_ref, vt_ref, gz_ref, bias_ref, h_ref, mod1_ref, wo_ref,
                      gpost_ref, o_ref, qp_ref, m_ref, acc_ref, s_ref, og_ref):
    tq = qt_ref.shape[-1]
    hw = 2 * HEAD_DIM
    heads = qt_ref.shape[0] // hw
    i = pl.program_id(1)

    zero = jnp.zeros((HEAD_DIM, tq), qt_ref.dtype)

    def pad_queries(q_ref, n_heads):
        for h in range(n_heads):
            qt = q_ref[h * hw:(h + 1) * hw, :]
            qp_ref[2 * h] = jnp.concatenate([qt[:HEAD_DIM], zero], axis=0)
            qp_ref[2 * h + 1] = jnp.concatenate([zero, qt[HEAD_DIM:]], axis=0)

    pad_queries(qt_ref, heads)
    n_strips = 2 * heads

    def scores(j, n):
        h = n // 2
        row0 = pl.multiple_of(j * tq, tq)
        return _dot(k_ref[pl.ds(row0, tq), h * hw:(h + 1) * hw], qp_ref[n])

    def step(j, last=False):
        for n in range(n_strips):
            s = s_ref[n % ATT_AHEAD]
            if n + ATT_AHEAD < n_strips:
                s_ref[n % ATT_AHEAD] = scores(j, n + ATT_AHEAD)
            elif not last:
                s_ref[n % ATT_AHEAD] = scores(j + 1, n + ATT_AHEAD - n_strips)
            m_old = m_ref[n]
            if last:
                hq = tq // 2
                s00 = s[:hq, :hq] + bias_ref[...]
                s01 = s[:hq, hq:]
                s11 = s[hq:, hq:] + bias_ref[...]
                m_blk = jnp.concatenate(
                    [jnp.max(s00, axis=0, keepdims=True),
                     jnp.maximum(jnp.max(s01, axis=0, keepdims=True),
                                 jnp.max(s11, axis=0, keepdims=True))], axis=1)
                m_new = jnp.maximum(m_old, m_blk)
                e = lambda v, m: jnp.exp2((v - m).astype(BF16))
                p = jnp.concatenate(
                    [jnp.concatenate([e(s00, m_new[:, :hq]), e(s01, m_new[:, hq:])], axis=1),
                     jnp.concatenate([jnp.zeros((hq, hq), BF16), e(s11, m_new[:, hq:])], axis=1)],
                    axis=0)
            else:
                m_new = jnp.maximum(m_old, jnp.max(s, axis=0, keepdims=True))
                p = jnp.exp2((s - m_new).astype(BF16))
            alpha = jnp.exp2(m_old - m_new)
            m_ref[n] = m_new
            h = n // 2
            vt = vt_ref[j, h * V_ROWS:(h + 1) * V_ROWS, :]
            acc_ref[n] = alpha * acc_ref[n] + _dot(vt, p)

    @pl.when(i == 0)
    def _():
        for n in range(ATT_AHEAD):
            s_ref[n] = scores(0, n)

    m_ref[...] = jnp.full(m_ref.shape, -jnp.inf, F32)
    acc_ref[...] = jnp.zeros(acc_ref.shape, F32)

    @pl.when(i == 0)
    def _():
        step(0, last=True)

    @pl.when(i > 0)
    def _():
        first = (i - 1) % ATT_BLOCKS
        for lead in range(1, ATT_BLOCKS):
            @pl.when(first == lead)
            def _():
                for j in range(lead):
                    step(j)

        def grouped_steps(t, carry):
            for u in range(ATT_BLOCKS):
                step(first + ATT_BLOCKS * t + u)
            return carry

        lax.fori_loop(0, (i - 1) // ATT_BLOCKS, grouped_steps, 0)
        step(i - 1)
        step(i, last=True)

    pad_queries(qn_ref, ATT_AHEAD // 2)
    for n in range(ATT_AHEAD):
        s_ref[n] = scores(0, n)

    for h in range(heads):
        r0 = 1.0 / acc_ref[2 * h, V_DIM:V_DIM + 1, :]
        r1 = lam_ref[0] / acc_ref[2 * h + 1, V_DIM:V_DIM + 1, :]
        o = acc_ref[2 * h, :V_DIM, :] * r0 - acc_ref[2 * h + 1, :V_DIM, :] * r1
        rows = slice(h * V_DIM, (h + 1) * V_DIM)
        inv_rms = lax.rsqrt(jnp.mean(o * o, axis=0, keepdims=True) + EPS)
        og_ref[rows, :] = (o * inv_rms * gz_ref[rows, :].astype(F32)).astype(BF16)
        if h % 2 == 1:
            pr = slice((h - 1) * V_DIM, (h + 1) * V_DIM)
            part = _dot_tn(og_ref[pr, :], wo_ref[pr, :])
            y = part if h == 1 else y + part
    d = h_ref.shape[-1]
    gate1 = mod1_ref[:, 2 * d:3 * d]
    o_ref[...] = h_ref[...] + gate1 * _rms_rows(y, gpost_ref[...])


def _attention(lam, q_t, k, v_t, gz_t, h, mod1, wo, g_post1):
    bsz, nb, width, tb = q_t.shape
    seq = nb * tb
    d = h.shape[-1]
    heads = width // (2 * HEAD_DIM)
    assert heads % 2 == 0 and (2 * heads) % ATT_AHEAD == 0
    pos = jnp.arange(tb // 2)
    bias = jnp.where(pos[:, None] <= pos[None, :], 0.0, -jnp.inf).astype(F32)
    blk = pl.BlockSpec((None, None, width, tb), lambda b, i: (b, i, 0, 0))
    once = pl.Buffered(1)
    return pl.pallas_call(
        _attention_kernel,
        grid=(bsz, nb),
        in_specs=[
            pl.BlockSpec(memory_space=pltpu.SMEM),
            blk,
            pl.BlockSpec((None, None, width, tb), lambda b, i: (b, jnp.minimum(i + 1, nb - 1), 0, 0)),
            pl.BlockSpec((None, seq, width), lambda b, i: (b, 0, 0)),
            pl.BlockSpec((None, nb, heads * V_ROWS, tb), lambda b, i: (b, 0, 0, 0)),
            blk,
            pl.BlockSpec((tb // 2, tb // 2), lambda b, i: (0, 0), pipeline_mode=once),
            pl.BlockSpec((None, tb, d), lambda b, i: (b, i, 0)),
            pl.BlockSpec((None, 1, 3 * d), lambda b, i: (b, 0, 0)),
            pl.BlockSpec((width, d), lambda b, i: (0, 0), pipeline_mode=once),
            pl.BlockSpec((1, d), lambda b, i: (0, 0), pipeline_mode=once),
        ],
        out_specs=pl.BlockSpec((None, tb, d), lambda b, i: (b, i, 0)),
        out_shape=jax.ShapeDtypeStruct((bsz, seq, d), F32),
        scratch_shapes=[
            pltpu.VMEM((2 * heads, 2 * HEAD_DIM, tb), BF16),
            pltpu.VMEM((2 * heads, 1, tb), F32),
            pltpu.VMEM((2 * heads, V_ROWS, tb), F32),
            pltpu.VMEM((ATT_AHEAD, tb, tb), F32),
            pltpu.VMEM((width, tb), BF16),
        ],
        compiler_params=pltpu.CompilerParams(dimension_semantics=("parallel", "arbitrary"),
                                             vmem_limit_bytes=ATT_VMEM_LIMIT),
        name="attention",
    )(lam, q_t, q_t, k, v_t, gz_t, bias, h, mod1.reshape(bsz, 1, 3 * d), wo, g_post1.reshape(1, d))


def kernel(x, c, ada_w, ada_b, g_pre, g_post, a_w_in, a_lam_re, a_lam_im, a_log_dt, a_b_re, a_b_im,
           a_c_re, a_c_im, a_d, a_w_glu, a_b_glu, a_w_out, g_kv, w_k, w_v, b_w_in, b_lq1, b_lk1,
           b_lq2, b_lk2, b_g_sub, b_w_out):
    bsz, seq, d = x.shape
    e = a_w_glu.shape[1]
    qk = w_k.shape[1]
    assert seq % (CHUNK * 128) == 0 and seq % GLU_BLOCK == 0 and d % 128 == 0
    assert e % (GROUPS_PER_STEP * SSM_GROUP) == 0

    mod = _modulation(c, ada_w, ada_b)
    perm = _chunk_permutation(TOKEN_BLOCK)

    hperm, toep, state_in, state_out, decay = _prenorm_and_operators(
        x, mod[0], g_pre[0], perm,
        a_lam_re[0], a_lam_im[0], a_log_dt[0], a_b_re[0], a_b_im[0], a_c_re[0], a_c_im[0], a_d[0])
    hperm = hperm.reshape(bsz, seq, d)
    w_in_t = a_w_in[0].T.astype(BF16)
    y_t = _ssm(hperm, w_in_t[:e], toep, state_in, state_out, decay)
    gated = _glu(y_t, hperm, a_w_glu[0].T.astype(BF16), a_b_glu[0], w_in_t[e:])

    layer = DEPTH // 2
    lambda_init = 0.8 - 0.6 * math.exp(-0.3 * layer)
    w_b_t = b_w_in[0].T.astype(BF16)
    gsub = jnp.tile(b_g_sub[0] * (1.0 - lambda_init), w_v.shape[1] // V_DIM)
    h, k, v_t, q_t, gz_t = _mid(
        gated, x, mod[0], mod[1], perm, a_w_out[0].astype(BF16), g_post[0], g_kv, g_pre[1],
        w_k.astype(BF16), w_v.T.astype(BF16), w_b_t[:qk], w_b_t[qk:], gsub)

    lam = (jnp.exp(jnp.sum(b_lq1[0] * b_lk1[0])) - jnp.exp(jnp.sum(b_lq2[0] * b_lk2[0]))
           + lambda_init).reshape(1).astype(F32)
    return _attention(lam, q_t, k, v_t, gz_t, h, mod[1], b_w_out[0].astype(BF16), g_post[1])
```

```python
import functools
import math

import jax
import jax.numpy as jnp
from jax import lax
from jax.experimental import pallas as pl
from jax.experimental.pallas import tpu as pltpu

F32 = jnp.float32
BF16 = jnp.bfloat16

EPS = 1e-6
DEPTH = 2
SSM_GROUP = 16
SSM_STATE = 64
CHUNK = 16
GROUPS_PER_STEP = 16
SSM_UNROLL = 8
HEAD_DIM = 64
V_DIM = 2 * HEAD_DIM
V_ROWS = V_DIM + 16
TOKEN_BLOCK = 256
PRENORM_BLOCK = 1024
X_SLOTS = 3
GLU_BLOCK = 2048
MID_SUB = 4
ATT_AHEAD = 8
ATT_BLOCKS = 3
Q_SCALE = HEAD_DIM ** -0.5 * math.log2(math.e)
VMEM_LIMIT = 48 * 1024 * 1024
ATT_VMEM_LIMIT = 56 * 1024 * 1024


def _params(semantics):
    return pltpu.CompilerParams(dimension_semantics=semantics, vmem_limit_bytes=VMEM_LIMIT)


def _sigmoid(v):
    return 1.0 / (1.0 + jnp.exp(-v))


def _silu(v):
    return v * _sigmoid(v)


def _gelu_tanh(v):
    k = -2.0 * math.sqrt(2.0 / math.pi) * math.log2(math.e)
    return v / (1.0 + jnp.exp2(v * ((k * 0.044715) * (v * v) + k)))


def _rms_rows(v, g):
    return v * lax.rsqrt(jnp.mean(v * v, axis=-1, keepdims=True) + EPS) * g


def _dot(a, b):
    return jnp.dot(a, b, preferred_element_type=F32)


def _dot_nt(a, b):
    return lax.dot_general(a, b, (((1,), (1,)), ((), ())), preferred_element_type=F32)


def _dot_tn(a, b):
    return lax.dot_general(a, b, (((0,), (0,)), ((), ())), preferred_element_type=F32)


def _modulation_kernel(c_ref, w_ref, b_ref, o_ref):
    def split(v):
        hi = v.astype(BF16)
        return hi, (v - hi.astype(F32)).astype(BF16)

    (sh, sl), (wh, wl) = split(_silu(c_ref[...])), split(w_ref[...])
    o_ref[...] = _dot(sh, wh) + _dot(sh, wl) + _dot(sl, wh) + b_ref[...]


def _modulation(c, ada_w, ada_b):
    bsz, d = c.shape
    depth, _, n = ada_w.shape
    tn = 1024
    return pl.pallas_call(
        _modulation_kernel,
        grid=(depth, n // tn),
        in_specs=[
            pl.BlockSpec((bsz, d), lambda l, j: (0, 0)),
            pl.BlockSpec((None, d, tn), lambda l, j: (l, 0, j)),
            pl.BlockSpec((None, 1, tn), lambda l, j: (l, 0, j)),
        ],
        out_specs=pl.BlockSpec((None, bsz, tn), lambda l, j: (l, 0, j)),
        out_shape=jax.ShapeDtypeStruct((depth, bsz, n), F32),
        compiler_params=_params(("parallel", "parallel")),
        name="modulation",
    )(c, ada_w, ada_b.reshape(depth, 1, n))


def _chunk_permutation(n):
    r = jnp.arange(n)
    src = (r % (n // CHUNK)) * CHUNK + r // (n // CHUNK)
    return (src[:, None] == r[None, :]).astype(BF16)


def _prenorm_kernel(x_ref, mod_ref, g_ref, p_ref, o_ref):
    d = x_ref.shape[-1]
    sub = p_ref.shape[0]
    shift = mod_ref[:, 0:d]
    scale = mod_ref[:, d:2 * d]
    for r in range(x_ref.shape[0] // sub):
        x = x_ref[r * sub:(r + 1) * sub, :]
        h = _rms_rows(x, g_ref[...]) * (1.0 + scale) + shift
        hp = _dot(p_ref[...], h.astype(BF16)).astype(BF16)
        o_ref[:, r * (sub // CHUNK):(r + 1) * (sub // CHUNK), :] = hp.reshape(CHUNK, sub // CHUNK, d)


def _operators_kernel(lam_re_ref, lam_im_ref, log_dt_ref, bt_re_ref, bt_im_ref, c_re_ref, c_im_ref,
                      d_ref, toep_ref, sin_ref, sout_ref, decay_ref):
    rows = CHUNK * SSM_GROUP
    lanes = 2 * SSM_STATE
    lam_re, lam_im, dt = lam_re_ref[...], lam_im_ref[...], jnp.exp(log_dt_ref[...])
    ar, ai = lam_re * dt, lam_im * dt

    def apow(k):
        mag = jnp.exp(k * ar)
        return mag * jnp.cos(k * ai), mag * jnp.sin(k * ai)

    pos = lax.broadcasted_iota(jnp.int32, (CHUNK, 1), 0).astype(F32)
    a1r, a1i = apow(jnp.ones((1, 1), F32))
    den = lam_re * lam_re + lam_im * lam_im
    fr = ((a1r - 1.0) * lam_re + a1i * lam_im) / den
    fi = (a1i * lam_re - (a1r - 1.0) * lam_im) / den
    bbr = fr * bt_re_ref[...] - fi * bt_im_ref[...]
    bbi = fr * bt_im_ref[...] + fi * bt_re_ref[...]

    r_idx = lax.broadcasted_iota(jnp.int32, (rows, CHUNK), 0)
    k_idx = lax.broadcasted_iota(jnp.int32, (rows, CHUNK), 1)
    rep = (r_idx // SSM_GROUP == k_idx).astype(BF16)
    tile = (r_idx % SSM_GROUP == k_idx).astype(BF16)
    lane_tile = (lax.broadcasted_iota(jnp.int32, (SSM_GROUP, rows), 1) % SSM_GROUP
                 == lax.broadcasted_iota(jnp.int32, (SSM_GROUP, rows), 0)).astype(BF16)

    def split(v):
        hi = v.astype(BF16)
        return hi, (v - hi.astype(F32)).astype(BF16)

    def expand(sel, v):
        hi, lo = split(v)
        return _dot(sel, hi) + _dot(sel, lo)

    def dot3(x, y):
        (xh, xl), (yh, yl) = split(x), split(y)
        return _dot_nt(xh, yh) + _dot_nt(xh, yl) + _dot_nt(xl, yh)

    def times(xr, xi, yr, yi):
        return xr * yr - xi * yi, xr * yi + xi * yr

    left = lax.broadcasted_iota(jnp.int32, (rows, lanes), 1) < SSM_STATE
    halves = lambda v: (jnp.where(left, v, 0.0), jnp.where(left, 0.0, v))

    qr, qi = times(*apow(CHUNK - 1.0 - pos), fr, fi)
    sr, si = times(expand(rep, qr), expand(rep, qi),
                   expand(tile, bt_re_ref[...]), expand(tile, bt_im_ref[...]))
    cr, ci = expand(tile, c_re_ref[...]), expand(tile, c_im_ref[...])
    wr, wi = apow(pos + 1.0)
    our, oui = times(cr, ci, expand(rep, wr), expand(rep, wi))
    for h, (s_r, s_i, o_r, o_i) in enumerate(zip(halves(sr), halves(si), halves(our), halves(oui))):
        sin_ref[h, 0] = s_r.astype(BF16)
        sin_ref[h, 1] = s_i.astype(BF16)
        sout_ref[h, 0] = o_r.astype(BF16)
        sout_ref[h, 1] = (-o_i).astype(BF16)

    pr, pi = apow(pos)
    lr, li = times(cr, ci, expand(rep, pr), expand(rep, pi))
    lane_blk = lax.broadcasted_iota(jnp.int32, (rows, rows), 1) // SSM_GROUP
    diag = (lax.broadcasted_iota(jnp.int32, (rows, rows), 0)
            == lax.broadcasted_iota(jnp.int32, (rows, rows), 1))
    for h, (l_r, l_i) in enumerate(zip(halves(lr), halves(li))):
        kern = dot3(l_r, bbr) - dot3(l_i, bbi)
        k_hi, k_lo = split(kern)
        wide = _dot(k_hi, lane_tile) + _dot(k_lo, lane_tile)
        toep = jnp.where(diag, d_ref[h], 0.0)
        for p in range(CHUNK):
            n = p * SSM_GROUP
            delayed = wide if p == 0 else jnp.concatenate(
                [jnp.zeros((n, rows), F32), wide[:rows - n]], axis=0)
            toep = toep + jnp.where(lane_blk == p, delayed, 0.0)
        toep_ref[h] = toep.astype(BF16)

    dr, di = apow(jnp.full((1, 1), float(CHUNK), F32))
    sub = lax.broadcasted_iota(jnp.int32, (8, lanes), 0)
    decay_ref[...] = jnp.where(sub == 0, dr, jnp.where(sub == 1, di, 0.0))


def _operators_block_kernel(*refs):
    for k in range(refs[0].shape[0]):
        two = pl.ds(2 * k, 2)
        _operators_kernel(*[r.at[k] for r in refs[:7]], refs[7].at[two], refs[8].at[two],
                          refs[9].at[two], refs[10].at[two], refs[11].at[k])


def _prenorm_operators_kernel(*refs, nblk, nsteps):
    x_hbm, xbuf, sem = refs[0], refs[17], refs[18]
    tb = xbuf.shape[1]
    s = pl.program_id(0) * nblk + pl.program_id(1)

    def fetch(step):
        row0 = pl.multiple_of((step % nblk) * tb, tb)
        slot = step % X_SLOTS
        return pltpu.make_async_copy(x_hbm.at[step // nblk, pl.ds(row0, tb), :], xbuf.at[slot],
                                     sem.at[slot])

    @pl.when(s == 0)
    def _():
        for first in range(X_SLOTS - 1):
            fetch(first).start()

    @pl.when(s + X_SLOTS - 1 < nsteps)
    def _():
        fetch(s + X_SLOTS - 1).start()

    fetch(s).wait()
    _prenorm_kernel(xbuf.at[s % X_SLOTS], *refs[1:4], refs[12])
    _operators_block_kernel(*refs[4:12], *refs[13:17])


def _prenorm_and_operators(x, mod0, g_pre0, perm, lam_re, lam_im, log_dt, b_re, b_im, c_re, c_im,
                           d_skip):
    bsz, seq, d = x.shape
    tb = PRENORM_BLOCK
    nblk = seq // tb
    g, p = lam_re.shape
    cpg = SSM_GROUP
    rows = CHUNK * cpg
    pairs = g // 2
    lanes = 2 * p
    row_pair = lambda v: v.reshape(pairs, 1, lanes)
    mat_pair = lambda m: m.reshape(pairs, 2, cpg, p).transpose(0, 2, 1, 3).reshape(pairs, cpg, lanes)
    op_args = (row_pair(lam_re), row_pair(lam_im),
               row_pair(jnp.broadcast_to(log_dt[:, None], (g, p))),
               mat_pair(b_re.transpose(0, 2, 1)), mat_pair(b_im.transpose(0, 2, 1)),
               mat_pair(c_re), mat_pair(c_im),
               jnp.tile(d_skip.reshape(g, 1, cpg), (1, 1, CHUNK)))
    op_shapes = [
        jax.ShapeDtypeStruct((g, rows, rows), BF16),
        jax.ShapeDtypeStruct((g, 2, rows, lanes), BF16),
        jax.ShapeDtypeStruct((g, 2, rows, lanes), BF16),
        jax.ShapeDtypeStruct((pairs, 8, lanes), F32),
    ]

    def op_specs(per, at):
        vec = pl.BlockSpec((per, 1, lanes), lambda *i: (at(*i), 0, 0))
        mat = pl.BlockSpec((per, cpg, lanes), lambda *i: (at(*i), 0, 0))
        ins = [vec, vec, vec, mat, mat, mat, mat,
               pl.BlockSpec((2 * per, 1, rows), lambda *i: (at(*i), 0, 0))]
        outs = [
            pl.BlockSpec((2 * per, rows, rows), lambda *i: (at(*i), 0, 0)),
            pl.BlockSpec((2 * per, 2, rows, lanes), lambda *i: (at(*i), 0, 0, 0)),
            pl.BlockSpec((2 * per, 2, rows, lanes), lambda *i: (at(*i), 0, 0, 0)),
            pl.BlockSpec((per, 8, lanes), lambda *i: (at(*i), 0, 0)),
        ]
        return ins, outs

    pre_args = (x, mod0.reshape(bsz, 1, 3 * d), g_pre0.reshape(1, d), perm)
    pre_specs = [
        pl.BlockSpec((None, tb, d), lambda b, j: (b, j, 0)),
        pl.BlockSpec((None, 1, 3 * d), lambda b, j: (b, 0, 0)),
        pl.BlockSpec((1, d), lambda b, j: (0, 0)),
        pl.BlockSpec(perm.shape, lambda b, j: (0, 0)),
    ]
    pre_out = pl.BlockSpec((None, CHUNK, tb // CHUNK, d), lambda b, j: (b, 0, j, 0))
    pre_shape = jax.ShapeDtypeStruct((bsz, CHUNK, seq // CHUNK, d), BF16)

    steps = bsz * nblk
    if pairs % steps == 0:
        ins, outs = op_specs(pairs // steps, lambda b, j: b * nblk + j)
        assert steps >= X_SLOTS
        return pl.pallas_call(
            functools.partial(_prenorm_operators_kernel, nblk=nblk, nsteps=steps),
            grid=(bsz, nblk),
            in_specs=[pl.BlockSpec(memory_space=pl.ANY)] + pre_specs[1:] + ins,
            out_specs=[pre_out] + outs,
            out_shape=[pre_shape] + op_shapes,
            scratch_shapes=[pltpu.VMEM((X_SLOTS, tb, d), F32), pltpu.SemaphoreType.DMA((X_SLOTS,))],
            compiler_params=_params(("arbitrary", "arbitrary")),
            name="prenorm_operators",
        )(*pre_args, *op_args)
    hperm = pl.pallas_call(
        _prenorm_kernel, grid=(bsz, nblk), in_specs=pre_specs, out_specs=pre_out,
        out_shape=pre_shape, compiler_params=_params(("parallel", "parallel")), name="prenorm",
    )(*pre_args)
    ins, outs = op_specs(1, lambda q: q)
    return (hperm, *pl.pallas_call(
        _operators_block_kernel, grid=(pairs,), in_specs=ins, out_specs=outs, out_shape=op_shapes,
        compiler_params=_params(("parallel",)), name="operators",
    )(*op_args))


def _ssm_kernel(h_ref, wu_ref, toep_ref, sin_ref, sout_ref, decay_ref, y_ref, xs_ref):
    n_chunks = y_ref.shape[-1] // CHUNK
    n_state = SSM_STATE
    j = pl.program_id(1)
    last = pl.num_programs(1) - 1
    fill, drain = j % 2, (j + 1) % 2

    def project(p):
        r0 = pl.multiple_of(p * n_chunks, n_chunks)
        u = _dot_nt(wu_ref[...], h_ref[pl.ds(r0, n_chunks), :]).astype(BF16)
        c0 = pl.multiple_of(p * SSM_GROUP, SSM_GROUP)
        for g in range(GROUPS_PER_STEP):
            xs_ref[fill, g, pl.ds(c0, SSM_GROUP), :] = u[g * SSM_GROUP:(g + 1) * SSM_GROUP, :]

    row = lax.broadcasted_iota(jnp.int32, (n_chunks, 2 * n_state), 0)

    def shift_rows(v, s):
        if s % 8 == 0:
            return jnp.concatenate([jnp.zeros((s, v.shape[1]), v.dtype), v[:-s]], axis=0)
        return jnp.where(row >= s, pltpu.roll(v, s, 0), 0.0)

    def gains(q):
        g0, g1 = 2 * q, 2 * q + 1
        x0, x1 = xs_ref[drain, g0], xs_ref[drain, g1]
        inc_r = _dot_tn(x0, sin_ref[g0, 0]) + _dot_tn(x1, sin_ref[g1, 0])
        inc_i = _dot_tn(x0, sin_ref[g0, 1]) + _dot_tn(x1, sin_ref[g1, 1])
        return inc_r, inc_i

    def finish(q, inc_r, inc_i):
        g0, g1 = 2 * q, 2 * q + 1
        er, ei = shift_rows(inc_r, 1), shift_rows(inc_i, 1)
        ar, ai = decay_ref[q, 0:1, :], decay_ref[q, 1:2, :]
        s = 1
        while s < n_chunks:
            if s % 8 == 0:
                dr = ar * er[:-s] - ai * ei[:-s]
                di = ar * ei[:-s] + ai * er[:-s]
                er = jnp.concatenate([er[:s], er[s:] + dr], axis=0)
                ei = jnp.concatenate([ei[:s], ei[s:] + di], axis=0)
            else:
                sr, si = shift_rows(er, s), shift_rows(ei, s)
                er, ei = er + (ar * sr - ai * si), ei + (ar * si + ai * sr)
            ar, ai = ar * ar - ai * ai, 2.0 * (ar * ai)
            s *= 2
        sr, si = er.astype(BF16), ei.astype(BF16)
        for g in (g0, g1):
            y = (_dot(toep_ref[g], xs_ref[drain, g]) + _dot_nt(sout_ref[g, 0], sr)
                 + _dot_nt(sout_ref[g, 1], si))
            act = _gelu_tanh(y).astype(BF16)
            row0 = pl.multiple_of(g * SSM_GROUP, SSM_GROUP)
            for p in range(CHUNK):
                y_ref[pl.ds(row0, SSM_GROUP), p * n_chunks:(p + 1) * n_chunks] = (
                    act[p * SSM_GROUP:(p + 1) * SSM_GROUP, :])

    n_iter = GROUPS_PER_STEP // 2 // SSM_UNROLL
    per_iter = CHUNK // n_iter

    def body(it, carry, with_scan, with_projection):
        qs = [it * SSM_UNROLL + u for u in range(SSM_UNROLL)]
        started = [gains(q) for q in qs] if with_scan else []
        if with_projection:
            for pp in range(per_iter):
                project(it * per_iter + pp)
        for q, inc in zip(qs, started):
            finish(q, *inc)
        return carry

    @pl.when(j == 0)
    def _():
        lax.fori_loop(0, n_iter, functools.partial(body, with_scan=False, with_projection=True), 0)

    @pl.when(jnp.logical_and(j > 0, j < last))
    def _():
        lax.fori_loop(0, n_iter, functools.partial(body, with_scan=True, with_projection=True), 0)

    @pl.when(j == last)
    def _():
        lax.fori_loop(0, n_iter, functools.partial(body, with_scan=True, with_projection=False), 0)


def _ssm(hperm, wu_t, toep, state_in, state_out, decay):
    bsz, seq, d = hperm.shape
    e = wu_t.shape[0]
    cb = GROUPS_PER_STEP * SSM_GROUP
    rows = CHUNK * SSM_GROUP
    n_chunks = seq // CHUNK
    gps = GROUPS_PER_STEP
    nblk = e // cb
    assert (gps // 2) % SSM_UNROLL == 0 and CHUNK % (gps // 2 // SSM_UNROLL) == 0
    proj = lambda j: jnp.minimum(j, nblk - 1)
    scan = lambda j: jnp.maximum(j - 1, 0)
    return pl.pallas_call(
        _ssm_kernel,
        grid=(bsz, nblk + 1),
        in_specs=[
            pl.BlockSpec((None, seq, d), lambda b, j: (b, 0, 0)),
            pl.BlockSpec((cb, d), lambda b, j: (proj(j), 0)),
            pl.BlockSpec((gps, rows, rows), lambda b, j: (scan(j), 0, 0)),
            pl.BlockSpec((gps, 2, rows, 2 * SSM_STATE), lambda b, j: (scan(j), 0, 0, 0)),
            pl.BlockSpec((gps, 2, rows, 2 * SSM_STATE), lambda b, j: (scan(j), 0, 0, 0)),
            pl.BlockSpec((gps // 2, 8, 2 * SSM_STATE), lambda b, j: (scan(j), 0, 0)),
        ],
        out_specs=pl.BlockSpec((None, cb, seq), lambda b, j: (b, scan(j), 0)),
        out_shape=jax.ShapeDtypeStruct((bsz, e, seq), BF16),
        scratch_shapes=[pltpu.VMEM((2, gps, rows, n_chunks), BF16)],
        compiler_params=_params(("parallel", "arbitrary")),
        name="ssm",
    )(hperm, wu_t, toep, state_in, state_out, decay)


def _glu_kernel(y_ref, h_ref, wg_ref, bg_ref, wz_ref, o_ref):
    e = y_ref.shape[0]
    rb = 256
    ya = y_ref[...]
    hb = h_ref[...]
    for r in range(e // rb):
        rows = slice(r * rb, (r + 1) * rb)
        gl = _dot(wg_ref[rows, :], ya) + bg_ref[rows, :]
        z = _dot_nt(wz_ref[rows, :], hb)
        yr = y_ref[rows, :].astype(F32)
        gated = yr * _sigmoid(gl) * _silu(z)
        o_ref[:, rows] = gated.T.astype(BF16)


def _glu(y_t, hperm, wglu_t, b_glu, wz_t):
    bsz, e, seq = y_t.shape
    d = hperm.shape[-1]
    tn = GLU_BLOCK
    return pl.pallas_call(
        _glu_kernel,
        grid=(bsz, seq // tn),
        in_specs=[
            pl.BlockSpec((None, e, tn), lambda b, j: (b, 0, j)),
            pl.BlockSpec((None, tn, d), lambda b, j: (b, j, 0)),
            pl.BlockSpec((e, e), lambda b, j: (0, 0)),
            pl.BlockSpec((e, 1), lambda b, j: (0, 0)),
            pl.BlockSpec((e, d), lambda b, j: (0, 0)),
        ],
        out_specs=pl.BlockSpec((None, tn, e), lambda b, j: (b, j, 0)),
        out_shape=jax.ShapeDtypeStruct((bsz, seq, e), BF16),
        compiler_params=_params(("parallel", "parallel")),
        name="glu",
    )(y_t, hperm, wglu_t, b_glu.reshape(e, 1), wz_t)


def _mid_kernel(gp_ref, x_ref, mod0_ref, mod1_ref, p_ref, wo_ref, gpost_ref, gkv_ref, gpre_ref,
                wk_ref, wv_ref, wq_ref, wz_ref, gsub_ref, h_ref, k_ref, vt_ref, qt_ref, gz_ref):
    d = x_ref.shape[-1]
    tb = p_ref.shape[0]
    subs = range(x_ref.shape[0] // tb)
    cps = tb // CHUNK
    gate0 = mod0_ref[:, 2 * d:3 * d]
    shift1 = mod1_ref[:, 0:d]
    scale1 = mod1_ref[:, d:2 * d]
    ones_row = (lax.broadcasted_iota(jnp.int32, (V_ROWS - V_DIM, tb), 0) == 0).astype(BF16)

    ys = []
    for s in subs:
        gp = gp_ref[:, s * cps:(s + 1) * cps, :].reshape(tb, -1)
        ys.append(_dot(_dot(p_ref[...], gp).astype(BF16), wo_ref[...]))
    ins = []
    for s, y in zip(subs, ys):
        rows = slice(s * tb, (s + 1) * tb)
        h = x_ref[rows, :] + gate0 * _rms_rows(y, gpost_ref[...])
        h_ref[rows, :] = h
        kv_in = _rms_rows(h, gkv_ref[...]).astype(BF16)
        h_in = (_rms_rows(h, gpre_ref[...]) * (1.0 + scale1) + shift1).astype(BF16)
        ins.append((kv_in, h_in))
    for s, (kv_in, h_in) in zip(subs, ins):
        k_ref[s * tb:(s + 1) * tb, :] = _dot(kv_in, wk_ref[...]).astype(BF16)
        vt = _dot_nt(wv_ref[...], kv_in).astype(BF16)
        for hd in range(vt.shape[0] // V_DIM):
            vt_ref[s, hd * V_ROWS:hd * V_ROWS + V_DIM, :] = vt[hd * V_DIM:(hd + 1) * V_DIM, :]
            vt_ref[s, hd * V_ROWS + V_DIM:(hd + 1) * V_ROWS, :] = ones_row
        qt_ref[s] = (_dot_nt(wq_ref[...], h_in) * Q_SCALE).astype(BF16)
        gz_ref[s] = (_silu(_dot_nt(wz_ref[...], h_in)) * gsub_ref[...]).astype(BF16)


def _mid(gated_perm, x, mod0, mod1, perm, wo, g_post0, g_kv, g_pre1, wk, wv_t, wq_t, wz_t, gsub):
    bsz, seq, d = x.shape
    tb = TOKEN_BLOCK
    nb = seq // tb
    sub = MID_SUB
    e = gated_perm.shape[-1]
    qk = wk.shape[1]
    av = wv_t.shape[0]
    row = lambda b, j: (b, j, 0)
    const = lambda shape: pl.BlockSpec(shape, lambda b, j: (0, 0), pipeline_mode=pl.Buffered(1))
    t_spec = lambda n: pl.BlockSpec((None, sub, n, tb), lambda b, j: (b, j, 0, 0))
    return pl.pallas_call(
        _mid_kernel,
        grid=(bsz, nb // sub),
        in_specs=[
            pl.BlockSpec((None, CHUNK, sub * tb // CHUNK, e), lambda b, j: (b, 0, j, 0)),
            pl.BlockSpec((None, sub * tb, d), row),
            pl.BlockSpec((None, 1, 3 * d), lambda b, j: (b, 0, 0)),
            pl.BlockSpec((None, 1, 3 * d), lambda b, j: (b, 0, 0)),
            const((tb, tb)),
            const((e, d)),
            const((1, d)),
            const((1, d)),
            const((1, d)),
            const((d, qk)),
            const((av, d)),
            const((qk, d)),
            const((av, d)),
            const((av, 1)),
        ],
        out_specs=[
            pl.BlockSpec((None, sub * tb, d), row),
            pl.BlockSpec((None, sub * tb, qk), row),
            t_spec(av // V_DIM * V_ROWS),
            t_spec(qk),
            t_spec(av),
        ],
        out_shape=[
            jax.ShapeDtypeStruct((bsz, seq, d), F32),
            jax.ShapeDtypeStruct((bsz, seq, qk), BF16),
            jax.ShapeDtypeStruct((bsz, nb, av // V_DIM * V_ROWS, tb), BF16),
            jax.ShapeDtypeStruct((bsz, nb, qk, tb), BF16),
            jax.ShapeDtypeStruct((bsz, nb, av, tb), BF16),
        ],
        compiler_params=_params(("parallel", "parallel")),
        name="mid",
    )(gated_perm.reshape(bsz, CHUNK, seq // CHUNK, e), x, mod0.reshape(bsz, 1, 3 * d),
      mod1.reshape(bsz, 1, 3 * d), perm, wo, g_post0.reshape(1, d), g_kv.reshape(1, d),
      g_pre1.reshape(1, d), wk, wv_t, wq_t, wz_t, gsub.reshape(av, 1))


def _attention_kernel(lam_ref, qt_ref, qn_ref, k_ref, vt_ref, gz_ref, bias_ref, h_ref, mod1_ref, wo_ref,
                      gpost_ref, o_ref, qp_ref, m_ref, acc_ref, s_ref, og_ref):
    tq = qt_ref.shape[-1]
    hw = 2 * HEAD_DIM
    heads = qt_ref.shape[0] // hw
    i = pl.program_id(1)

    zero = jnp.zeros((HEAD_DIM, tq), qt_ref.dtype)

    def pad_queries(q_ref, n_heads):
        for h in range(n_heads):
            qt = q_ref[h * hw:(h + 1) * hw, :]
            qp_ref[2 * h] = jnp.concatenate([qt[:HEAD_DIM], zero], axis=0)
            qp_ref[2 * h + 1] = jnp.concatenate([zero, qt[HEAD_DIM:]], axis=0)

    pad_queries(qt_ref, heads)
    n_strips = 2 * heads

    def scores(j, n):
        h = n // 2
        row0 = pl.multiple_of(j * tq, tq)
        return _dot(k_ref[pl.ds(row0, tq), h * hw:(h + 1) * hw], qp_ref[n])

    def step(j, last=False):
        for n in range(n_strips):
            s = s_ref[n % ATT_AHEAD]
            if n + ATT_AHEAD < n_strips:
                s_ref[n % ATT_AHEAD] = scores(j, n + ATT_AHEAD)
            elif not last:
                s_ref[n % ATT_AHEAD] = scores(j + 1, n + ATT_AHEAD - n_strips)
            m_old = m_ref[n]
            if last:
                hq = tq // 2
                s00 = s[:hq, :hq] + bias_ref[...]
                s01 = s[:hq, hq:]
                s11 = s[hq:, hq:] + bias_ref[...]
                m_blk = jnp.concatenate(
                    [jnp.max(s00, axis=0, keepdims=True),
                     jnp.maximum(jnp.max(s01, axis=0, keepdims=True),
                                 jnp.max(s11, axis=0, keepdims=True))], axis=1)
                m_new = jnp.maximum(m_old, m_blk)
                e = lambda v, m: jnp.exp2((v - m).astype(BF16))
                p = jnp.concatenate(
                    [jnp.concatenate([e(s00, m_new[:, :hq]), e(s01, m_new[:, hq:])], axis=1),
                     jnp.concatenate([jnp.zeros((hq, hq), BF16), e(s11, m_new[:, hq:])], axis=1)],
                    axis=0)
            else:
                m_new = jnp.maximum(m_old, jnp.max(s, axis=0, keepdims=True))
                p = jnp.exp2((s - m_new).astype(BF16))
            alpha = jnp.exp2(m_old - m_new)
            m_ref[n] = m_new
            h = n // 2
            vt = vt_ref[j, h * V_ROWS:(h + 1) * V_ROWS, :]
            acc_ref[n] = alpha * acc_ref[n] + _dot(vt, p)

    @pl.when(i == 0)
    def _():
        for n in range(ATT_AHEAD):
            s_ref[n] = scores(0, n)

    m_ref[...] = jnp.full(m_ref.shape, -jnp.inf, F32)
    acc_ref[...] = jnp.zeros(acc_ref.shape, F32)

    @pl.when(i == 0)
    def _():
        step(0, last=True)

    @pl.when(i > 0)
    def _():
        first = (i - 1) % ATT_BLOCKS
        for lead in range(1, ATT_BLOCKS):
            @pl.when(first == lead)
            def _():
                for j in range(lead):
                    step(j)

        def grouped_steps(t, carry):
            for u in range(ATT_BLOCKS):
                step(first + ATT_BLOCKS * t + u)
            return carry

        lax.fori_loop(0, (i - 1) // ATT_BLOCKS, grouped_steps, 0)
        step(i - 1)
        step(i, last=True)

    pad_queries(qn_ref, ATT_AHEAD // 2)
    for n in range(ATT_AHEAD):
        s_ref[n] = scores(0, n)

    for h in range(heads):
        r0 = 1.0 / acc_ref[2 * h, V_DIM:V_DIM + 1, :]
        r1 = lam_ref[0] / acc_ref[2 * h + 1, V_DIM:V_DIM + 1, :]
        o = acc_ref[2 * h, :V_DIM, :] * r0 - acc_ref[2 * h + 1, :V_DIM, :] * r1
        rows = slice(h * V_DIM, (h + 1) * V_DIM)
        inv_rms = lax.rsqrt(jnp.mean(o * o, axis=0, keepdims=True) + EPS)
        og_ref[rows, :] = (o * inv_rms * gz_ref[rows, :].astype(F32)).astype(BF16)
        if h % 2 == 1:
            pr = slice((h - 1) * V_DIM, (h + 1) * V_DIM)
            part = _dot_tn(og_ref[pr, :], wo_ref[pr, :])
            y = part if h == 1 else y + part
    d = h_ref.shape[-1]
    gate1 = mod1_ref[:, 2 * d:3 * d]
    o_ref[...] = h_ref[...] + gate1 * _rms_rows(y, gpost_ref[...])


def _attention(lam, q_t, k, v_t, gz_t, h, mod1, wo, g_post1):
    bsz, nb, width, tb = q_t.shape
    seq = nb * tb
    d = h.shape[-1]
    heads = width // (2 * HEAD_DIM)
    assert heads % 2 == 0 and (2 * heads) % ATT_AHEAD == 0
    pos = jnp.arange(tb // 2)
    bias = jnp.where(pos[:, None] <= pos[None, :], 0.0, -jnp.inf).astype(F32)
    blk = pl.BlockSpec((None, None, width, tb), lambda b, i: (b, i, 0, 0))
    once = pl.Buffered(1)
    return pl.pallas_call(
        _attention_kernel,
        grid=(bsz, nb),
        in_specs=[
            pl.BlockSpec(memory_space=pltpu.SMEM),
            blk,
            pl.BlockSpec((None, None, width, tb), lambda b, i: (b, jnp.minimum(i + 1, nb - 1), 0, 0)),
            pl.BlockSpec((None, seq, width), lambda b, i: (b, 0, 0)),
            pl.BlockSpec((None, nb, heads * V_ROWS, tb), lambda b, i: (b, 0, 0, 0)),
            blk,
            pl.BlockSpec((tb // 2, tb // 2), lambda b, i: (0, 0), pipeline_mode=once),
            pl.BlockSpec((None, tb, d), lambda b, i: (b, i, 0)),
            pl.BlockSpec((None, 1, 3 * d), lambda b, i: (b, 0, 0)),
            pl.BlockSpec((width, d), lambda b, i: (0, 0), pipeline_mode=once),
            pl.BlockSpec((1, d), lambda b, i: (0, 0), pipeline_mode=once),
        ],
        out_specs=pl.BlockSpec((None, tb, d), lambda b, i: (b, i, 0)),
        out_shape=jax.ShapeDtypeStruct((bsz, seq, d), F32),
        scratch_shapes=[
            pltpu.VMEM((2 * heads, 2 * HEAD_DIM, tb), BF16),
            pltpu.VMEM((2 * heads, 1, tb), F32),
            pltpu.VMEM((2 * heads, V_ROWS, tb), F32),
            pltpu.VMEM((ATT_AHEAD, tb, tb), F32),
            pltpu.VMEM((width, tb), BF16),
        ],
        compiler_params=pltpu.CompilerParams(dimension_semantics=("parallel", "arbitrary"),
                                             vmem_limit_bytes=ATT_VMEM_LIMIT),
        name="attention",
    )(lam, q_t, q_t, k, v_t, gz_t, bias, h, mod1.reshape(bsz, 1, 3 * d), wo, g_post1.reshape(1, d))


def kernel(x, c, ada_w, ada_b, g_pre, g_post, a_w_in, a_lam_re, a_lam_im, a_log_dt, a_b_re, a_b_im,
           a_c_re, a_c_im, a_d, a_w_glu, a_b_glu, a_w_out, g_kv, w_k, w_v, b_w_in, b_lq1, b_lk1,
           b_lq2, b_lk2, b_g_sub, b_w_out):
    bsz, seq, d = x.shape
    e = a_w_glu.shape[1]
    qk = w_k.shape[1]
    assert seq % (CHUNK * 128) == 0 and seq % GLU_BLOCK == 0 and d % 128 == 0
    assert e % (GROUPS_PER_STEP * SSM_GROUP) == 0

    mod = _modulation(c, ada_w, ada_b)
    perm = _chunk_permutation(TOKEN_BLOCK)

    hperm, toep, state_in, state_out, decay = _prenorm_and_operators(
        x, mod[0], g_pre[0], perm,
        a_lam_re[0], a_lam_im[0], a_log_dt[0], a_b_re[0], a_b_im[0], a_c_re[0], a_c_im[0], a_d[0])
    hperm = hperm.reshape(bsz, seq, d)
    w_in_t = a_w_in[0].T.astype(BF16)
    y_t = _ssm(hperm, w_in_t[:e], toep, state_in, state_out, decay)
    gated = _glu(y_t, hperm, a_w_glu[0].T.astype(BF16), a_b_glu[0], w_in_t[e:])

    layer = DEPTH // 2
    lambda_init = 0.8 - 0.6 * math.exp(-0.3 * layer)
    w_b_t = b_w_in[0].T.astype(BF16)
    gsub = jnp.tile(b_g_sub[0] * (1.0 - lambda_init), w_v.shape[1] // V_DIM)
    h, k, v_t, q_t, gz_t = _mid(
        gated, x, mod[0], mod[1], perm, a_w_out[0].astype(BF16), g_post[0], g_kv, g_pre[1],
        w_k.astype(BF16), w_v.T.astype(BF16), w_b_t[:qk], w_b_t[qk:], gsub)

    lam = (jnp.exp(jnp.sum(b_lq1[0] * b_lk1[0])) - jnp.exp(jnp.sum(b_lq2[0] * b_lk2[0]))
           + lambda_init).reshape(1).astype(F32)
    return _attention(lam, q_t, k, v_t, gz_t, h, mod[1], b_w_out[0].astype(BF16), g_post[1])
```
